```python
import jax
import jax.numpy as jnp
from jax import lax
import numpy as np

D_MODEL = 2048
BATCH = 4
SEQ = 2048
DEPTH = 1
DEC_BATCH = 16
DEC_SEQ = 64
PAST_LEN = 4096

CHUNK = 64
MIX_WIDTH = D_MODEL
GLA_WIDTH = MIX_WIDTH // 2
GLA_HEADS = 4
GLA_DV = GLA_WIDTH // GLA_HEADS
GLA_DK = GLA_DV // 2
GLA_GATE_RANK = 16
GLA_GATE_NORM = 16.0
GDN_WIDTH = MIX_WIDTH - GLA_WIDTH
GDN_HEADS = 8
GDN_DV = GDN_WIDTH // GDN_HEADS
GDN_DK = 128
GDN_CONV = 4
GDN_CONV_DIM = 2 * GDN_HEADS * GDN_DK + GDN_WIDTH
D_FF = 5632
FFN_CONV = 3
N_MOD = 6
EPS = 1e-6
IN_SIZES = (GLA_HEADS * GLA_DK, GLA_HEADS * GLA_DK, GLA_WIDTH, GLA_WIDTH, GLA_GATE_RANK,
            GDN_HEADS * GDN_DK, GDN_HEADS * GDN_DK, GDN_WIDTH, GDN_WIDTH, GDN_HEADS, GDN_HEADS)
D_IN = sum(IN_SIZES)
IN_OFFSETS = tuple(int(v) for v in np.cumsum(IN_SIZES)[:-1])

kernel_name = 'hymba_gla_gdn_convglu_adaln_stream'


def rmsnorm(x, gain):
    xf = x.astype(jnp.float32)
    y = xf * lax.rsqrt(jnp.mean(xf * xf, axis=-1, keepdims=True) + EPS)
    return (y * gain.astype(jnp.float32)).astype(x.dtype)


def l2norm(x):
    return x * lax.rsqrt(jnp.sum(x * x, axis=-1, keepdims=True) + EPS)


def causal_dwconv(x, buf, w):
    k_w, seq = w.shape[0], x.shape[1]
    xp = jnp.concatenate([buf.astype(x.dtype), x], axis=1)
    y = xp[:, 0:seq] * w[0]
    for i in range(1, k_w):
        y = y + xp[:, i:i + seq] * w[i]
    return y, xp[:, seq:]


def to_heads(t, n_heads):
    b, l, _ = t.shape
    return t.reshape(b, l, n_heads, -1).transpose(0, 2, 1, 3).astype(jnp.float32)


def from_heads(o, gain):
    b, h, l, d = o.shape
    o = o.transpose(0, 2, 1, 3)
    o = o * lax.rsqrt(jnp.mean(o * o, axis=-1, keepdims=True) + EPS) * gain.astype(jnp.float32)
    return o.reshape(b, l, h * d)


def blocked_recurrence(step, s0, seqs, block):
    b, h, l = seqs[0].shape[:3]
    n = l // block
    xs = tuple(jnp.moveaxis(a.reshape(b, h, n, block, a.shape[-1]), 2, 0) for a in seqs)
    s, o = lax.scan(step, s0.astype(jnp.float32), xs)
    o = jnp.moveaxis(o, 0, 2).reshape(b, h, l, o.shape[-1])
    return o, s


def gla_step(s, blk):
    q, k, v, g = blk
    c = q.shape[2]
    big_g = jnp.cumsum(g, axis=2)
    causal = jnp.tril(jnp.ones((c, c), dtype=bool))
    diff = big_g[:, :, :, None, :] - big_g[:, :, None, :, :]
    decay = jnp.exp(jnp.where(causal[:, :, None], diff, -jnp.inf))
    attn = jnp.einsum('bhid,bhjd,bhijd->bhij', q, k, decay)
    o = (jnp.einsum('bhij,bhjv->bhiv', attn, v)
         + jnp.einsum('bhid,bhdv->bhiv', q * jnp.exp(big_g), s))
    g_last = big_g[:, :, -1:, :]
    s_new = (jnp.exp(g_last[:, :, 0, :, None]) * s
             + jnp.einsum('bhjd,bhjv->bhdv', k * jnp.exp(g_last - big_g), v))
    return s_new, o


def gdn_step(s, blk):
    q, k, v, g, beta = blk
    c, dv = q.shape[2], v.shape[-1]
    big_g = jnp.cumsum(g[..., 0], axis=-1)
    causal = jnp.tril(jnp.ones((c, c), dtype=bool))
    strict = jnp.tril(jnp.ones((c, c), dtype=bool), -1)
    decay = jnp.exp(jnp.where(causal, big_g[..., :, None] - big_g[..., None, :], -jnp.inf))
    kb = k * beta
    lower = jnp.where(strict, jnp.einsum('bhid,bhjd->bhij', kb, k) * decay, 0.0)
    eye = jnp.eye(c, dtype=jnp.float32)
    rhs = jnp.concatenate([v * beta, kb * jnp.exp(big_g)[..., None]], axis=-1)
    sol = lax.linalg.triangular_solve(eye + lower, rhs, left_side=True, lower=True,
                                      unit_diagonal=True)
    u = sol[..., :dv] - jnp.einsum('bhik,bhkv->bhiv', sol[..., dv:], s)
    attn = jnp.einsum('bhid,bhjd->bhij', q, k) * decay
    o = (jnp.einsum('bhid,bhdv->bhiv', q * jnp.exp(big_g)[..., None], s)
         + jnp.einsum('bhij,bhjv->bhiv', attn, u))
    g_last = big_g[..., -1:]
    s_new = (jnp.exp(g_last)[..., None] * s
             + jnp.einsum('bhjd,bhjv->bhdv', k * jnp.exp(g_last - big_g)[..., None], u))
    return s_new, o


def token_mixers(h, st_gla, st_gdn, st_conv, w_in, gla_wg, gla_bg, gla_norm, gdn_conv_w,
                 gdn_a_log, gdn_dt_bias, gdn_norm, w_out, block):
    f32 = jnp.float32
    proj = h @ w_in
    (a_q, a_k, a_v, a_r, a_lr, b_q, b_k, b_v, b_g, b_beta, b_alpha) = jnp.split(
        proj, IN_OFFSETS, axis=-1)
    log_a = jax.nn.log_sigmoid((a_lr @ gla_wg + gla_bg).astype(f32)) / GLA_GATE_NORM
    o_a, s_gla = blocked_recurrence(
        gla_step, st_gla,
        (to_heads(a_q, GLA_HEADS) * GLA_DK ** -0.5, to_heads(a_k, GLA_HEADS),
         to_heads(a_v, GLA_HEADS), to_heads(log_a, GLA_HEADS)), block)
    o_a = from_heads(o_a, gla_norm) * jax.nn.silu(a_r.astype(f32))
    qkv, conv_new = causal_dwconv(jnp.concatenate([b_q, b_k, b_v], axis=-1), st_conv, gdn_conv_w)
    qkv = jax.nn.silu(qkv)
    c_q, c_k, c_v = jnp.split(qkv, [GDN_HEADS * GDN_DK, 2 * GDN_HEADS * GDN_DK], axis=-1)
    q = l2norm(to_heads(c_q, GDN_HEADS)) * GDN_DK ** -0.5
    k = l2norm(to_heads(c_k, GDN_HEADS))
    v = to_heads(c_v, GDN_HEADS)
    beta = jax.nn.sigmoid(to_heads(b_beta, GDN_HEADS))
    g = -jnp.exp(gdn_a_log.astype(f32))[:, None, None] * jax.nn.softplus(
        to_heads(b_alpha, GDN_HEADS) + gdn_dt_bias.astype(f32)[:, None, None])
    o_b, s_gdn = blocked_recurrence(gdn_step, st_gdn, (q, k, v, g, beta), block)
    o_b = from_heads(o_b, gdn_norm) * jax.nn.silu(b_g.astype(f32))
    out = jnp.concatenate([o_a, o_b], axis=-1).astype(h.dtype) @ w_out
    return out, s_gla, s_gdn, conv_new


def conv_ffn(h, st, w_up, ffn_conv_w, ffn_conv_b, w_down):
    gate, val = jnp.split(h @ w_up, 2, axis=-1)
    gate_c, new_st = causal_dwconv(gate, st, ffn_conv_w)
    act = jax.nn.silu(gate_c + ffn_conv_b) * val
    return act @ w_down, new_st


def trunk(x, c, states, params, block):
    (w_ada, b_ada, norm1, w_in, gla_wg, gla_bg, gla_norm, gdn_conv_w, gdn_a_log, gdn_dt_bias,
     gdn_norm, w_out, norm2, w_up, ffn_conv_w, ffn_conv_b, w_down, final_norm) = params
    gla_s, gdn_s, conv_s, ffn_s = states
    n_gla, n_gdn, n_conv, n_ffn = [], [], [], []
    for l in range(DEPTH):
        mod = jax.nn.silu(c) @ w_ada[l] + b_ada[l]
        sh1, sc1, g1, sh2, sc2, g2 = jnp.split(mod[:, None, :], N_MOD, axis=-1)
        h = rmsnorm(x, norm1[l]) * (1 + sc1) + sh1
        mix, s_a, s_b, s_c = token_mixers(
            h, gla_s[l], gdn_s[l], conv_s[l], w_in[l], gla_wg[l], gla_bg[l], gla_norm[l],
            gdn_conv_w[l], gdn_a_log[l], gdn_dt_bias[l], gdn_norm[l], w_out[l], block)
        x = x + g1 * mix
        h = rmsnorm(x, norm2[l]) * (1 + sc2) + sh2
        f, s_f = conv_ffn(h, ffn_s[l], w_up[l], ffn_conv_w[l], ffn_conv_b[l], w_down[l])
        x = x + g2 * f
        n_gla.append(s_a)
        n_gdn.append(s_b)
        n_conv.append(s_c)
        n_ffn.append(s_f)
    y = rmsnorm(x, final_norm)
    return y, jnp.stack(n_gla), jnp.stack(n_gdn), jnp.stack(n_conv), jnp.stack(n_ffn)


def setup_inputs(seed: int = 0) -> dict:
    key = jax.random.key(seed)
    ks = jax.random.split(key, 32)
    f32 = jnp.float32

    def nrm(k, shape, scale):
        return jax.random.normal(k, shape, f32) * scale

    d = D_MODEL
    dt = jnp.exp(jax.random.uniform(ks[15], (DEPTH, GDN_HEADS), f32,
                                    minval=float(np.log(1e-3)), maxval=float(np.log(1e-1))))
    return {
        'x_prompt': nrm(ks[0], (BATCH, SEQ, d), 1.0),
        'x_sample': nrm(ks[1], (DEC_BATCH, DEC_SEQ, d), 1.0),
        'c_prompt': nrm(ks[2], (BATCH, d), 1.0),
        'c_sample': nrm(ks[3], (DEC_BATCH, d), 1.0),
        'state_gla': nrm(ks[4], (DEPTH, DEC_BATCH, GLA_HEADS, GLA_DK, GLA_DV), GLA_DK ** -0.5),
        'state_gdn': nrm(ks[5], (DEPTH, DEC_BATCH, GDN_HEADS, GDN_DK, GDN_DV), GDN_DK ** -0.5),
        'state_gdn_conv': nrm(ks[6], (DEPTH, DEC_BATCH, GDN_CONV - 1, GDN_CONV_DIM), 1.0),
        'state_ffn_conv': nrm(ks[7], (DEPTH, DEC_BATCH, FFN_CONV - 1, D_FF), 1.0),
        'w_ada': nrm(ks[8], (DEPTH, d, N_MOD * d), d ** -0.5),
        'b_ada': nrm(ks[9], (DEPTH, N_MOD * d), 0.02),
        'norm1': 1.0 + nrm(ks[10], (DEPTH, d), 0.02),
        'w_in': nrm(ks[11], (DEPTH, d, D_IN), d ** -0.5),
        'gla_wg': nrm(ks[12], (DEPTH, GLA_GATE_RANK, GLA_HEADS * GLA_DK), GLA_GATE_RANK ** -0.5),
        'gla_bg': nrm(ks[13], (DEPTH, GLA_HEADS * GLA_DK), 0.02),
        'gla_norm': 1.0 + nrm(ks[14], (DEPTH, GLA_DV), 0.02),
        'gdn_conv_w': nrm(ks[16], (DEPTH, GDN_CONV, GDN_CONV_DIM), GDN_CONV ** -0.5),
        'gdn_a_log': jnp.log(jax.random.uniform(ks[17], (DEPTH, GDN_HEADS), f32,
                                                minval=1.0, maxval=16.0)),
        'gdn_dt_bias': dt + jnp.log(-jnp.expm1(-dt)),
        'gdn_norm': 1.0 + nrm(ks[18], (DEPTH, GDN_DV), 0.02),
        'w_out': nrm(ks[19], (DEPTH, MIX_WIDTH, d), MIX_WIDTH ** -0.5),
        'norm2': 1.0 + nrm(ks[20], (DEPTH, d), 0.02),
        'w_up': nrm(ks[21], (DEPTH, d, 2 * D_FF), d ** -0.5),
        'ffn_conv_w': nrm(ks[22], (DEPTH, FFN_CONV, D_FF), FFN_CONV ** -0.5),
        'ffn_conv_b': nrm(ks[23], (DEPTH, D_FF), 0.02),
        'w_down': nrm(ks[24], (DEPTH, D_FF, d), D_FF ** -0.5),
        'final_norm': 1.0 + nrm(ks[25], (d,), 0.02),
    }


def reference(x_prompt, x_sample, c_prompt, c_sample, state_gla, state_gdn, state_gdn_conv,
              state_ffn_conv, w_ada, b_ada, norm1, w_in, gla_wg, gla_bg, gla_norm, gdn_conv_w,
              gdn_a_log, gdn_dt_bias, gdn_norm, w_out, norm2, w_up, ffn_conv_w, ffn_conv_b,
              w_down, final_norm):
    params = (w_ada, b_ada, norm1, w_in, gla_wg, gla_bg, gla_norm, gdn_conv_w, gdn_a_log,
              gdn_dt_bias, gdn_norm, w_out, norm2, w_up, ffn_conv_w, ffn_conv_b, w_down,
              final_norm)
    b = x_prompt.shape[0]
    fresh = (jnp.zeros((DEPTH, b, GLA_HEADS, GLA_DK, GLA_DV), jnp.float32),
             jnp.zeros((DEPTH, b, GDN_HEADS, GDN_DK, GDN_DV), jnp.float32),
             jnp.zeros((DEPTH, b, GDN_CONV - 1, GDN_CONV_DIM), x_prompt.dtype),
             jnp.zeros((DEPTH, b, FFN_CONV - 1, D_FF), x_prompt.dtype))
    y_prompt, p_gla, p_gdn, p_conv, p_ffn = trunk(x_prompt, c_prompt, fresh, params, CHUNK)
    y_sample, s_gla, s_gdn, s_conv, s_ffn = trunk(
        x_sample, c_sample, (state_gla, state_gdn, state_gdn_conv, state_ffn_conv), params,
        x_sample.shape[1])
    return (y_prompt, y_sample, p_gla, p_gdn, p_conv, p_ffn, s_gla, s_gdn, s_conv, s_ffn)
```

```python
import functools

import jax
import jax.numpy as jnp
from jax import lax
from jax.experimental import pallas as pl
from jax.experimental.pallas import tpu as pltpu

F32 = jnp.float32
BF16 = jnp.bfloat16

D_MODEL = 2048
CHUNK = 64
GLA_HEADS = 4
GLA_DK = 128
GLA_DV = 256
GLA_WIDTH = GLA_HEADS * GLA_DV
GLA_GATE_RANK = 16
GLA_GATE_NORM = 16.0
GDN_HEADS = 8
GDN_DK = 128
GDN_DV = 128
GDN_WIDTH = GDN_HEADS * GDN_DV
GDN_CONV = 4
GDN_CONV_DIM = 3 * GDN_WIDTH
D_FF = 5632
FFN_CONV = 3
N_MOD = 6
EPS = 1e-6

GLA_COLS = 2 * GLA_HEADS * GLA_DK + 2 * GLA_WIDTH
GDN_COLS = 4 * GDN_WIDTH
MAIN_COLS = GLA_COLS + GDN_COLS
SMALL_COLS = 128
LR_OFF, BETA_OFF, ALPHA_OFF = 0, GLA_GATE_RANK, GLA_GATE_RANK + GDN_HEADS
IN_OFF_LR = GLA_COLS
IN_OFF_GDN = IN_OFF_LR + GLA_GATE_RANK
IN_OFF_BETA = IN_OFF_GDN + GDN_COLS

SUBLANES = 8
VMEM_LIMIT_BYTES = 56 * 1024 * 1024

NT_DIMS = (((1,), (1,)), ((), ()))
TN_DIMS = (((0,), (0,)), ((), ()))


def _dot(a, b):
    return jnp.dot(a, b, preferred_element_type=F32)


def _dot_nt(a, b):
    return lax.dot_general(a, b, NT_DIMS, preferred_element_type=F32)


def _dot_tn(a, b):
    return lax.dot_general(a, b, TN_DIMS, preferred_element_type=F32)


def _dot_f32(a, b):
    return jnp.dot(a, b, preferred_element_type=F32, precision=lax.Precision.HIGHEST)


def _split_bf16(a):
    hi = a.astype(BF16)
    lo = (a - hi.astype(F32)).astype(BF16)
    return hi, lo


def _dot_x3(a, b):
    a_hi, a_lo = _split_bf16(a)
    b_hi, b_lo = _split_bf16(b)
    return _dot(a_hi, b_hi) + (_dot(a_hi, b_lo) + _dot(a_lo, b_hi))


def _silu(x):
    return x * jax.nn.sigmoid(x)


def _softplus(x):
    return jnp.maximum(x, 0.0) + jnp.log1p(jnp.exp(-jnp.abs(x)))


def _log_sigmoid(x):
    return jnp.minimum(x, 0.0) - jnp.log1p(jnp.exp(-jnp.abs(x)))


def _params(*semantics):
    return pltpu.CompilerParams(dimension_semantics=semantics, vmem_limit_bytes=VMEM_LIMIT_BYTES)


def _tri_masks():
    row = lax.broadcasted_iota(jnp.int32, (CHUNK, CHUNK), 0)
    col = lax.broadcasted_iota(jnp.int32, (CHUNK, CHUNK), 1)
    return row >= col, row > col


def _ada_kernel(c_ref, w_ref, b_ref, o_ref):
    s = _silu(c_ref[...]).astype(BF16)
    o_ref[...] = _dot(s, w_ref[...].astype(BF16)) + b_ref[...]


def _ada_mod(c_all, w_ada, b_ada):
    rows = c_all.shape[0]
    n = w_ada.shape[1]
    tn = 1024
    return pl.pallas_call(
        _ada_kernel,
        grid=(n // tn,),
        in_specs=[
            pl.BlockSpec((rows, D_MODEL), lambda j: (0, 0)),
            pl.BlockSpec((D_MODEL, tn), lambda j: (0, j)),
            pl.BlockSpec((1, tn), lambda j: (0, j)),
        ],
        out_specs=pl.BlockSpec((rows, tn), lambda j: (0, j)),
        out_shape=jax.ShapeDtypeStruct((rows, n), F32),
        compiler_params=_params("arbitrary"),
        name="ada_mod",
    )(c_all, w_ada, b_ada.reshape(1, n))


def _in_proj_kernel(x_ref, sc_ref, sh_ref, n1_ref, w_ref, ws_ref, o_ref, os_ref, h_scr):
    bt, lt, d = x_ref.shape

    @pl.when(pl.program_id(2) == 0)
    def _():
        x = x_ref[...]
        y = x * lax.rsqrt(jnp.mean(x * x, axis=-1, keepdims=True) + EPS) * n1_ref[...]
        h = y * (1.0 + sc_ref[...]) + sh_ref[...]
        hb = h.reshape(bt * lt, d).astype(BF16)
        h_scr[...] = hb
        os_ref[...] = _dot(hb, ws_ref[...]).reshape(os_ref.shape)

    o_ref[...] = _dot(h_scr[...], w_ref[...]).reshape(o_ref.shape)


def _in_proj(x, mod3, mod_row0, norm1, w_main, w_small, bt, lt):
    b, l, d = x.shape
    tn = 1024
    mrow = mod_row0 // bt
    grid = (b // bt, l // lt, MAIN_COLS // tn)
    return pl.pallas_call(
        _in_proj_kernel,
        grid=grid,
        in_specs=[
            pl.BlockSpec((bt, lt, d), lambda i, t, j: (i, t, 0)),
            pl.BlockSpec((bt, 1, d), lambda i, t, j: (mrow + i, 0, 1)),
            pl.BlockSpec((bt, 1, d), lambda i, t, j: (mrow + i, 0, 0)),
            pl.BlockSpec((1, 1, d), lambda i, t, j: (0, 0, 0)),
            pl.BlockSpec((d, tn), lambda i, t, j: (0, j)),
            pl.BlockSpec((d, SMALL_COLS), lambda i, t, j: (0, 0)),
        ],
        out_specs=[
            pl.BlockSpec((bt, lt, tn), lambda i, t, j: (i, t, j)),
            pl.BlockSpec((bt, lt, SMALL_COLS), lambda i, t, j: (i, t, 0)),
        ],
        out_shape=[
            jax.ShapeDtypeStruct((b, l, MAIN_COLS), F32),
            jax.ShapeDtypeStruct((b, l, SMALL_COLS), F32),
        ],
        scratch_shapes=[pltpu.VMEM((bt * lt, d), BF16)],
        compiler_params=_params("arbitrary", "arbitrary", "arbitrary"),
        name="in_proj",
    )(x, mod3, mod3, norm1.reshape(1, 1, d), w_main, w_small)


def _gla_kernel(q_ref, k_ref, v_ref, r_ref, sm_ref, wg_ref, bg_ref, gn_ref, s0_ref,
                o_ref, s_ref, st_scr):
    t = pl.program_id(1)
    lt = q_ref.shape[1]
    causal, _ = _tri_masks()
    tri = causal.astype(F32)
    scale = GLA_DK ** -0.5

    @pl.when(t == 0)
    def _():
        for h in range(GLA_HEADS):
            st_scr[h] = s0_ref[0, h].T

    wg = wg_ref[...].astype(BF16)
    for c in range(lt // CHUNK):
        rows = pl.ds(c * CHUNK, CHUNK)
        a_lr = sm_ref[0, rows, LR_OFF:LR_OFF + GLA_GATE_RANK].astype(BF16)
        log_a = _log_sigmoid(_dot(a_lr, wg) + bg_ref[...]) / GLA_GATE_NORM
        big_g = _dot_f32(tri, log_a)
        g_mid = big_g[CHUNK // 2 - 1:CHUNK // 2, :]
        g_last = big_g[CHUNK - 1:CHUNK, :]
        e_q = jnp.exp(big_g - g_mid)
        e_k = jnp.exp(g_mid - big_g)
        e_g = jnp.exp(big_g)
        e_kl = jnp.exp(g_last - big_g)
        e_l = jnp.exp(g_last)
        for h in range(GLA_HEADS):
            kc = slice(h * GLA_DK, (h + 1) * GLA_DK)
            vc = slice(h * GLA_DV, (h + 1) * GLA_DV)
            q = q_ref[0, rows, kc] * scale
            k = k_ref[0, rows, kc]
            v = v_ref[0, rows, vc].astype(BF16)
            st = st_scr[h]
            attn = _dot_nt((q * e_q[:, kc]).astype(BF16), (k * e_k[:, kc]).astype(BF16))
            attn = jnp.where(causal, attn, 0.0)
            o = _dot(attn.astype(BF16), v) + _dot_nt((q * e_g[:, kc]).astype(BF16), st.astype(BF16))
            st_scr[h] = st * e_l[:, kc] + _dot_tn(v, (k * e_kl[:, kc]).astype(BF16))
            o = o * lax.rsqrt(jnp.mean(o * o, axis=-1, keepdims=True) + EPS) * gn_ref[...]
            o = o * _silu(r_ref[0, rows, vc])
            o_ref[0, rows, vc] = o.astype(o_ref.dtype)

    @pl.when(t == pl.num_programs(1) - 1)
    def _():
        for h in range(GLA_HEADS):
            s_ref[0, h] = st_scr[h].T


def _gla_mix(proj, small, s0, gla_wg, gla_bg, gla_norm, lt):
    b, l, _ = proj.shape
    kw = GLA_HEADS * GLA_DK
    return pl.pallas_call(
        _gla_kernel,
        grid=(b, l // lt),
        in_specs=[
            pl.BlockSpec((1, lt, kw), lambda i, t: (i, t, 0)),
            pl.BlockSpec((1, lt, kw), lambda i, t: (i, t, 1)),
            pl.BlockSpec((1, lt, GLA_WIDTH), lambda i, t: (i, t, 1)),
            pl.BlockSpec((1, lt, GLA_WIDTH), lambda i, t: (i, t, 2)),
            pl.BlockSpec((1, lt, SMALL_COLS), lambda i, t: (i, t, 0)),
            pl.BlockSpec((GLA_GATE_RANK, kw), lambda i, t: (0, 0)),
            pl.BlockSpec((1, kw), lambda i, t: (0, 0)),
            pl.BlockSpec((1, GLA_DV), lambda i, t: (0, 0)),
            pl.BlockSpec((1, GLA_HEADS, GLA_DK, GLA_DV), lambda i, t: (i, 0, 0, 0)),
        ],
        out_specs=[
            pl.BlockSpec((1, lt, GLA_WIDTH), lambda i, t: (i, t, 0)),
            pl.BlockSpec((1, GLA_HEADS, GLA_DK, GLA_DV), lambda i, t: (i, 0, 0, 0)),
        ],
        out_shape=[
            jax.ShapeDtypeStruct((b, l, GLA_WIDTH), BF16),
            jax.ShapeDtypeStruct((b, GLA_HEADS, GLA_DK, GLA_DV), F32),
        ],
        scratch_shapes=[pltpu.VMEM((GLA_HEADS, GLA_DV, GLA_DK), F32)],
        compiler_params=_params("arbitrary", "arbitrary"),
        name="gla_mix",
    )(proj, proj, proj, proj, small, gla_wg, gla_bg.reshape(1, kw), gla_norm.reshape(1, GLA_DV), s0)


CONV_PAD = SUBLANES
INV_BASE_LOG2 = 3


def _inverse_masks():
    row = lax.broadcasted_iota(jnp.int32, (CHUNK, CHUNK), 0)
    col = lax.broadcasted_iota(jnp.int32, (CHUNK, CHUNK), 1)

    def same_block(log2):
        return jnp.right_shift(row, log2) == jnp.right_shift(col, log2)

    base = same_block(INV_BASE_LOG2)
    merges = []
    log2 = INV_BASE_LOG2
    while (1 << log2) < CHUNK:
        merges.append(same_block(log2 + 1) & jnp.logical_not(same_block(log2)))
        log2 += 1
    return base, merges


def _unit_lower_inverse_minus_eye(lower, base, merges):
    neg = jnp.where(base, -lower, 0.0)
    x = neg
    p = neg
    for _ in range(INV_BASE_LOG2 - 1):
        p = _dot_x3(p, p)
        x = x + p + _dot_x3(x, p)
    for m in merges:
        c = jnp.where(m, lower, 0.0)
        w = c + _dot_x3(x, c)
        x = x - (w + _dot_x3(w, x))
    return x


def _gdn_kernel(x_ref, g_ref, sm_ref, cw_ref, ga_ref, gn_ref, c0_ref, s0_ref,
                o_ref, s_ref, c_ref, xbuf):
    t = pl.program_id(1)
    lt = x_ref.shape[1]
    causal, strict = _tri_masks()
    tri = causal.astype(F32)
    inv_base, inv_merges = _inverse_masks()
    scale = GDN_DK ** -0.5

    @pl.when(t == 0)
    def _():
        xbuf[0:CONV_PAD, :] = jnp.zeros((CONV_PAD, GDN_CONV_DIM), F32)
        xbuf[CONV_PAD - (GDN_CONV - 1):CONV_PAD, :] = c0_ref[0]
        s_ref[...] = s0_ref[...]

    @pl.when(t > 0)
    def _():
        xbuf[0:CONV_PAD, :] = xbuf[lt:lt + CONV_PAD, :]

    xbuf[CONV_PAD:CONV_PAD + lt, :] = x_ref[0]
    c_ref[0] = xbuf[lt + CONV_PAD - (GDN_CONV - 1):lt + CONV_PAD, :]

    def conv_silu(r0, cols):
        acc = cw_ref[GDN_CONV - 1:GDN_CONV, cols] * xbuf[r0 + CONV_PAD:r0 + CONV_PAD + CHUNK, cols]
        for tap in range(GDN_CONV - 1):
            back = GDN_CONV - 1 - tap
            acc = acc + cw_ref[tap:tap + 1, cols] * xbuf[r0 + CONV_PAD - back:r0 + CONV_PAD - back + CHUNK, cols]
        return _silu(acc)

    def l2norm(x):
        return x * lax.rsqrt(jnp.sum(x * x, axis=-1, keepdims=True) + EPS)

    for c in range(lt // CHUNK):
        r0 = c * CHUNK
        rows = pl.ds(r0, CHUNK)
        sm = sm_ref[0, rows, :]
        gate = -jnp.exp(ga_ref[0:1, :]) * _softplus(sm + ga_ref[1:2, :])
        beta = jax.nn.sigmoid(sm)
        big_g = _dot_f32(tri, gate)
        big_gt = big_g.T
        for h in range(GDN_HEADS):
            hc = slice(h * GDN_DK, (h + 1) * GDN_DK)
            g_col = big_g[:, ALPHA_OFF + h:ALPHA_OFF + h + 1]
            g_row = big_gt[ALPHA_OFF + h:ALPHA_OFF + h + 1, :]
            b_col = beta[:, BETA_OFF + h:BETA_OFF + h + 1]
            g_last = g_col[CHUNK - 1:CHUNK, :]
            decay = jnp.where(causal, jnp.exp(g_col - g_row), 0.0)
            e_g = jnp.exp(g_col)
            q = l2norm(conv_silu(r0, slice(h * GDN_DK, (h + 1) * GDN_DK))) * scale
            k = l2norm(conv_silu(r0, slice(GDN_WIDTH + h * GDN_DK, GDN_WIDTH + (h + 1) * GDN_DK)))
            v = conv_silu(r0, slice(2 * GDN_WIDTH + h * GDN_DV, 2 * GDN_WIDTH + (h + 1) * GDN_DV))
            kb = k * b_col
            k16 = k.astype(BF16)
            lower = jnp.where(strict, _dot_nt(kb.astype(BF16), k16) * decay, 0.0)
            attn = jnp.where(causal, _dot_nt(q.astype(BF16), k16) * decay, 0.0)
            xinv = _unit_lower_inverse_minus_eye(lower, inv_base, inv_merges)
            rhs = jnp.concatenate([v * b_col, kb * e_g], axis=-1)
            sol = rhs + _dot_x3(xinv, rhs)
            s = s_ref[0, h]
            s16 = s.astype(BF16)
            u = sol[:, :GDN_DV] - _dot(sol[:, GDN_DV:].astype(BF16), s16)
            u16 = u.astype(BF16)
            o = _dot((q * e_g).astype(BF16), s16) + _dot(attn.astype(BF16), u16)
            k_dec = (k * jnp.exp(g_last - g_col)).astype(BF16)
            s_ref[0, h] = jnp.exp(g_last) * s + _dot_tn(k_dec, u16)
            o = o * lax.rsqrt(jnp.mean(o * o, axis=-1, keepdims=True) + EPS) * gn_ref[...]
            o = o * _silu(g_ref[0, rows, hc])
            o_ref[0, rows, hc] = o.astype(o_ref.dtype)


def _gdn_mix(proj, small, c0, s0, conv_w, gate_vec, gdn_norm, lt):
    b, l, _ = proj.shape
    return pl.pallas_call(
        _gdn_kernel,
        grid=(b, l // lt),
        in_specs=[
            pl.BlockSpec((1, lt, GDN_CONV_DIM), lambda i, t: (i, t, 1)),
            pl.BlockSpec((1, lt, GDN_WIDTH), lambda i, t: (i, t, 6)),
            pl.BlockSpec((1, lt, SMALL_COLS), lambda i, t: (i, t, 0)),
            pl.BlockSpec((GDN_CONV, GDN_CONV_DIM), lambda i, t: (0, 0)),
            pl.BlockSpec((2, SMALL_COLS), lambda i, t: (0, 0)),
            pl.BlockSpec((1, GDN_DV), lambda i, t: (0, 0)),
            pl.BlockSpec((1, GDN_CONV - 1, GDN_CONV_DIM), lambda i, t: (i, 0, 0)),
            pl.BlockSpec((1, GDN_HEADS, GDN_DK, GDN_DV), lambda i, t: (i, 0, 0, 0)),
        ],
        out_specs=[
            pl.BlockSpec((1, lt, GDN_WIDTH), lambda i, t: (i, t, 0)),
            pl.BlockSpec((1, GDN_HEADS, GDN_DK, GDN_DV), lambda i, t: (i, 0, 0, 0)),
            pl.BlockSpec((1, GDN_CONV - 1, GDN_CONV_DIM), lambda i, t: (i, 0, 0)),
        ],
        out_shape=[
            jax.ShapeDtypeStruct((b, l, GDN_WIDTH), BF16),
            jax.ShapeDtypeStruct((b, GDN_HEADS, GDN_DK, GDN_DV), F32),
            jax.ShapeDtypeStruct((b, GDN_CONV - 1, GDN_CONV_DIM), F32),
        ],
        scratch_shapes=[pltpu.VMEM((lt + CONV_PAD, GDN_CONV_DIM), F32)],
        compiler_params=_params("arbitrary", "arbitrary"),
        name="gdn_mix",
    )(proj, proj, small, conv_w, gate_vec, gdn_norm.reshape(1, GDN_DV), c0, s0)


def _out_proj_kernel(x_ref, oa_ref, ob_ref, w_ref, g1_ref, sc_ref, sh_ref, n2_ref, x1_ref, h2_ref):
    bt, lt, d = x_ref.shape
    oa = oa_ref[...].reshape(bt * lt, GLA_WIDTH)
    ob = ob_ref[...].reshape(bt * lt, GDN_WIDTH)
    mix = _dot(oa, w_ref[0:GLA_WIDTH, :]) + _dot(ob, w_ref[GLA_WIDTH:GLA_WIDTH + GDN_WIDTH, :])
    x1 = x_ref[...] + g1_ref[...] * mix.reshape(bt, lt, d)
    x1_ref[...] = x1
    y = x1 * lax.rsqrt(jnp.mean(x1 * x1, axis=-1, keepdims=True) + EPS) * n2_ref[...]
    h2_ref[...] = (y * (1.0 + sc_ref[...]) + sh_ref[...]).astype(h2_ref.dtype)


def _out_proj(x, o_a, o_b, mod3, mod_row0, w_out16, norm2, bt, lt):
    b, l, d = x.shape
    mrow = mod_row0 // bt
    return pl.pallas_call(
        _out_proj_kernel,
        grid=(b // bt, l // lt),
        in_specs=[
            pl.BlockSpec((bt, lt, d), lambda i, t: (i, t, 0)),
            pl.BlockSpec((bt, lt, GLA_WIDTH), lambda i, t: (i, t, 0)),
            pl.BlockSpec((bt, lt, GDN_WIDTH), lambda i, t: (i, t, 0)),
            pl.BlockSpec((GLA_WIDTH + GDN_WIDTH, d), lambda i, t: (0, 0)),
            pl.BlockSpec((bt, 1, d), lambda i, t: (mrow + i, 0, 2)),
            pl.BlockSpec((bt, 1, d), lambda i, t: (mrow + i, 0, 4)),
            pl.BlockSpec((bt, 1, d), lambda i, t: (mrow + i, 0, 3)),
            pl.BlockSpec((1, 1, d), lambda i, t: (0, 0, 0)),
        ],
        out_specs=[
            pl.BlockSpec((bt, lt, d), lambda i, t: (i, t, 0)),
            pl.BlockSpec((bt, lt, d), lambda i, t: (i, t, 0)),
        ],
        out_shape=[
            jax.ShapeDtypeStruct((b, l, d), F32),
            jax.ShapeDtypeStruct((b, l, d), BF16),
        ],
        compiler_params=_params("arbitrary", "arbitrary"),
        name="out_proj",
    )(x, o_a, o_b, w_out16, mod3, mod3, mod3, norm2.reshape(1, 1, d))


FFN_PAD = SUBLANES


def _ffn_kernel(h_ref, halo_ref, wg_ref, wv_ref, wd_ref, cw_ref, cb_ref, st_ref, x1_ref, g2_ref,
                fn_ref, y_ref, so_ref, acc, gbuf, *, has_halo):
    t = pl.program_id(1)
    f = pl.program_id(2)
    bt, lt, d = h_ref.shape
    tf = wg_ref.shape[1]
    hb = h_ref[...].reshape(bt * lt, d)
    gate = _dot(hb, wg_ref[...]).reshape(bt, lt, tf)
    val = _dot(hb, wv_ref[...])
    prev = st_ref[...]
    if has_halo:
        halo_gate = _dot(halo_ref[0], wg_ref[...])
        prev = jnp.where(t == 0, prev, halo_gate[SUBLANES - (FFN_CONV - 1):SUBLANES, :][None])
    gbuf[:, FFN_PAD - (FFN_CONV - 1):FFN_PAD, :] = prev
    gbuf[:, FFN_PAD:FFN_PAD + lt, :] = gate
    so_ref[...] = gate[:, lt - (FFN_CONV - 1):lt, :].reshape(so_ref.shape)
    conv = cw_ref[FFN_CONV - 1:FFN_CONV, :] * gate
    for tap in range(FFN_CONV - 1):
        back = FFN_CONV - 1 - tap
        conv = conv + cw_ref[tap:tap + 1, :] * gbuf[:, FFN_PAD - back:FFN_PAD - back + lt, :]
    act = _silu(conv + cb_ref[...]).reshape(bt * lt, tf) * val
    part = _dot(act.astype(BF16), wd_ref[...])

    @pl.when(f == 0)
    def _():
        acc[...] = part

    @pl.when(f > 0)
    def _():
        acc[...] += part

    @pl.when(f == pl.num_programs(2) - 1)
    def _():
        x2 = x1_ref[...] + g2_ref[...] * acc[...].reshape(bt, lt, d)
        y = x2 * lax.rsqrt(jnp.mean(x2 * x2, axis=-1, keepdims=True) + EPS) * fn_ref[...]
        y_ref[...] = y


def _ffn(h2, x1, st0, mod3, mod_row0, w_up16, w_down16, conv_w, conv_b, final_norm, bt, lt):
    b, l, d = h2.shape
    tf = 512
    nf = D_FF // tf
    mrow = mod_row0 // bt
    has_halo = lt < l
    halo_blocks = lt // SUBLANES
    kern = functools.partial(_ffn_kernel, has_halo=has_halo)
    return pl.pallas_call(
        kern,
        grid=(b // bt, l // lt, nf),
        in_specs=[
            pl.BlockSpec((bt, lt, d), lambda i, t, f: (i, t, 0)),
            pl.BlockSpec((1, SUBLANES, d), lambda i, t, f: (i, jnp.maximum(t * halo_blocks - 1, 0), 0)),
            pl.BlockSpec((d, tf), lambda i, t, f: (0, f)),
            pl.BlockSpec((d, tf), lambda i, t, f: (0, nf + f)),
            pl.BlockSpec((tf, d), lambda i, t, f: (f, 0)),
            pl.BlockSpec((FFN_CONV, tf), lambda i, t, f: (0, f)),
            pl.BlockSpec((1, tf), lambda i, t, f: (0, f)),
            pl.BlockSpec((bt, FFN_CONV - 1, tf), lambda i, t, f: (i, 0, f)),
            pl.BlockSpec((bt, lt, d), lambda i, t, f: (i, t, 0)),
            pl.BlockSpec((bt, 1, d), lambda i, t, f: (mrow + i, 0, 5)),
            pl.BlockSpec((1, 1, d), lambda i, t, f: (0, 0, 0)),
        ],
        out_specs=[
            pl.BlockSpec((bt, lt, d), lambda i, t, f: (i, t, 0)),
            pl.BlockSpec((bt, 1, FFN_CONV - 1, tf), lambda i, t, f: (i, t, 0, f)),
        ],
        out_shape=[
            jax.ShapeDtypeStruct((b, l, d), F32),
            jax.ShapeDtypeStruct((b, l // lt, FFN_CONV - 1, D_FF), F32),
        ],
        scratch_shapes=[
            pltpu.VMEM((bt * lt, d), F32),
            pltpu.VMEM((bt, lt + FFN_PAD, tf), F32),
        ],
        compiler_params=_params("arbitrary", "arbitrary", "arbitrary"),
        name="ffn",
    )(h2, h2, w_up16, w_up16, w_down16, conv_w, conv_b.reshape(1, D_FF), st0, x1, mod3,
      final_norm.reshape(1, 1, d))


def _trunk(x, mod3, mod_row0, states, weights, cfg):
    (w_main, w_small, norm1, gla_wg, gla_bg, gla_norm, gdn_conv_w, gate_vec, gdn_norm, w_out16,
     norm2, w_up16, w_down16, ffn_conv_w, ffn_conv_b, final_norm) = weights
    s_gla, s_gdn, s_conv, s_ffn = states
    proj, small = _in_proj(x, mod3, mod_row0, norm1, w_main, w_small, cfg["in_bt"], cfg["in_lt"])
    o_a, n_gla = _gla_mix(proj, small, s_gla, gla_wg, gla_bg, gla_norm, cfg["mix_lt"])
    o_b, n_gdn, n_conv = _gdn_mix(proj, small, s_conv, s_gdn, gdn_conv_w, gate_vec, gdn_norm,
                                  cfg["mix_lt"])
    x1, h2 = _out_proj(x, o_a, o_b, mod3, mod_row0, w_out16, norm2, cfg["ffn_bt"], cfg["ffn_lt"])
    y, n_ffn = _ffn(h2, x1, s_ffn, mod3, mod_row0, w_up16, w_down16, ffn_conv_w, ffn_conv_b,
                    final_norm, cfg["ffn_bt"], cfg["ffn_lt"])
    return y, n_gla[None], n_gdn[None], n_conv[None], n_ffn[:, -1][None]


def kernel(x_prompt, x_sample, c_prompt, c_sample, state_gla, state_gdn, state_gdn_conv, state_ffn_conv, w_ada, b_ada, norm1, w_in, gla_wg, gla_bg, gla_norm, gdn_conv_w, gdn_a_log, gdn_dt_bias, gdn_norm, w_out, norm2, w_up, ffn_conv_w, ffn_conv_b, w_down, final_norm):
    bp = x_prompt.shape[0]
    bs = x_sample.shape[0]

    w_in0 = w_in[0]
    w_main = jnp.concatenate(
        [w_in0[:, :GLA_COLS], w_in0[:, IN_OFF_GDN:IN_OFF_BETA]], axis=1).astype(BF16)
    w_small = jnp.concatenate(
        [w_in0[:, IN_OFF_LR:IN_OFF_GDN], w_in0[:, IN_OFF_BETA:],
         jnp.zeros((D_MODEL, SMALL_COLS - GLA_GATE_RANK - 2 * GDN_HEADS), F32)], axis=1).astype(BF16)
    w_out16 = w_out[0].astype(BF16)
    w_up16 = w_up[0].astype(BF16)
    w_down16 = w_down[0].astype(BF16)
    gate_vec = jnp.zeros((2, SMALL_COLS), F32)
    gate_vec = gate_vec.at[0, ALPHA_OFF:ALPHA_OFF + GDN_HEADS].set(gdn_a_log[0])
    gate_vec = gate_vec.at[1, ALPHA_OFF:ALPHA_OFF + GDN_HEADS].set(gdn_dt_bias[0])

    c_all = jnp.concatenate([c_sample, c_prompt], axis=0)
    mod = _ada_mod(c_all, w_ada[0], b_ada[0])
    mod3 = mod.reshape(bs + bp, 1, N_MOD * D_MODEL)

    weights = (w_main, w_small, norm1[0], gla_wg[0], gla_bg[0], gla_norm[0], gdn_conv_w[0], gate_vec,
               gdn_norm[0], w_out16, norm2[0], w_up16, w_down16, ffn_conv_w[0], ffn_conv_b[0],
               final_norm)

    fresh = (jnp.zeros((bp, GLA_HEADS, GLA_DK, GLA_DV), F32),
             jnp.zeros((bp, GDN_HEADS, GDN_DK, GDN_DV), F32),
             jnp.zeros((bp, GDN_CONV - 1, GDN_CONV_DIM), F32),
             jnp.zeros((bp, FFN_CONV - 1, D_FF), F32))
    cfg_p = dict(in_bt=1, in_lt=1024, mix_lt=256, ffn_bt=1, ffn_lt=512)
    y_p, p_gla, p_gdn, p_conv, p_ffn = _trunk(x_prompt, mod3, bs, fresh, weights, cfg_p)

    carried = (state_gla[0], state_gdn[0], state_gdn_conv[0], state_ffn_conv[0])
    cfg_s = dict(in_bt=bs, in_lt=x_sample.shape[1], mix_lt=x_sample.shape[1], ffn_bt=8,
                 ffn_lt=x_sample.shape[1])
    y_s, s_gla, s_gdn, s_conv, s_ffn = _trunk(x_sample, mod3, 0, carried, weights, cfg_s)
    return (y_p, y_s, p_gla, p_gdn, p_conv, p_ffn, s_gla, s_gdn, s_conv, s_ffn)
```

```python
import functools

import jax
import jax.numpy as jnp
from jax import lax
from jax.experimental import pallas as pl
from jax.experimental.pallas import tpu as pltpu

F32 = jnp.float32
BF16 = jnp.bfloat16

D_MODEL = 2048
CHUNK = 64
GLA_HEADS = 4
GLA_DK = 128
GLA_DV = 256
GLA_WIDTH = GLA_HEADS * GLA_DV
GLA_GATE_RANK = 16
GLA_GATE_NORM = 16.0
GDN_HEADS = 8
GDN_DK = 128
GDN_DV = 128
GDN_WIDTH = GDN_HEADS * GDN_DV
GDN_CONV = 4
GDN_CONV_DIM = 3 * GDN_WIDTH
D_FF = 5632
FFN_CONV = 3
N_MOD = 6
EPS = 1e-6

GLA_COLS = 2 * GLA_HEADS * GLA_DK + 2 * GLA_WIDTH
GDN_COLS = 4 * GDN_WIDTH
MAIN_COLS = GLA_COLS + GDN_COLS
SMALL_COLS = 128
LR_OFF, BETA_OFF, ALPHA_OFF = 0, GLA_GATE_RANK, GLA_GATE_RANK + GDN_HEADS
IN_OFF_LR = GLA_COLS
IN_OFF_GDN = IN_OFF_LR + GLA_GATE_RANK
IN_OFF_BETA = IN_OFF_GDN + GDN_COLS

SUBLANES = 8
VMEM_LIMIT_BYTES = 56 * 1024 * 1024

NT_DIMS = (((1,), (1,)), ((), ()))
TN_DIMS = (((0,), (0,)), ((), ()))


def _dot(a, b):
    return jnp.dot(a, b, preferred_element_type=F32)


def _dot_nt(a, b):
    return lax.dot_general(a, b, NT_DIMS, preferred_element_type=F32)


def _dot_tn(a, b):
    return lax.dot_general(a, b, TN_DIMS, preferred_element_type=F32)


def _dot_f32(a, b):
    return jnp.dot(a, b, preferred_element_type=F32, precision=lax.Precision.HIGHEST)


def _split_bf16(a):
    hi = a.astype(BF16)
    lo = (a - hi.astype(F32)).astype(BF16)
    return hi, lo


def _dot_x3_many(a_list, b_list):
    sa = [_split_bf16(a) for a in a_list]
    sb = [_split_bf16(b) for b in b_list]
    hh = [_dot(a[0], b[0]) for a, b in zip(sa, sb)]
    hl = [_dot(a[0], b[1]) for a, b in zip(sa, sb)]
    lh = [_dot(a[1], b[0]) for a, b in zip(sa, sb)]
    return [x + (y + z) for x, y, z in zip(hh, hl, lh)]


def _silu(x):
    return x * jax.nn.sigmoid(x)


def _softplus(x):
    return jnp.maximum(x, 0.0) + jnp.log1p(jnp.exp(-jnp.abs(x)))


def _log_sigmoid(x):
    return jnp.minimum(x, 0.0) - jnp.log1p(jnp.exp(-jnp.abs(x)))


def _params(*semantics):
    return pltpu.CompilerParams(dimension_semantics=semantics, vmem_limit_bytes=VMEM_LIMIT_BYTES)


def _tri_masks():
    row = lax.broadcasted_iota(jnp.int32, (CHUNK, CHUNK), 0)
    col = lax.broadcasted_iota(jnp.int32, (CHUNK, CHUNK), 1)
    return row >= col, row > col


def _ada_kernel(c_ref, w_ref, b_ref, o_ref):
    s = _silu(c_ref[...]).astype(BF16)
    o_ref[...] = _dot(s, w_ref[...].astype(BF16)) + b_ref[...]


def _ada_mod(c_all, w_ada, b_ada):
    rows = c_all.shape[0]
    n = w_ada.shape[1]
    tn = 1024
    return pl.pallas_call(
        _ada_kernel,
        grid=(n // tn,),
        in_specs=[
            pl.BlockSpec((rows, D_MODEL), lambda j: (0, 0)),
            pl.BlockSpec((D_MODEL, tn), lambda j: (0, j)),
            pl.BlockSpec((1, tn), lambda j: (0, j)),
        ],
        out_specs=pl.BlockSpec((rows, tn), lambda j: (0, j)),
        out_shape=jax.ShapeDtypeStruct((rows, n), F32),
        compiler_params=_params("arbitrary"),
        name="ada_mod",
    )(c_all, w_ada, b_ada.reshape(1, n))


def _in_proj_kernel(x_ref, sc_ref, sh_ref, n1_ref, w_ref, ws_ref, o_ref, os_ref, h_scr):
    bt, lt, d = x_ref.shape

    @pl.when(pl.program_id(2) == 0)
    def _():
        x = x_ref[...]
        y = x * lax.rsqrt(jnp.mean(x * x, axis=-1, keepdims=True) + EPS) * n1_ref[...]
        h = y * (1.0 + sc_ref[...]) + sh_ref[...]
        hb = h.reshape(bt * lt, d).astype(BF16)
        h_scr[...] = hb
        os_ref[...] = _dot(hb, ws_ref[...]).reshape(os_ref.shape)

    o_ref[...] = _dot(h_scr[...], w_ref[...]).reshape(o_ref.shape)


def _in_proj(x, mod3, mod_row0, norm1, w_main, w_small, bt, lt):
    b, l, d = x.shape
    tn = 1024
    mrow = mod_row0 // bt
    grid = (b // bt, l // lt, MAIN_COLS // tn)
    return pl.pallas_call(
        _in_proj_kernel,
        grid=grid,
        in_specs=[
            pl.BlockSpec((bt, lt, d), lambda i, t, j: (i, t, 0)),
            pl.BlockSpec((bt, 1, d), lambda i, t, j: (mrow + i, 0, 1)),
            pl.BlockSpec((bt, 1, d), lambda i, t, j: (mrow + i, 0, 0)),
            pl.BlockSpec((1, 1, d), lambda i, t, j: (0, 0, 0)),
            pl.BlockSpec((d, tn), lambda i, t, j: (0, j)),
            pl.BlockSpec((d, SMALL_COLS), lambda i, t, j: (0, 0)),
        ],
        out_specs=[
            pl.BlockSpec((bt, lt, tn), lambda i, t, j: (i, t, j)),
            pl.BlockSpec((bt, lt, SMALL_COLS), lambda i, t, j: (i, t, 0)),
        ],
        out_shape=[
            jax.ShapeDtypeStruct((b, l, MAIN_COLS), F32),
            jax.ShapeDtypeStruct((b, l, SMALL_COLS), F32),
        ],
        scratch_shapes=[pltpu.VMEM((bt * lt, d), BF16)],
        compiler_params=_params("arbitrary", "arbitrary", "arbitrary"),
        name="in_proj",
    )(x, mod3, mod3, norm1.reshape(1, 1, d), w_main, w_small)


def _gla_kernel(q_ref, k_ref, v_ref, r_ref, sm_ref, wg_ref, bg_ref, gn_ref, s0_ref,
                o_ref, s_ref, st_scr):
    t = pl.program_id(1)
    lt = q_ref.shape[1]
    causal, _ = _tri_masks()
    tri = causal.astype(F32)
    scale = GLA_DK ** -0.5

    @pl.when(t == 0)
    def _():
        for h in range(GLA_HEADS):
            st_scr[h] = s0_ref[0, h].T

    wg = wg_ref[...].astype(BF16)
    heads = range(GLA_HEADS)
    for c in range(lt // CHUNK):
        rows = pl.ds(c * CHUNK, CHUNK)
        a_lr = sm_ref[0, rows, LR_OFF:LR_OFF + GLA_GATE_RANK].astype(BF16)
        log_a = _log_sigmoid(_dot(a_lr, wg) + bg_ref[...]) / GLA_GATE_NORM
        big_g = _dot_f32(tri, log_a)
        g_mid = big_g[CHUNK // 2 - 1:CHUNK // 2, :]
        g_last = big_g[CHUNK - 1:CHUNK, :]
        e_q = jnp.exp(big_g - g_mid)
        e_k = jnp.exp(g_mid - big_g)
        e_g = jnp.exp(big_g)
        e_kl = jnp.exp(g_last - big_g)
        e_l = jnp.exp(g_last)
        kcs = [slice(h * GLA_DK, (h + 1) * GLA_DK) for h in heads]
        vcs = [slice(h * GLA_DV, (h + 1) * GLA_DV) for h in heads]
        q = [q_ref[0, rows, kc] * scale for kc in kcs]
        k = [k_ref[0, rows, kc] for kc in kcs]
        v = [v_ref[0, rows, vc].astype(BF16) for vc in vcs]
        st = [st_scr[h] for h in heads]
        attn = [_dot_nt((qq * e_q[:, kc]).astype(BF16), (kk * e_k[:, kc]).astype(BF16))
                for qq, kk, kc in zip(q, k, kcs)]
        q_s = [_dot_nt((qq * e_g[:, kc]).astype(BF16), ss.astype(BF16))
               for qq, ss, kc in zip(q, st, kcs)]
        v_k = [_dot_tn(vv, (kk * e_kl[:, kc]).astype(BF16)) for vv, kk, kc in zip(v, k, kcs)]
        a_v = [_dot(jnp.where(causal, a, 0.0).astype(BF16), vv) for a, vv in zip(attn, v)]
        for h in heads:
            st_scr[h] = st[h] * e_l[:, kcs[h]] + v_k[h]
            o = a_v[h] + q_s[h]
            o = o * lax.rsqrt(jnp.mean(o * o, axis=-1, keepdims=True) + EPS) * gn_ref[...]
            o = o * _silu(r_ref[0, rows, vcs[h]])
            o_ref[0, rows, vcs[h]] = o.astype(o_ref.dtype)

    @pl.when(t == pl.num_programs(1) - 1)
    def _():
        for h in range(GLA_HEADS):
            s_ref[0, h] = st_scr[h].T


def _gla_mix(proj, small, s0, gla_wg, gla_bg, gla_norm, lt):
    b, l, _ = proj.shape
    kw = GLA_HEADS * GLA_DK
    return pl.pallas_call(
        _gla_kernel,
        grid=(b, l // lt),
        in_specs=[
            pl.BlockSpec((1, lt, kw), lambda i, t: (i, t, 0)),
            pl.BlockSpec((1, lt, kw), lambda i, t: (i, t, 1)),
            pl.BlockSpec((1, lt, GLA_WIDTH), lambda i, t: (i, t, 1)),
            pl.BlockSpec((1, lt, GLA_WIDTH), lambda i, t: (i, t, 2)),
            pl.BlockSpec((1, lt, SMALL_COLS), lambda i, t: (i, t, 0)),
            pl.BlockSpec((GLA_GATE_RANK, kw), lambda i, t: (0, 0)),
            pl.BlockSpec((1, kw), lambda i, t: (0, 0)),
            pl.BlockSpec((1, GLA_DV), lambda i, t: (0, 0)),
            pl.BlockSpec((1, GLA_HEADS, GLA_DK, GLA_DV), lambda i, t: (i, 0, 0, 0)),
        ],
        out_specs=[
            pl.BlockSpec((1, lt, GLA_WIDTH), lambda i, t: (i, t, 0)),
            pl.BlockSpec((1, GLA_HEADS, GLA_DK, GLA_DV), lambda i, t: (i, 0, 0, 0)),
        ],
        out_shape=[
            jax.ShapeDtypeStruct((b, l, GLA_WIDTH), BF16),
            jax.ShapeDtypeStruct((b, GLA_HEADS, GLA_DK, GLA_DV), F32),
        ],
        scratch_shapes=[pltpu.VMEM((GLA_HEADS, GLA_DV, GLA_DK), F32)],
        compiler_params=_params("arbitrary", "arbitrary"),
        name="gla_mix",
    )(proj, proj, proj, proj, small, gla_wg, gla_bg.reshape(1, kw), gla_norm.reshape(1, GLA_DV), s0)


CONV_PAD = SUBLANES
INV_BASE_LOG2 = 3


def _inverse_masks():
    row = lax.broadcasted_iota(jnp.int32, (CHUNK, CHUNK), 0)
    col = lax.broadcasted_iota(jnp.int32, (CHUNK, CHUNK), 1)

    def same_block(log2):
        return jnp.right_shift(row, log2) == jnp.right_shift(col, log2)

    base = same_block(INV_BASE_LOG2)
    merges = []
    log2 = INV_BASE_LOG2
    while (1 << log2) < CHUNK:
        merges.append(same_block(log2 + 1) & jnp.logical_not(same_block(log2)))
        log2 += 1
    return base, merges


def _unit_lower_inverse_minus_eye(lower, base, merges):
    neg = [jnp.where(base, -l, 0.0) for l in lower]
    x = neg
    p = neg
    for _ in range(INV_BASE_LOG2 - 1):
        p = _dot_x3_many(p, p)
        xp = _dot_x3_many(x, p)
        x = [a + b + c for a, b, c in zip(x, p, xp)]
    for m in merges:
        c = [jnp.where(m, l, 0.0) for l in lower]
        w = [a + b for a, b in zip(c, _dot_x3_many(x, c))]
        wx = _dot_x3_many(w, x)
        x = [a - (b + d) for a, b, d in zip(x, w, wx)]
    return x


def _gdn_kernel(x_ref, g_ref, sm_ref, cw_ref, ga_ref, gn_ref, c0_ref, s0_ref,
                o_ref, s_ref, c_ref, xbuf):
    t = pl.program_id(1)
    lt = x_ref.shape[1]
    causal, strict = _tri_masks()
    tri = causal.astype(F32)
    inv_base, inv_merges = _inverse_masks()
    scale = GDN_DK ** -0.5

    @pl.when(t == 0)
    def _():
        xbuf[0:CONV_PAD, :] = jnp.zeros((CONV_PAD, GDN_CONV_DIM), F32)
        xbuf[CONV_PAD - (GDN_CONV - 1):CONV_PAD, :] = c0_ref[0]
        s_ref[...] = s0_ref[...]

    @pl.when(t > 0)
    def _():
        xbuf[0:CONV_PAD, :] = xbuf[lt:lt + CONV_PAD, :]

    xbuf[CONV_PAD:CONV_PAD + lt, :] = x_ref[0]
    c_ref[0] = xbuf[lt + CONV_PAD - (GDN_CONV - 1):lt + CONV_PAD, :]

    def conv_silu(r0, cols):
        acc = cw_ref[GDN_CONV - 1:GDN_CONV, cols] * xbuf[r0 + CONV_PAD:r0 + CONV_PAD + CHUNK, cols]
        for tap in range(GDN_CONV - 1):
            back = GDN_CONV - 1 - tap
            acc = acc + cw_ref[tap:tap + 1, cols] * xbuf[r0 + CONV_PAD - back:r0 + CONV_PAD - back + CHUNK, cols]
        return _silu(acc)

    def l2norm(x):
        return x * lax.rsqrt(jnp.sum(x * x, axis=-1, keepdims=True) + EPS)

    heads = range(GDN_HEADS)
    for c in range(lt // CHUNK):
        r0 = c * CHUNK
        rows = pl.ds(r0, CHUNK)
        sm = sm_ref[0, rows, :]
        gate = -jnp.exp(ga_ref[0:1, :]) * _softplus(sm + ga_ref[1:2, :])
        beta = jax.nn.sigmoid(sm)
        big_g = _dot_f32(tri, gate)
        big_gt = big_g.T
        g_col = [big_g[:, ALPHA_OFF + h:ALPHA_OFF + h + 1] for h in heads]
        g_row = [big_gt[ALPHA_OFF + h:ALPHA_OFF + h + 1, :] for h in heads]
        b_col = [beta[:, BETA_OFF + h:BETA_OFF + h + 1] for h in heads]
        g_last = [g[CHUNK - 1:CHUNK, :] for g in g_col]
        decay = [jnp.where(causal, jnp.exp(gc - gr), 0.0) for gc, gr in zip(g_col, g_row)]
        e_g = [jnp.exp(g) for g in g_col]
        q = [l2norm(conv_silu(r0, slice(h * GDN_DK, (h + 1) * GDN_DK))) * scale for h in heads]
        k = [l2norm(conv_silu(r0, slice(GDN_WIDTH + h * GDN_DK, GDN_WIDTH + (h + 1) * GDN_DK)))
             for h in heads]
        v = [conv_silu(r0, slice(2 * GDN_WIDTH + h * GDN_DV, 2 * GDN_WIDTH + (h + 1) * GDN_DV))
             for h in heads]
        kb = [kk * b for kk, b in zip(k, b_col)]
        k16 = [kk.astype(BF16) for kk in k]
        kk_t = [_dot_nt(a.astype(BF16), b) for a, b in zip(kb, k16)]
        qk_t = [_dot_nt(a.astype(BF16), b) for a, b in zip(q, k16)]
        lower = [jnp.where(strict, m * d, 0.0) for m, d in zip(kk_t, decay)]
        attn = [jnp.where(causal, m * d, 0.0).astype(BF16) for m, d in zip(qk_t, decay)]
        xinv = _unit_lower_inverse_minus_eye(lower, inv_base, inv_merges)
        rhs = [jnp.concatenate([vv * b, kbb * e], axis=-1)
               for vv, b, kbb, e in zip(v, b_col, kb, e_g)]
        sol = [r + xr for r, xr in zip(rhs, _dot_x3_many(xinv, rhs))]
        s = [s_ref[0, h] for h in heads]
        s16 = [ss.astype(BF16) for ss in s]
        k_s = [_dot(so[:, GDN_DV:].astype(BF16), ss) for so, ss in zip(sol, s16)]
        q_s = [_dot((qq * e).astype(BF16), ss) for qq, e, ss in zip(q, e_g, s16)]
        u16 = [(so[:, :GDN_DV] - ks).astype(BF16) for so, ks in zip(sol, k_s)]
        a_u = [_dot(a, uu) for a, uu in zip(attn, u16)]
        k_u = [_dot_tn((kk * jnp.exp(gl - gc)).astype(BF16), uu)
               for kk, gl, gc, uu in zip(k, g_last, g_col, u16)]
        for h in heads:
            hc = slice(h * GDN_DV, (h + 1) * GDN_DV)
            s_ref[0, h] = jnp.exp(g_last[h]) * s[h] + k_u[h]
            o = q_s[h] + a_u[h]
            o = o * lax.rsqrt(jnp.mean(o * o, axis=-1, keepdims=True) + EPS) * gn_ref[...]
            o = o * _silu(g_ref[0, rows, hc])
            o_ref[0, rows, hc] = o.astype(o_ref.dtype)


def _gdn_mix(proj, small, c0, s0, conv_w, gate_vec, gdn_norm, lt):
    b, l, _ = proj.shape
    return pl.pallas_call(
        _gdn_kernel,
        grid=(b, l // lt),
        in_specs=[
            pl.BlockSpec((1, lt, GDN_CONV_DIM), lambda i, t: (i, t, 1)),
            pl.BlockSpec((1, lt, GDN_WIDTH), lambda i, t: (i, t, 6)),
            pl.BlockSpec((1, lt, SMALL_COLS), lambda i, t: (i, t, 0)),
            pl.BlockSpec((GDN_CONV, GDN_CONV_DIM), lambda i, t: (0, 0)),
            pl.BlockSpec((2, SMALL_COLS), lambda i, t: (0, 0)),
            pl.BlockSpec((1, GDN_DV), lambda i, t: (0, 0)),
            pl.BlockSpec((1, GDN_CONV - 1, GDN_CONV_DIM), lambda i, t: (i, 0, 0)),
            pl.BlockSpec((1, GDN_HEADS, GDN_DK, GDN_DV), lambda i, t: (i, 0, 0, 0)),
        ],
        out_specs=[
            pl.BlockSpec((1, lt, GDN_WIDTH), lambda i, t: (i, t, 0)),
            pl.BlockSpec((1, GDN_HEADS, GDN_DK, GDN_DV), lambda i, t: (i, 0, 0, 0)),
            pl.BlockSpec((1, GDN_CONV - 1, GDN_CONV_DIM), lambda i, t: (i, 0, 0)),
        ],
        out_shape=[
            jax.ShapeDtypeStruct((b, l, GDN_WIDTH), BF16),
            jax.ShapeDtypeStruct((b, GDN_HEADS, GDN_DK, GDN_DV), F32),
            jax.ShapeDtypeStruct((b, GDN_CONV - 1, GDN_CONV_DIM), F32),
        ],
        scratch_shapes=[pltpu.VMEM((lt + CONV_PAD, GDN_CONV_DIM), F32)],
        compiler_params=_params("arbitrary", "arbitrary"),
        name="gdn_mix",
    )(proj, proj, small, conv_w, gate_vec, gdn_norm.reshape(1, GDN_DV), c0, s0)


def _out_proj_kernel(x_ref, oa_ref, ob_ref, w_ref, g1_ref, sc_ref, sh_ref, n2_ref, x1_ref, h2_ref):
    bt, lt, d = x_ref.shape
    oa = oa_ref[...].reshape(bt * lt, GLA_WIDTH)
    ob = ob_ref[...].reshape(bt * lt, GDN_WIDTH)
    mix = _dot(oa, w_ref[0:GLA_WIDTH, :]) + _dot(ob, w_ref[GLA_WIDTH:GLA_WIDTH + GDN_WIDTH, :])
    x1 = x_ref[...] + g1_ref[...] * mix.reshape(bt, lt, d)
    x1_ref[...] = x1
    y = x1 * lax.rsqrt(jnp.mean(x1 * x1, axis=-1, keepdims=True) + EPS) * n2_ref[...]
    h2_ref[...] = (y * (1.0 + sc_ref[...]) + sh_ref[...]).astype(h2_ref.dtype)


def _out_proj(x, o_a, o_b, mod3, mod_row0, w_out16, norm2, bt, lt):
    b, l, d = x.shape
    mrow = mod_row0 // bt
    return pl.pallas_call(
        _out_proj_kernel,
        grid=(b // bt, l // lt),
        in_specs=[
            pl.BlockSpec((bt, lt, d), lambda i, t: (i, t, 0)),
            pl.BlockSpec((bt, lt, GLA_WIDTH), lambda i, t: (i, t, 0)),
            pl.BlockSpec((bt, lt, GDN_WIDTH), lambda i, t: (i, t, 0)),
            pl.BlockSpec((GLA_WIDTH + GDN_WIDTH, d), lambda i, t: (0, 0)),
            pl.BlockSpec((bt, 1, d), lambda i, t: (mrow + i, 0, 2)),
            pl.BlockSpec((bt, 1, d), lambda i, t: (mrow + i, 0, 4)),
            pl.BlockSpec((bt, 1, d), lambda i, t: (mrow + i, 0, 3)),
            pl.BlockSpec((1, 1, d), lambda i, t: (0, 0, 0)),
        ],
        out_specs=[
            pl.BlockSpec((bt, lt, d), lambda i, t: (i, t, 0)),
            pl.BlockSpec((bt, lt, d), lambda i, t: (i, t, 0)),
        ],
        out_shape=[
            jax.ShapeDtypeStruct((b, l, d), F32),
            jax.ShapeDtypeStruct((b, l, d), BF16),
        ],
        compiler_params=_params("arbitrary", "arbitrary"),
        name="out_proj",
    )(x, o_a, o_b, w_out16, mod3, mod3, mod3, norm2.reshape(1, 1, d))


FFN_PAD = SUBLANES


def _ffn_kernel(h_ref, halo_ref, wg_ref, wv_ref, wd_ref, cw_ref, cb_ref, st_ref, x1_ref, g2_ref,
                fn_ref, y_ref, so_ref, acc, gbuf, *, has_halo):
    t = pl.program_id(1)
    f = pl.program_id(2)
    bt, lt, d = h_ref.shape
    tf = wg_ref.shape[1]
    hb = h_ref[...].reshape(bt * lt, d)
    gate = _dot(hb, wg_ref[...]).reshape(bt, lt, tf)
    val = _dot(hb, wv_ref[...])
    prev = st_ref[...]
    if has_halo:
        halo_gate = _dot(halo_ref[0], wg_ref[...])
        prev = jnp.where(t == 0, prev, halo_gate[SUBLANES - (FFN_CONV - 1):SUBLANES, :][None])
    gbuf[:, FFN_PAD - (FFN_CONV - 1):FFN_PAD, :] = prev
    gbuf[:, FFN_PAD:FFN_PAD + lt, :] = gate
    so_ref[...] = gate[:, lt - (FFN_CONV - 1):lt, :].reshape(so_ref.shape)
    conv = cw_ref[FFN_CONV - 1:FFN_CONV, :] * gate
    for tap in range(FFN_CONV - 1):
        back = FFN_CONV - 1 - tap
        conv = conv + cw_ref[tap:tap + 1, :] * gbuf[:, FFN_PAD - back:FFN_PAD - back + lt, :]
    act = _silu(conv + cb_ref[...]).reshape(bt * lt, tf) * val
    part = _dot(act.astype(BF16), wd_ref[...])

    @pl.when(f == 0)
    def _():
        acc[...] = part

    @pl.when(f > 0)
    def _():
        acc[...] += part

    @pl.when(f == pl.num_programs(2) - 1)
    def _():
        x2 = x1_ref[...] + g2_ref[...] * acc[...].reshape(bt, lt, d)
        y = x2 * lax.rsqrt(jnp.mean(x2 * x2, axis=-1, keepdims=True) + EPS) * fn_ref[...]
        y_ref[...] = y


def _ffn(h2, x1, st0, mod3, mod_row0, w_up16, w_down16, conv_w, conv_b, final_norm, bt, lt):
    b, l, d = h2.shape
    tf = 512
    nf = D_FF // tf
    mrow = mod_row0 // bt
    has_halo = lt < l
    halo_blocks = lt // SUBLANES
    kern = functools.partial(_ffn_kernel, has_halo=has_halo)
    return pl.pallas_call(
        kern,
        grid=(b // bt, l // lt, nf),
        in_specs=[
            pl.BlockSpec((bt, lt, d), lambda i, t, f: (i, t, 0)),
            pl.BlockSpec((1, SUBLANES, d), lambda i, t, f: (i, jnp.maximum(t * halo_blocks - 1, 0), 0)),
            pl.BlockSpec((d, tf), lambda i, t, f: (0, f)),
            pl.BlockSpec((d, tf), lambda i, t, f: (0, nf + f)),
            pl.BlockSpec((tf, d), lambda i, t, f: (f, 0)),
            pl.BlockSpec((FFN_CONV, tf), lambda i, t, f: (0, f)),
            pl.BlockSpec((1, tf), lambda i, t, f: (0, f)),
            pl.BlockSpec((bt, FFN_CONV - 1, tf), lambda i, t, f: (i, 0, f)),
            pl.BlockSpec((bt, lt, d), lambda i, t, f: (i, t, 0)),
            pl.BlockSpec((bt, 1, d), lambda i, t, f: (mrow + i, 0, 5)),
            pl.BlockSpec((1, 1, d), lambda i, t, f: (0, 0, 0)),
        ],
        out_specs=[
            pl.BlockSpec((bt, lt, d), lambda i, t, f: (i, t, 0)),
            pl.BlockSpec((bt, 1, FFN_CONV - 1, tf), lambda i, t, f: (i, t, 0, f)),
        ],
        out_shape=[
            jax.ShapeDtypeStruct((b, l, d), F32),
            jax.ShapeDtypeStruct((b, l // lt, FFN_CONV - 1, D_FF), F32),
        ],
        scratch_shapes=[
            pltpu.VMEM((bt * lt, d), F32),
            pltpu.VMEM((bt, lt + FFN_PAD, tf), F32),
        ],
        compiler_params=_params("arbitrary", "arbitrary", "arbitrary"),
        name="ffn",
    )(h2, h2, w_up16, w_up16, w_down16, conv_w, conv_b.reshape(1, D_FF), st0, x1, mod3,
      final_norm.reshape(1, 1, d))


def _trunk(x, mod3, mod_row0, states, weights, cfg):
    (w_main, w_small, norm1, gla_wg, gla_bg, gla_norm, gdn_conv_w, gate_vec, gdn_norm, w_out16,
     norm2, w_up16, w_down16, ffn_conv_w, ffn_conv_b, final_norm) = weights
    s_gla, s_gdn, s_conv, s_ffn = states
    proj, small = _in_proj(x, mod3, mod_row0, norm1, w_main, w_small, cfg["in_bt"], cfg["in_lt"])
    o_a, n_gla = _gla_mix(proj, small, s_gla, gla_wg, gla_bg, gla_norm, cfg["mix_lt"])
    o_b, n_gdn, n_conv = _gdn_mix(proj, small, s_conv, s_gdn, gdn_conv_w, gate_vec, gdn_norm,
                                  cfg["mix_lt"])
    x1, h2 = _out_proj(x, o_a, o_b, mod3, mod_row0, w_out16, norm2, cfg["ffn_bt"], cfg["ffn_lt"])
    y, n_ffn = _ffn(h2, x1, s_ffn, mod3, mod_row0, w_up16, w_down16, ffn_conv_w, ffn_conv_b,
                    final_norm, cfg["ffn_bt"], cfg["ffn_lt"])
    return y, n_gla[None], n_gdn[None], n_conv[None], n_ffn[:, -1][None]


def kernel(x_prompt, x_sample, c_prompt, c_sample, state_gla, state_gdn, state_gdn_conv, state_ffn_conv, w_ada, b_ada, norm1, w_in, gla_wg, gla_bg, gla_norm, gdn_conv_w, gdn_a_log, gdn_dt_bias, gdn_norm, w_out, norm2, w_up, ffn_conv_w, ffn_conv_b, w_down, final_norm):
    bp = x_prompt.shape[0]
    bs = x_sample.shape[0]

    w_in0 = w_in[0]
    w_main = jnp.concatenate(
        [w_in0[:, :GLA_COLS], w_in0[:, IN_OFF_GDN:IN_OFF_BETA]], axis=1).astype(BF16)
    w_small = jnp.concatenate(
        [w_in0[:, IN_OFF_LR:IN_OFF_GDN], w_in0[:, IN_OFF_BETA:],
         jnp.zeros((D_MODEL, SMALL_COLS - GLA_GATE_RANK - 2 * GDN_HEADS), F32)], axis=1).astype(BF16)
    w_out16 = w_out[0].astype(BF16)
    w_up16 = w_up[0].astype(BF16)
    w_down16 = w_down[0].astype(BF16)
    gate_vec = jnp.zeros((2, SMALL_COLS), F32)
    gate_vec = gate_vec.at[0, ALPHA_OFF:ALPHA_OFF + GDN_HEADS].set(gdn_a_log[0])
    gate_vec = gate_vec.at[1, ALPHA_OFF:ALPHA_OFF + GDN_HEADS].set(gdn_dt_bias[0])

    c_all = jnp.concatenate([c_sample, c_prompt], axis=0)
    mod = _ada_mod(c_all, w_ada[0], b_ada[0])
    mod3 = mod.reshape(bs + bp, 1, N_MOD * D_MODEL)

    weights = (w_main, w_small, norm1[0], gla_wg[0], gla_bg[0], gla_norm[0], gdn_conv_w[0], gate_vec,
               gdn_norm[0], w_out16, norm2[0], w_up16, w_down16, ffn_conv_w[0], ffn_conv_b[0],
               final_norm)

    fresh = (jnp.zeros((bp, GLA_HEADS, GLA_DK, GLA_DV), F32),
             jnp.zeros((bp, GDN_HEADS, GDN_DK, GDN_DV), F32),
             jnp.zeros((bp, GDN_CONV - 1, GDN_CONV_DIM), F32),
             jnp.zeros((bp, FFN_CONV - 1, D_FF), F32))
    cfg_p = dict(in_bt=1, in_lt=1024, mix_lt=256, ffn_bt=1, ffn_lt=512)
    y_p, p_gla, p_gdn, p_conv, p_ffn = _trunk(x_prompt, mod3, bs, fresh, weights, cfg_p)

    carried = (state_gla[0], state_gdn[0], state_gdn_conv[0], state_ffn_conv[0])
    cfg_s = dict(in_bt=bs, in_lt=x_sample.shape[1], mix_lt=x_sample.shape[1], ffn_bt=8,
                 ffn_lt=x_sample.shape[1])
    y_s, s_gla, s_gdn, s_conv, s_ffn = _trunk(x_sample, mod3, 0, carried, weights, cfg_s)
    return (y_p, y_s, p_gla, p_gdn, p_conv, p_ffn, s_gla, s_gdn, s_conv, s_ffn)
```

```python
import functools

import jax
import jax.numpy as jnp
from jax import lax
from jax.experimental import pallas as pl
from jax.experimental.pallas import tpu as pltpu

F32 = jnp.float32
BF16 = jnp.bfloat16

D_MODEL = 2048
CHUNK = 64
GLA_HEADS = 4
GLA_DK = 128
GLA_DV = 256
GLA_WIDTH = GLA_HEADS * GLA_DV
GLA_GATE_RANK = 16
GLA_GATE_NORM = 16.0
GDN_HEADS = 8
GDN_DK = 128
GDN_DV = 128
GDN_WIDTH = GDN_HEADS * GDN_DV
GDN_CONV = 4
GDN_CONV_DIM = 3 * GDN_WIDTH
D_FF = 5632
FFN_CONV = 3
N_MOD = 6
EPS = 1e-6

GLA_COLS = 2 * GLA_HEADS * GLA_DK + 2 * GLA_WIDTH
GDN_COLS = 4 * GDN_WIDTH
MAIN_COLS = GLA_COLS + GDN_COLS
SMALL_COLS = 128
LR_OFF, BETA_OFF, ALPHA_OFF = 0, GLA_GATE_RANK, GLA_GATE_RANK + GDN_HEADS
IN_OFF_LR = GLA_COLS
IN_OFF_GDN = IN_OFF_LR + GLA_GATE_RANK
IN_OFF_BETA = IN_OFF_GDN + GDN_COLS

SUBLANES = 8
VMEM_LIMIT_BYTES = 56 * 1024 * 1024

NT_DIMS = (((1,), (1,)), ((), ()))
TN_DIMS = (((0,), (0,)), ((), ()))


def _dot(a, b):
    return jnp.dot(a, b, preferred_element_type=F32)


def _dot_nt(a, b):
    return lax.dot_general(a, b, NT_DIMS, preferred_element_type=F32)


def _dot_tn(a, b):
    return lax.dot_general(a, b, TN_DIMS, preferred_element_type=F32)


def _dot_f32(a, b):
    return jnp.dot(a, b, preferred_element_type=F32, precision=lax.Precision.HIGHEST)


def _dot_many(a_list, b_list):
    return [_dot(a.astype(BF16), b.astype(BF16)) for a, b in zip(a_list, b_list)]


def _silu(x):
    return x * jax.nn.sigmoid(x)


def _softplus(x):
    return jnp.maximum(x, 0.0) + jnp.log1p(jnp.exp(-jnp.abs(x)))


def _log_sigmoid(x):
    return jnp.minimum(x, 0.0) - jnp.log1p(jnp.exp(-jnp.abs(x)))


def _params(*semantics):
    return pltpu.CompilerParams(dimension_semantics=semantics, vmem_limit_bytes=VMEM_LIMIT_BYTES)


def _tri_masks():
    row = lax.broadcasted_iota(jnp.int32, (CHUNK, CHUNK), 0)
    col = lax.broadcasted_iota(jnp.int32, (CHUNK, CHUNK), 1)
    return row >= col, row > col


def _ada_kernel(c_ref, w_ref, b_ref, o_ref):
    s = _silu(c_ref[...]).astype(BF16)
    o_ref[...] = _dot(s, w_ref[...].astype(BF16)) + b_ref[...]


def _ada_mod(c_all, w_ada, b_ada):
    rows = c_all.shape[0]
    n = w_ada.shape[1]
    tn = 1024
    return pl.pallas_call(
        _ada_kernel,
        grid=(n // tn,),
        in_specs=[
            pl.BlockSpec((rows, D_MODEL), lambda j: (0, 0)),
            pl.BlockSpec((D_MODEL, tn), lambda j: (0, j)),
            pl.BlockSpec((1, tn), lambda j: (0, j)),
        ],
        out_specs=pl.BlockSpec((rows, tn), lambda j: (0, j)),
        out_shape=jax.ShapeDtypeStruct((rows, n), F32),
        compiler_params=_params("arbitrary"),
        name="ada_mod",
    )(c_all, w_ada, b_ada.reshape(1, n))


def _in_proj_kernel(x_ref, sc_ref, sh_ref, n1_ref, w_ref, ws_ref, o_ref, os_ref, h_scr):
    bt, lt, d = x_ref.shape

    @pl.when(pl.program_id(2) == 0)
    def _():
        x = x_ref[...]
        y = x * lax.rsqrt(jnp.mean(x * x, axis=-1, keepdims=True) + EPS) * n1_ref[...]
        h = y * (1.0 + sc_ref[...]) + sh_ref[...]
        hb = h.reshape(bt * lt, d).astype(BF16)
        h_scr[...] = hb
        os_ref[...] = _dot(hb, ws_ref[...]).reshape(os_ref.shape)

    o_ref[...] = _dot(h_scr[...], w_ref[...]).reshape(o_ref.shape)


def _in_proj(x, mod3, mod_row0, norm1, w_main, w_small, bt, lt):
    b, l, d = x.shape
    tn = 1024
    mrow = mod_row0 // bt
    grid = (b // bt, l // lt, MAIN_COLS // tn)
    return pl.pallas_call(
        _in_proj_kernel,
        grid=grid,
        in_specs=[
            pl.BlockSpec((bt, lt, d), lambda i, t, j: (i, t, 0)),
            pl.BlockSpec((bt, 1, d), lambda i, t, j: (mrow + i, 0, 1)),
            pl.BlockSpec((bt, 1, d), lambda i, t, j: (mrow + i, 0, 0)),
            pl.BlockSpec((1, 1, d), lambda i, t, j: (0, 0, 0)),
            pl.BlockSpec((d, tn), lambda i, t, j: (0, j)),
            pl.BlockSpec((d, SMALL_COLS), lambda i, t, j: (0, 0)),
        ],
        out_specs=[
            pl.BlockSpec((bt, lt, tn), lambda i, t, j: (i, t, j)),
            pl.BlockSpec((bt, lt, SMALL_COLS), lambda i, t, j: (i, t, 0)),
        ],
        out_shape=[
            jax.ShapeDtypeStruct((b, l, MAIN_COLS), F32),
            jax.ShapeDtypeStruct((b, l, SMALL_COLS), F32),
        ],
        scratch_shapes=[pltpu.VMEM((bt * lt, d), BF16)],
        compiler_params=_params("arbitrary", "arbitrary", "arbitrary"),
        name="in_proj",
    )(x, mod3, mod3, norm1.reshape(1, 1, d), w_main, w_small)


def _gla_kernel(q_ref, k_ref, v_ref, r_ref, sm_ref, wg_ref, bg_ref, gn_ref, s0_ref,
                o_ref, s_ref, st_scr):
    t = pl.program_id(1)
    lt = q_ref.shape[1]
    causal, _ = _tri_masks()
    tri = causal.astype(F32)
    scale = GLA_DK ** -0.5

    @pl.when(t == 0)
    def _():
        for h in range(GLA_HEADS):
            st_scr[h] = s0_ref[0, h].T

    wg = wg_ref[...].astype(BF16)
    heads = range(GLA_HEADS)
    for c in range(lt // CHUNK):
        rows = pl.ds(c * CHUNK, CHUNK)
        a_lr = sm_ref[0, rows, LR_OFF:LR_OFF + GLA_GATE_RANK].astype(BF16)
        log_a = _log_sigmoid(_dot(a_lr, wg) + bg_ref[...]) / GLA_GATE_NORM
        big_g = _dot_f32(tri, log_a)
        g_mid = big_g[CHUNK // 2 - 1:CHUNK // 2, :]
        g_last = big_g[CHUNK - 1:CHUNK, :]
        e_q = jnp.exp(big_g - g_mid)
        e_k = jnp.exp(g_mid - big_g)
        e_g = jnp.exp(big_g)
        e_kl = jnp.exp(g_last - big_g)
        e_l = jnp.exp(g_last)
        kcs = [slice(h * GLA_DK, (h + 1) * GLA_DK) for h in heads]
        vcs = [slice(h * GLA_DV, (h + 1) * GLA_DV) for h in heads]
        q = [q_ref[0, rows, kc] * scale for kc in kcs]
        k = [k_ref[0, rows, kc] for kc in kcs]
        v = [v_ref[0, rows, vc].astype(BF16) for vc in vcs]
        st = [st_scr[h] for h in heads]
        attn = [_dot_nt((qq * e_q[:, kc]).astype(BF16), (kk * e_k[:, kc]).astype(BF16))
                for qq, kk, kc in zip(q, k, kcs)]
        q_s = [_dot_nt((qq * e_g[:, kc]).astype(BF16), ss.astype(BF16))
               for qq, ss, kc in zip(q, st, kcs)]
        v_k = [_dot_tn(vv, (kk * e_kl[:, kc]).astype(BF16)) for vv, kk, kc in zip(v, k, kcs)]
        a_v = [_dot(jnp.where(causal, a, 0.0).astype(BF16), vv) for a, vv in zip(attn, v)]
        for h in heads:
            st_scr[h] = st[h] * e_l[:, kcs[h]] + v_k[h]
            o = a_v[h] + q_s[h]
            o = o * lax.rsqrt(jnp.mean(o * o, axis=-1, keepdims=True) + EPS) * gn_ref[...]
            o = o * _silu(r_ref[0, rows, vcs[h]])
            o_ref[0, rows, vcs[h]] = o.astype(o_ref.dtype)

    @pl.when(t == pl.num_programs(1) - 1)
    def _():
        for h in range(GLA_HEADS):
            s_ref[0, h] = st_scr[h].T


def _gla_mix(proj, small, s0, gla_wg, gla_bg, gla_norm, lt):
    b, l, _ = proj.shape
    kw = GLA_HEADS * GLA_DK
    return pl.pallas_call(
        _gla_kernel,
        grid=(b, l // lt),
        in_specs=[
            pl.BlockSpec((1, lt, kw), lambda i, t: (i, t, 0)),
            pl.BlockSpec((1, lt, kw), lambda i, t: (i, t, 1)),
            pl.BlockSpec((1, lt, GLA_WIDTH), lambda i, t: (i, t, 1)),
            pl.BlockSpec((1, lt, GLA_WIDTH), lambda i, t: (i, t, 2)),
            pl.BlockSpec((1, lt, SMALL_COLS), lambda i, t: (i, t, 0)),
            pl.BlockSpec((GLA_GATE_RANK, kw), lambda i, t: (0, 0)),
            pl.BlockSpec((1, kw), lambda i, t: (0, 0)),
            pl.BlockSpec((1, GLA_DV), lambda i, t: (0, 0)),
            pl.BlockSpec((1, GLA_HEADS, GLA_DK, GLA_DV), lambda i, t: (i, 0, 0, 0)),
        ],
        out_specs=[
            pl.BlockSpec((1, lt, GLA_WIDTH), lambda i, t: (i, t, 0)),
            pl.BlockSpec((1, GLA_HEADS, GLA_DK, GLA_DV), lambda i, t: (i, 0, 0, 0)),
        ],
        out_shape=[
            jax.ShapeDtypeStruct((b, l, GLA_WIDTH), BF16),
            jax.ShapeDtypeStruct((b, GLA_HEADS, GLA_DK, GLA_DV), F32),
        ],
        scratch_shapes=[pltpu.VMEM((GLA_HEADS, GLA_DV, GLA_DK), F32)],
        compiler_params=_params("arbitrary", "arbitrary"),
        name="gla_mix",
    )(proj, proj, proj, proj, small, gla_wg, gla_bg.reshape(1, kw), gla_norm.reshape(1, GLA_DV), s0)


CONV_PAD = SUBLANES
INV_BASE_LOG2 = 3


def _inverse_masks():
    row = lax.broadcasted_iota(jnp.int32, (CHUNK, CHUNK), 0)
    col = lax.broadcasted_iota(jnp.int32, (CHUNK, CHUNK), 1)

    def same_block(log2):
        return jnp.right_shift(row, log2) == jnp.right_shift(col, log2)

    base = same_block(INV_BASE_LOG2)
    merges = []
    log2 = INV_BASE_LOG2
    while (1 << log2) < CHUNK:
        merges.append(same_block(log2 + 1) & jnp.logical_not(same_block(log2)))
        log2 += 1
    return base, merges


def _unit_lower_inverse_minus_eye(lower, base, merges):
    neg = [jnp.where(base, -l, 0.0) for l in lower]
    x = neg
    p = neg
    for _ in range(INV_BASE_LOG2 - 1):
        p = _dot_many(p, p)
        xp = _dot_many(x, p)
        x = [a + b + c for a, b, c in zip(x, p, xp)]
    for m in merges:
        c = [jnp.where(m, l, 0.0) for l in lower]
        w = [a + b for a, b in zip(c, _dot_many(x, c))]
        wx = _dot_many(w, x)
        x = [a - (b + d) for a, b, d in zip(x, w, wx)]
    return x


def _gdn_kernel(x_ref, g_ref, sm_ref, cw_ref, ga_ref, gn_ref, c0_ref, s0_ref,
                o_ref, s_ref, c_ref, xbuf):
    t = pl.program_id(1)
    lt = x_ref.shape[1]
    causal, strict = _tri_masks()
    tri = causal.astype(F32)
    inv_base, inv_merges = _inverse_masks()
    scale = GDN_DK ** -0.5

    @pl.when(t == 0)
    def _():
        xbuf[0:CONV_PAD, :] = jnp.zeros((CONV_PAD, GDN_CONV_DIM), F32)
        xbuf[CONV_PAD - (GDN_CONV - 1):CONV_PAD, :] = c0_ref[0]
        s_ref[...] = s0_ref[...]

    @pl.when(t > 0)
    def _():
        xbuf[0:CONV_PAD, :] = xbuf[lt:lt + CONV_PAD, :]

    xbuf[CONV_PAD:CONV_PAD + lt, :] = x_ref[0]
    c_ref[0] = xbuf[lt + CONV_PAD - (GDN_CONV - 1):lt + CONV_PAD, :]

    def conv_silu(r0, cols):
        acc = cw_ref[GDN_CONV - 1:GDN_CONV, cols] * xbuf[r0 + CONV_PAD:r0 + CONV_PAD + CHUNK, cols]
        for tap in range(GDN_CONV - 1):
            back = GDN_CONV - 1 - tap
            acc = acc + cw_ref[tap:tap + 1, cols] * xbuf[r0 + CONV_PAD - back:r0 + CONV_PAD - back + CHUNK, cols]
        return _silu(acc)

    def l2norm(x):
        return x * lax.rsqrt(jnp.sum(x * x, axis=-1, keepdims=True) + EPS)

    heads = range(GDN_HEADS)
    for c in range(lt // CHUNK):
        r0 = c * CHUNK
        rows = pl.ds(r0, CHUNK)
        sm = sm_ref[0, rows, :]
        gate = -jnp.exp(ga_ref[0:1, :]) * _softplus(sm + ga_ref[1:2, :])
        beta = jax.nn.sigmoid(sm)
        big_g = _dot_f32(tri, gate)
        big_gt = big_g.T
        g_col = [big_g[:, ALPHA_OFF + h:ALPHA_OFF + h + 1] for h in heads]
        g_row = [big_gt[ALPHA_OFF + h:ALPHA_OFF + h + 1, :] for h in heads]
        b_col = [beta[:, BETA_OFF + h:BETA_OFF + h + 1] for h in heads]
        g_last = [g[CHUNK - 1:CHUNK, :] for g in g_col]
        decay = [jnp.where(causal, jnp.exp(gc - gr), 0.0) for gc, gr in zip(g_col, g_row)]
        e_g = [jnp.exp(g) for g in g_col]
        q = [l2norm(conv_silu(r0, slice(h * GDN_DK, (h + 1) * GDN_DK))) * scale for h in heads]
        k = [l2norm(conv_silu(r0, slice(GDN_WIDTH + h * GDN_DK, GDN_WIDTH + (h + 1) * GDN_DK)))
             for h in heads]
        v = [conv_silu(r0, slice(2 * GDN_WIDTH + h * GDN_DV, 2 * GDN_WIDTH + (h + 1) * GDN_DV))
             for h in heads]
        kb = [kk * b for kk, b in zip(k, b_col)]
        k16 = [kk.astype(BF16) for kk in k]
        kk_t = [_dot_nt(a.astype(BF16), b) for a, b in zip(kb, k16)]
        qk_t = [_dot_nt(a.astype(BF16), b) for a, b in zip(q, k16)]
        lower = [jnp.where(strict, m * d, 0.0) for m, d in zip(kk_t, decay)]
        attn = [jnp.where(causal, m * d, 0.0).astype(BF16) for m, d in zip(qk_t, decay)]
        xinv = _unit_lower_inverse_minus_eye(lower, inv_base, inv_merges)
        rhs = [jnp.concatenate([vv * b, kbb * e], axis=-1)
               for vv, b, kbb, e in zip(v, b_col, kb, e_g)]
        sol = [r + xr for r, xr in zip(rhs, _dot_many(xinv, rhs))]
        s = [s_ref[0, h] for h in heads]
        s16 = [ss.astype(BF16) for ss in s]
        k_s = [_dot(so[:, GDN_DV:].astype(BF16), ss) for so, ss in zip(sol, s16)]
        q_s = [_dot((qq * e).astype(BF16), ss) for qq, e, ss in zip(q, e_g, s16)]
        u16 = [(so[:, :GDN_DV] - ks).astype(BF16) for so, ks in zip(sol, k_s)]
        a_u = [_dot(a, uu) for a, uu in zip(attn, u16)]
        k_u = [_dot_tn((kk * jnp.exp(gl - gc)).astype(BF16), uu)
               for kk, gl, gc, uu in zip(k, g_last, g_col, u16)]
        for h in heads:
            hc = slice(h * GDN_DV, (h + 1) * GDN_DV)
            s_ref[0, h] = jnp.exp(g_last[h]) * s[h] + k_u[h]
            o = q_s[h] + a_u[h]
            o = o * lax.rsqrt(jnp.mean(o * o, axis=-1, keepdims=True) + EPS) * gn_ref[...]
            o = o * _silu(g_ref[0, rows, hc])
            o_ref[0, rows, hc] = o.astype(o_ref.dtype)


def _gdn_mix(proj, small, c0, s0, conv_w, gate_vec, gdn_norm, lt):
    b, l, _ = proj.shape
    return pl.pallas_call(
        _gdn_kernel,
        grid=(b, l // lt),
        in_specs=[
            pl.BlockSpec((1, lt, GDN_CONV_DIM), lambda i, t: (i, t, 1)),
            pl.BlockSpec((1, lt, GDN_WIDTH), lambda i, t: (i, t, 6)),
            pl.BlockSpec((1, lt, SMALL_COLS), lambda i, t: (i, t, 0)),
            pl.BlockSpec((GDN_CONV, GDN_CONV_DIM), lambda i, t: (0, 0)),
            pl.BlockSpec((2, SMALL_COLS), lambda i, t: (0, 0)),
            pl.BlockSpec((1, GDN_DV), lambda i, t: (0, 0)),
            pl.BlockSpec((1, GDN_CONV - 1, GDN_CONV_DIM), lambda i, t: (i, 0, 0)),
            pl.BlockSpec((1, GDN_HEADS, GDN_DK, GDN_DV), lambda i, t: (i, 0, 0, 0)),
        ],
        out_specs=[
            pl.BlockSpec((1, lt, GDN_WIDTH), lambda i, t: (i, t, 0)),
            pl.BlockSpec((1, GDN_HEADS, GDN_DK, GDN_DV), lambda i, t: (i, 0, 0, 0)),
            pl.BlockSpec((1, GDN_CONV - 1, GDN_CONV_DIM), lambda i, t: (i, 0, 0)),
        ],
        out_shape=[
            jax.ShapeDtypeStruct((b, l, GDN_WIDTH), BF16),
            jax.ShapeDtypeStruct((b, GDN_HEADS, GDN_DK, GDN_DV), F32),
            jax.ShapeDtypeStruct((b, GDN_CONV - 1, GDN_CONV_DIM), F32),
        ],
        scratch_shapes=[pltpu.VMEM((lt + CONV_PAD, GDN_CONV_DIM), F32)],
        compiler_params=_params("arbitrary", "arbitrary"),
        name="gdn_mix",
    )(proj, proj, small, conv_w, gate_vec, gdn_norm.reshape(1, GDN_DV), c0, s0)


def _out_proj_kernel(x_ref, oa_ref, ob_ref, w_ref, g1_ref, sc_ref, sh_ref, n2_ref, x1_ref, h2_ref,
                     *, n_sub):
    bt, lt, d = x_ref.shape
    if bt == 1:
        ls = lt // n_sub
        subs = [(slice(0, 1), slice(i * ls, (i + 1) * ls)) for i in range(n_sub)]
    else:
        bs = bt // n_sub
        subs = [(slice(i * bs, (i + 1) * bs), slice(0, lt)) for i in range(n_sub)]
    mixes = []
    for bsl, lsl in subs:
        oa = oa_ref[bsl, lsl, :]
        nb, nl, _ = oa.shape
        oa = oa.reshape(nb * nl, GLA_WIDTH)
        ob = ob_ref[bsl, lsl, :].reshape(nb * nl, GDN_WIDTH)
        mix = _dot(oa, w_ref[0:GLA_WIDTH, :]) + _dot(ob, w_ref[GLA_WIDTH:GLA_WIDTH + GDN_WIDTH, :])
        mixes.append(mix.reshape(nb, nl, d))
    for (bsl, lsl), mix in zip(subs, mixes):
        x1 = x_ref[bsl, lsl, :] + g1_ref[bsl] * mix
        x1_ref[bsl, lsl, :] = x1
        y = x1 * lax.rsqrt(jnp.mean(x1 * x1, axis=-1, keepdims=True) + EPS) * n2_ref[...]
        h2_ref[bsl, lsl, :] = (y * (1.0 + sc_ref[bsl]) + sh_ref[bsl]).astype(h2_ref.dtype)


def _out_proj(x, o_a, o_b, mod3, mod_row0, w_out16, norm2, bt, lt):
    b, l, d = x.shape
    mrow = mod_row0 // bt
    return pl.pallas_call(
        functools.partial(_out_proj_kernel, n_sub=2),
        grid=(b // bt, l // lt),
        in_specs=[
            pl.BlockSpec((bt, lt, d), lambda i, t: (i, t, 0)),
            pl.BlockSpec((bt, lt, GLA_WIDTH), lambda i, t: (i, t, 0)),
            pl.BlockSpec((bt, lt, GDN_WIDTH), lambda i, t: (i, t, 0)),
            pl.BlockSpec((GLA_WIDTH + GDN_WIDTH, d), lambda i, t: (0, 0)),
            pl.BlockSpec((bt, 1, d), lambda i, t: (mrow + i, 0, 2)),
            pl.BlockSpec((bt, 1, d), lambda i, t: (mrow + i, 0, 4)),
            pl.BlockSpec((bt, 1, d), lambda i, t: (mrow + i, 0, 3)),
            pl.BlockSpec((1, 1, d), lambda i, t: (0, 0, 0)),
        ],
        out_specs=[
            pl.BlockSpec((bt, lt, d), lambda i, t: (i, t, 0)),
            pl.BlockSpec((bt, lt, d), lambda i, t: (i, t, 0)),
        ],
        out_shape=[
            jax.ShapeDtypeStruct((b, l, d), F32),
            jax.ShapeDtypeStruct((b, l, d), BF16),
        ],
        compiler_params=_params("arbitrary", "arbitrary"),
        name="out_proj",
    )(x, o_a, o_b, w_out16, mod3, mod3, mod3, norm2.reshape(1, 1, d))


FFN_PAD = SUBLANES


def _ffn_kernel(h_ref, halo_ref, wg_ref, wv_ref, wd_ref, cw_ref, cb_ref, st_ref, x1_ref, g2_ref,
                fn_ref, y_ref, so_ref, acc, gbuf, *, has_halo, n_sub):
    t = pl.program_id(1)
    f = pl.program_id(2)
    bt, lt, d = h_ref.shape
    tf = wg_ref.shape[1]
    if bt == 1:
        ls = lt // n_sub
        subs = [(slice(0, 1), slice(i * ls, (i + 1) * ls)) for i in range(n_sub)]
    else:
        bs = bt // n_sub
        subs = [(slice(i * bs, (i + 1) * bs), slice(0, lt)) for i in range(n_sub)]

    @pl.when(f == 0)
    def _():
        acc[...] = jnp.zeros(acc.shape, F32)

    prev = st_ref[...]
    if has_halo:
        halo_gate = _dot(halo_ref[0], wg_ref[...])
        prev = jnp.where(t == 0, prev, halo_gate[SUBLANES - (FFN_CONV - 1):SUBLANES, :][None])
    gbuf[:, FFN_PAD - (FFN_CONV - 1):FFN_PAD, :] = prev
    gates, vals = [], []
    for bsl, lsl in subs:
        hb = h_ref[bsl, lsl, :]
        nb, nl, _ = hb.shape
        hb = hb.reshape(nb * nl, d)
        gate = _dot(hb, wg_ref[...]).reshape(nb, nl, tf)
        gbuf[bsl, FFN_PAD + lsl.start:FFN_PAD + lsl.stop, :] = gate
        gates.append(gate)
        vals.append(_dot(hb, wv_ref[...]))
    so_ref[...] = gbuf[:, FFN_PAD + lt - (FFN_CONV - 1):FFN_PAD + lt, :].reshape(so_ref.shape)
    acts = []
    for (bsl, lsl), gate, val in zip(subs, gates, vals):
        nb, nl, _ = gate.shape
        conv = cw_ref[FFN_CONV - 1:FFN_CONV, :] * gate
        for tap in range(FFN_CONV - 1):
            back = FFN_CONV - 1 - tap
            conv = conv + cw_ref[tap:tap + 1, :] * gbuf[bsl, FFN_PAD + lsl.start - back:FFN_PAD + lsl.stop - back, :]
        acts.append((_silu(conv + cb_ref[...]).reshape(nb * nl, tf) * val).astype(BF16))
    parts = [_dot(act, wd_ref[...]) for act in acts]
    r0 = 0
    for part in parts:
        acc[r0:r0 + part.shape[0], :] += part
        r0 += part.shape[0]

    @pl.when(f == pl.num_programs(2) - 1)
    def _():
        x2 = x1_ref[...] + g2_ref[...] * acc[...].reshape(bt, lt, d)
        y = x2 * lax.rsqrt(jnp.mean(x2 * x2, axis=-1, keepdims=True) + EPS) * fn_ref[...]
        y_ref[...] = y


def _ffn(h2, x1, st0, mod3, mod_row0, w_up16, w_down16, conv_w, conv_b, final_norm, bt, lt):
    b, l, d = h2.shape
    tf = 512
    nf = D_FF // tf
    mrow = mod_row0 // bt
    has_halo = lt < l
    halo_blocks = lt // SUBLANES
    kern = functools.partial(_ffn_kernel, has_halo=has_halo, n_sub=2)
    return pl.pallas_call(
        kern,
        grid=(b // bt, l // lt, nf),
        in_specs=[
            pl.BlockSpec((bt, lt, d), lambda i, t, f: (i, t, 0)),
            pl.BlockSpec((1, SUBLANES, d), lambda i, t, f: (i, jnp.maximum(t * halo_blocks - 1, 0), 0)),
            pl.BlockSpec((d, tf), lambda i, t, f: (0, f)),
            pl.BlockSpec((d, tf), lambda i, t, f: (0, nf + f)),
            pl.BlockSpec((tf, d), lambda i, t, f: (f, 0)),
            pl.BlockSpec((FFN_CONV, tf), lambda i, t, f: (0, f)),
            pl.BlockSpec((1, tf), lambda i, t, f: (0, f)),
            pl.BlockSpec((bt, FFN_CONV - 1, tf), lambda i, t, f: (i, 0, f)),
            pl.BlockSpec((bt, lt, d), lambda i, t, f: (i, t, 0)),
            pl.BlockSpec((bt, 1, d), lambda i, t, f: (mrow + i, 0, 5)),
            pl.BlockSpec((1, 1, d), lambda i, t, f: (0, 0, 0)),
        ],
        out_specs=[
            pl.BlockSpec((bt, lt, d), lambda i, t, f: (i, t, 0)),
            pl.BlockSpec((bt, 1, FFN_CONV - 1, tf), lambda i, t, f: (i, t, 0, f)),
        ],
        out_shape=[
            jax.ShapeDtypeStruct((b, l, d), F32),
            jax.ShapeDtypeStruct((b, l // lt, FFN_CONV - 1, D_FF), F32),
        ],
        scratch_shapes=[
            pltpu.VMEM((bt * lt, d), F32),
            pltpu.VMEM((bt, lt + FFN_PAD, tf), F32),
        ],
        compiler_params=_params("arbitrary", "arbitrary", "arbitrary"),
        name="ffn",
    )(h2, h2, w_up16, w_up16, w_down16, conv_w, conv_b.reshape(1, D_FF), st0, x1, mod3,
      final_norm.reshape(1, 1, d))


def _trunk(x, mod3, mod_row0, states, weights, cfg):
    (w_main, w_small, norm1, gla_wg, gla_bg, gla_norm, gdn_conv_w, gate_vec, gdn_norm, w_out16,
     norm2, w_up16, w_down16, ffn_conv_w, ffn_conv_b, final_norm) = weights
    s_gla, s_gdn, s_conv, s_ffn = states
    proj, small = _in_proj(x, mod3, mod_row0, norm1, w_main, w_small, cfg["in_bt"], cfg["in_lt"])
    o_a, n_gla = _gla_mix(proj, small, s_gla, gla_wg, gla_bg, gla_norm, cfg["mix_lt"])
    o_b, n_gdn, n_conv = _gdn_mix(proj, small, s_conv, s_gdn, gdn_conv_w, gate_vec, gdn_norm,
                                  cfg["mix_lt"])
    x1, h2 = _out_proj(x, o_a, o_b, mod3, mod_row0, w_out16, norm2, cfg["ffn_bt"], cfg["ffn_lt"])
    y, n_ffn = _ffn(h2, x1, s_ffn, mod3, mod_row0, w_up16, w_down16, ffn_conv_w, ffn_conv_b,
                    final_norm, cfg["ffn_bt"], cfg["ffn_lt"])
    return y, n_gla[None], n_gdn[None], n_conv[None], n_ffn[:, -1][None]


def kernel(x_prompt, x_sample, c_prompt, c_sample, state_gla, state_gdn, state_gdn_conv, state_ffn_conv, w_ada, b_ada, norm1, w_in, gla_wg, gla_bg, gla_norm, gdn_conv_w, gdn_a_log, gdn_dt_bias, gdn_norm, w_out, norm2, w_up, ffn_conv_w, ffn_conv_b, w_down, final_norm):
    bp = x_prompt.shape[0]
    bs = x_sample.shape[0]

    w_in0 = w_in[0]
    w_main = jnp.concatenate(
        [w_in0[:, :GLA_COLS], w_in0[:, IN_OFF_GDN:IN_OFF_BETA]], axis=1).astype(BF16)
    w_small = jnp.concatenate(
        [w_in0[:, IN_OFF_LR:IN_OFF_GDN], w_in0[:, IN_OFF_BETA:],
         jnp.zeros((D_MODEL, SMALL_COLS - GLA_GATE_RANK - 2 * GDN_HEADS), F32)], axis=1).astype(BF16)
    w_out16 = w_out[0].astype(BF16)
    w_up16 = w_up[0].astype(BF16)
    w_down16 = w_down[0].astype(BF16)
    gate_vec = jnp.zeros((2, SMALL_COLS), F32)
    gate_vec = gate_vec.at[0, ALPHA_OFF:ALPHA_OFF + GDN_HEADS].set(gdn_a_log[0])
    gate_vec = gate_vec.at[1, ALPHA_OFF:ALPHA_OFF + GDN_HEADS].set(gdn_dt_bias[0])

    c_all = jnp.concatenate([c_sample, c_prompt], axis=0)
    mod = _ada_mod(c_all, w_ada[0], b_ada[0])
    mod3 = mod.reshape(bs + bp, 1, N_MOD * D_MODEL)

    weights = (w_main, w_small, norm1[0], gla_wg[0], gla_bg[0], gla_norm[0], gdn_conv_w[0], gate_vec,
               gdn_norm[0], w_out16, norm2[0], w_up16, w_down16, ffn_conv_w[0], ffn_conv_b[0],
               final_norm)

    fresh = (jnp.zeros((bp, GLA_HEADS, GLA_DK, GLA_DV), F32),
             jnp.zeros((bp, GDN_HEADS, GDN_DK, GDN_DV), F32),
             jnp.zeros((bp, GDN_CONV - 1, GDN_CONV_DIM), F32),
             jnp.zeros((bp, FFN_CONV - 1, D_FF), F32))
    cfg_p = dict(in_bt=1, in_lt=1024, mix_lt=256, ffn_bt=1, ffn_lt=512)
    y_p, p_gla, p_gdn, p_conv, p_ffn = _trunk(x_prompt, mod3, bs, fresh, weights, cfg_p)

    carried = (state_gla[0], state_gdn[0], state_gdn_conv[0], state_ffn_conv[0])
    cfg_s = dict(in_bt=bs, in_lt=x_sample.shape[1], mix_lt=x_sample.shape[1], ffn_bt=8,
                 ffn_lt=x_sample.shape[1])
    y_s, s_gla, s_gdn, s_conv, s_ffn = _trunk(x_sample, mod3, 0, carried, weights, cfg_s)
    return (y_p, y_s, p_gla, p_gdn, p_conv, p_ffn, s_gla, s_gdn, s_conv, s_ffn)
```

```python
import functools

import jax
import jax.numpy as jnp
from jax import lax
from jax.experimental import pallas as pl
from jax.experimental.pallas import tpu as pltpu

F32 = jnp.float32
BF16 = jnp.bfloat16

D_MODEL = 2048
CHUNK = 64
GLA_HEADS = 4
GLA_DK = 128
GLA_DV = 256
GLA_WIDTH = GLA_HEADS * GLA_DV
GLA_GATE_RANK = 16
GLA_GATE_NORM = 16.0
GLA_MILD_LOG_DECAY = 60.0
GDN_HEADS = 8
GDN_DK = 128
GDN_DV = 128
GDN_WIDTH = GDN_HEADS * GDN_DV
GDN_CONV = 4
GDN_CONV_DIM = 3 * GDN_WIDTH
D_FF = 5632
FFN_CONV = 3
N_MOD = 6
EPS = 1e-6

GLA_COLS = 2 * GLA_HEADS * GLA_DK + 2 * GLA_WIDTH
GDN_COLS = 4 * GDN_WIDTH
MAIN_COLS = GLA_COLS + GDN_COLS
SMALL_COLS = 128
LR_OFF, BETA_OFF, ALPHA_OFF = 0, GLA_GATE_RANK, GLA_GATE_RANK + GDN_HEADS
IN_OFF_LR = GLA_COLS
IN_OFF_GDN = IN_OFF_LR + GLA_GATE_RANK
IN_OFF_BETA = IN_OFF_GDN + GDN_COLS

SUBLANES = 8
VMEM_LIMIT_BYTES = 56 * 1024 * 1024

NT_DIMS = (((1,), (1,)), ((), ()))
TN_DIMS = (((0,), (0,)), ((), ()))


def _dot(a, b):
    return jnp.dot(a, b, preferred_element_type=F32)


def _dot_nt(a, b):
    return lax.dot_general(a, b, NT_DIMS, preferred_element_type=F32)


def _dot_tn(a, b):
    return lax.dot_general(a, b, TN_DIMS, preferred_element_type=F32)


def _dot_f32(a, b):
    return jnp.dot(a, b, preferred_element_type=F32, precision=lax.Precision.HIGHEST)


def _dot_many(a_list, b_list):
    return [_dot(a.astype(BF16), b.astype(BF16)) for a, b in zip(a_list, b_list)]


def _silu(x):
    return x * jax.nn.sigmoid(x)


def _softplus(x):
    return jnp.maximum(x, 0.0) + jnp.log1p(jnp.exp(-jnp.abs(x)))


def _log_sigmoid(x):
    return jnp.minimum(x, 0.0) - jnp.log1p(jnp.exp(-jnp.abs(x)))


def _params(*semantics):
    return pltpu.CompilerParams(dimension_semantics=semantics, vmem_limit_bytes=VMEM_LIMIT_BYTES)


def _tri_masks():
    row = lax.broadcasted_iota(jnp.int32, (CHUNK, CHUNK), 0)
    col = lax.broadcasted_iota(jnp.int32, (CHUNK, CHUNK), 1)
    return row >= col, row > col


def _ada_kernel(c_ref, w_ref, b_ref, o_ref):
    s = _silu(c_ref[...]).astype(BF16)
    o_ref[...] = _dot(s, w_ref[...].astype(BF16)) + b_ref[...]


def _ada_mod(c_all, w_ada, b_ada):
    rows = c_all.shape[0]
    n = w_ada.shape[1]
    tn = 1024
    return pl.pallas_call(
        _ada_kernel,
        grid=(n // tn,),
        in_specs=[
            pl.BlockSpec((rows, D_MODEL), lambda j: (0, 0)),
            pl.BlockSpec((D_MODEL, tn), lambda j: (0, j)),
            pl.BlockSpec((1, tn), lambda j: (0, j)),
        ],
        out_specs=pl.BlockSpec((rows, tn), lambda j: (0, j)),
        out_shape=jax.ShapeDtypeStruct((rows, n), F32),
        compiler_params=_params("arbitrary"),
        name="ada_mod",
    )(c_all, w_ada, b_ada.reshape(1, n))


def _in_proj_kernel(x_ref, sc_ref, sh_ref, n1_ref, wa_ref, wb_ref, ws_ref, o_ref, os_ref, h_scr,
                    *, n_a, n_sub):
    bt, lt, d = x_ref.shape
    j = pl.program_id(2)
    if bt == 1:
        ls = lt // n_sub
        subs = [(slice(0, 1), slice(i * ls, (i + 1) * ls)) for i in range(n_sub)]
    else:
        bs = bt // n_sub
        subs = [(slice(i * bs, (i + 1) * bs), slice(0, lt)) for i in range(n_sub)]

    @pl.when(j == 0)
    def _():
        r0 = 0
        for bsl, lsl in subs:
            x = x_ref[bsl, lsl, :]
            nb, nl, _ = x.shape
            y = x * lax.rsqrt(jnp.mean(x * x, axis=-1, keepdims=True) + EPS) * n1_ref[...]
            h = y * (1.0 + sc_ref[bsl]) + sh_ref[bsl]
            hb = h.reshape(nb * nl, d).astype(BF16)
            h_scr[r0:r0 + nb * nl, :] = hb
            r0 += nb * nl
            os_ref[bsl, lsl, :] = _dot(hb, ws_ref[...]).reshape(nb, nl, SMALL_COLS)
            o_ref[bsl, lsl, :] = _dot(hb, wa_ref[...]).reshape(nb, nl, o_ref.shape[-1])

    @pl.when((j > 0) & (j < n_a))
    def _():
        o_ref[...] = _dot(h_scr[...], wa_ref[...]).reshape(o_ref.shape)

    @pl.when(j >= n_a)
    def _():
        o_ref[...] = _dot(h_scr[...], wb_ref[...]).reshape(o_ref.shape)


def _in_proj(x, mod3, mod_row0, norm1, w_gla16, w_gdn16, w_small, bt, lt):
    b, l, d = x.shape
    tn = 1024
    n_a = GLA_COLS // tn
    mrow = mod_row0 // bt
    grid = (b // bt, l // lt, MAIN_COLS // tn)
    return pl.pallas_call(
        functools.partial(_in_proj_kernel, n_a=n_a, n_sub=4),
        grid=grid,
        in_specs=[
            pl.BlockSpec((bt, lt, d), lambda i, t, j: (i, t, 0)),
            pl.BlockSpec((bt, 1, d), lambda i, t, j: (mrow + i, 0, 1)),
            pl.BlockSpec((bt, 1, d), lambda i, t, j: (mrow + i, 0, 0)),
            pl.BlockSpec((1, 1, d), lambda i, t, j: (0, 0, 0)),
            pl.BlockSpec((d, tn), lambda i, t, j: (0, jnp.minimum(j, n_a - 1))),
            pl.BlockSpec((d, tn), lambda i, t, j: (0, jnp.maximum(j - n_a, 0))),
            pl.BlockSpec((d, SMALL_COLS), lambda i, t, j: (0, 0)),
        ],
        out_specs=[
            pl.BlockSpec((bt, lt, tn), lambda i, t, j: (i, t, j)),
            pl.BlockSpec((bt, lt, SMALL_COLS), lambda i, t, j: (i, t, 0)),
        ],
        out_shape=[
            jax.ShapeDtypeStruct((b, l, MAIN_COLS), F32),
            jax.ShapeDtypeStruct((b, l, SMALL_COLS), F32),
        ],
        scratch_shapes=[pltpu.VMEM((bt * lt, d), BF16)],
        compiler_params=_params("arbitrary", "arbitrary", "arbitrary"),
        name="in_proj",
    )(x, mod3, mod3, norm1.reshape(1, 1, d), w_gla16, w_gdn16, w_small)


def _gla_levels():
    row = lax.broadcasted_iota(jnp.int32, (CHUNK, CHUNK), 0)
    col = lax.broadcasted_iota(jnp.int32, (CHUNK, CHUNK), 1)
    ops, masks = [], []
    half = CHUNK // 2
    while half >= 1:
        shift = half.bit_length()
        parent_row = jnp.left_shift(jnp.right_shift(row, shift), shift)
        parent_col = jnp.left_shift(jnp.right_shift(col, shift), shift)
        m_row = parent_row + (half - 1)
        right_row = row > m_row
        in_span = (right_row & (col > m_row) & (col <= row)) | (
            jnp.logical_not(right_row) & (col > row) & (col <= m_row))
        ops.append(jnp.where(in_span, 1.0, 0.0))
        masks.append((parent_row == parent_col) & right_row & (col <= m_row))
        half //= 2
    return ops, masks, row == col


def _gla_kernel(q_ref, k_ref, v_ref, r_ref, sm_ref, wg_ref, bg_ref, gn_ref, s0_ref,
                o_ref, s_ref, st_scr, attn_scr):
    t = pl.program_id(1)
    lt = q_ref.shape[1]
    n_chunks = lt // CHUNK
    causal, _ = _tri_masks()
    tri = causal.astype(F32)
    scale = GLA_DK ** -0.5

    @pl.when(t == 0)
    def _():
        for h in range(GLA_HEADS):
            st_scr[h] = s0_ref[0, h].T

    wg = wg_ref[...].astype(BF16)
    heads = range(GLA_HEADS)
    kcs = [slice(h * GLA_DK, (h + 1) * GLA_DK) for h in heads]
    vcs = [slice(h * GLA_DV, (h + 1) * GLA_DV) for h in heads]
    chunk_rows = [pl.ds(c * CHUNK, CHUNK) for c in range(n_chunks)]

    def scaled_q(rows):
        return [q_ref[0, rows, kc] * scale for kc in kcs]

    log_a, big_g = [], []
    for rows in chunk_rows:
        a_lr = sm_ref[0, rows, LR_OFF:LR_OFF + GLA_GATE_RANK].astype(BF16)
        la = _log_sigmoid(_dot(a_lr, wg) + bg_ref[...]) / GLA_GATE_NORM
        log_a.append(la)
        big_g.append(_dot_f32(tri, la))
    total = big_g[0][CHUNK - 1:CHUNK, :]
    for g in big_g[1:]:
        total = jnp.minimum(total, g[CHUNK - 1:CHUNK, :])
    mild = jnp.min(total) > -GLA_MILD_LOG_DECAY

    @pl.when(mild)
    def _():
        for c, rows in enumerate(chunk_rows):
            g_mid = big_g[c][CHUNK // 2 - 1:CHUNK // 2, :]
            e_q = jnp.exp(big_g[c] - g_mid)
            e_k = jnp.exp(g_mid - big_g[c])
            q = scaled_q(rows)
            scores = [_dot_nt((qq * e_q[:, kc]).astype(BF16), (k_ref[0, rows, kc] * e_k[:, kc]).astype(BF16))
                      for qq, kc in zip(q, kcs)]
            for h in heads:
                attn_scr[c, h] = jnp.where(causal, scores[h], 0.0)

    @pl.when(jnp.logical_not(mild))
    def _():
        level_ops, level_masks, diag = _gla_levels()
        level_op = jnp.concatenate(level_ops, axis=0)
        for c, rows in enumerate(chunk_rows):
            sums = _dot_f32(level_op, log_a[c])
            q = scaled_q(rows)
            k = [k_ref[0, rows, kc] for kc in kcs]
            scores = [jnp.where(diag, _dot_nt(qq.astype(BF16), kk.astype(BF16)), 0.0)
                      for qq, kk in zip(q, k)]
            for i, mask in enumerate(level_masks):
                e = jnp.exp(sums[i * CHUNK:(i + 1) * CHUNK])
                part = [_dot_nt((qq * e[:, kc]).astype(BF16), (kk * e[:, kc]).astype(BF16))
                        for qq, kk, kc in zip(q, k, kcs)]
                scores = [jnp.where(mask, p, a) for p, a in zip(part, scores)]
            for h in heads:
                attn_scr[c, h] = scores[h]

    pending = []
    for c, rows in enumerate(chunk_rows):
        g_last = big_g[c][CHUNK - 1:CHUNK, :]
        e_g = jnp.exp(big_g[c])
        e_kl = jnp.exp(g_last - big_g[c])
        e_l = jnp.exp(g_last)
        q = scaled_q(rows)
        v = [v_ref[0, rows, vc].astype(BF16) for vc in vcs]
        v_k = [_dot_tn(vv, (k_ref[0, rows, kc] * e_kl[:, kc]).astype(BF16)) for vv, kc in zip(v, kcs)]
        a_v = [_dot(attn_scr[c, h].astype(BF16), v[h]) for h in heads]
        q_g = [(qq * e_g[:, kc]).astype(BF16) for qq, kc in zip(q, kcs)]
        pending.append((rows, q_g, a_v, v_k, [e_l[:, kc] for kc in kcs]))
    for rows, q_g, a_v, v_k, e_l in pending:
        st = [st_scr[h] for h in heads]
        q_s = [_dot_nt(qq, ss.astype(BF16)) for qq, ss in zip(q_g, st)]
        for h in heads:
            st_scr[h] = st[h] * e_l[h] + v_k[h]
            o = a_v[h] + q_s[h]
            o = o * lax.rsqrt(jnp.mean(o * o, axis=-1, keepdims=True) + EPS) * gn_ref[...]
            o = o * _silu(r_ref[0, rows, vcs[h]])
            o_ref[0, rows, vcs[h]] = o.astype(o_ref.dtype)

    @pl.when(t == pl.num_programs(1) - 1)
    def _():
        for h in range(GLA_HEADS):
            s_ref[0, h] = st_scr[h].T


def _gla_mix(proj, small, s0, gla_wg, gla_bg, gla_norm, lt):
    b, l, _ = proj.shape
    kw = GLA_HEADS * GLA_DK
    return pl.pallas_call(
        _gla_kernel,
        grid=(b, l // lt),
        in_specs=[
            pl.BlockSpec((1, lt, kw), lambda i, t: (i, t, 0)),
            pl.BlockSpec((1, lt, kw), lambda i, t: (i, t, 1)),
            pl.BlockSpec((1, lt, GLA_WIDTH), lambda i, t: (i, t, 1)),
            pl.BlockSpec((1, lt, GLA_WIDTH), lambda i, t: (i, t, 2)),
            pl.BlockSpec((1, lt, SMALL_COLS), lambda i, t: (i, t, 0)),
            pl.BlockSpec((GLA_GATE_RANK, kw), lambda i, t: (0, 0)),
            pl.BlockSpec((1, kw), lambda i, t: (0, 0)),
            pl.BlockSpec((1, GLA_DV), lambda i, t: (0, 0)),
            pl.BlockSpec((1, GLA_HEADS, GLA_DK, GLA_DV), lambda i, t: (i, 0, 0, 0)),
        ],
        out_specs=[
            pl.BlockSpec((1, lt, GLA_WIDTH), lambda i, t: (i, t, 0)),
            pl.BlockSpec((1, GLA_HEADS, GLA_DK, GLA_DV), lambda i, t: (i, 0, 0, 0)),
        ],
        out_shape=[
            jax.ShapeDtypeStruct((b, l, GLA_WIDTH), BF16),
            jax.ShapeDtypeStruct((b, GLA_HEADS, GLA_DK, GLA_DV), F32),
        ],
        scratch_shapes=[pltpu.VMEM((GLA_HEADS, GLA_DV, GLA_DK), F32),
                        pltpu.VMEM((lt // CHUNK, GLA_HEADS, CHUNK, CHUNK), F32)],
        compiler_params=_params("arbitrary", "arbitrary"),
        name="gla_mix",
    )(proj, proj, proj, proj, small, gla_wg, gla_bg.reshape(1, kw), gla_norm.reshape(1, GLA_DV), s0)


CONV_PAD = SUBLANES
INV_BASE_LOG2 = 3


def _inverse_masks():
    row = lax.broadcasted_iota(jnp.int32, (CHUNK, CHUNK), 0)
    col = lax.broadcasted_iota(jnp.int32, (CHUNK, CHUNK), 1)

    def same_block(log2):
        return jnp.right_shift(row, log2) == jnp.right_shift(col, log2)

    base = same_block(INV_BASE_LOG2)
    merges = []
    log2 = INV_BASE_LOG2
    while (1 << log2) < CHUNK:
        merges.append(same_block(log2 + 1) & jnp.logical_not(same_block(log2)))
        log2 += 1
    return base, merges


def _unit_lower_inverse_minus_eye(lower, base, merges):
    neg = [jnp.where(base, -l, 0.0) for l in lower]
    x = neg
    p = neg
    for _ in range(INV_BASE_LOG2 - 1):
        p = _dot_many(p, p)
        xp = _dot_many(x, p)
        x = [a + b + c for a, b, c in zip(x, p, xp)]
    for m in merges:
        c = [jnp.where(m, l, 0.0) for l in lower]
        w = [a + b for a, b in zip(c, _dot_many(x, c))]
        wx = _dot_many(w, x)
        x = [a - (b + d) for a, b, d in zip(x, w, wx)]
    return x


def _gdn_kernel(x_ref, g_ref, sm_ref, cw_ref, ga_ref, gn_ref, c0_ref, s0_ref,
                o_ref, s_ref, c_ref, xbuf):
    t = pl.program_id(1)
    lt = x_ref.shape[1]
    causal, strict = _tri_masks()
    tri = causal.astype(F32)
    inv_base, inv_merges = _inverse_masks()
    scale = GDN_DK ** -0.5

    @pl.when(t == 0)
    def _():
        xbuf[0:CONV_PAD, :] = jnp.zeros((CONV_PAD, GDN_CONV_DIM), F32)
        xbuf[CONV_PAD - (GDN_CONV - 1):CONV_PAD, :] = c0_ref[0]
        s_ref[...] = s0_ref[...]

    @pl.when(t > 0)
    def _():
        xbuf[0:CONV_PAD, :] = xbuf[lt:lt + CONV_PAD, :]

    xbuf[CONV_PAD:CONV_PAD + lt, :] = x_ref[0]
    c_ref[0] = xbuf[lt + CONV_PAD - (GDN_CONV - 1):lt + CONV_PAD, :]

    def conv_silu(r0, cols):
        acc = cw_ref[GDN_CONV - 1:GDN_CONV, cols] * xbuf[r0 + CONV_PAD:r0 + CONV_PAD + CHUNK, cols]
        for tap in range(GDN_CONV - 1):
            back = GDN_CONV - 1 - tap
            acc = acc + cw_ref[tap:tap + 1, cols] * xbuf[r0 + CONV_PAD - back:r0 + CONV_PAD - back + CHUNK, cols]
        return _silu(acc)

    def l2norm(x):
        return x * lax.rsqrt(jnp.sum(x * x, axis=-1, keepdims=True) + EPS)

    heads = range(GDN_HEADS)
    pending = []
    for c in range(lt // CHUNK):
        r0 = c * CHUNK
        rows = pl.ds(r0, CHUNK)
        sm = sm_ref[0, rows, :]
        gate = -jnp.exp(ga_ref[0:1, :]) * _softplus(sm + ga_ref[1:2, :])
        beta = jax.nn.sigmoid(sm)
        big_g = _dot_f32(tri, gate)
        big_gt = big_g.T
        g_col = [big_g[:, ALPHA_OFF + h:ALPHA_OFF + h + 1] for h in heads]
        g_row = [big_gt[ALPHA_OFF + h:ALPHA_OFF + h + 1, :] for h in heads]
        b_col = [beta[:, BETA_OFF + h:BETA_OFF + h + 1] for h in heads]
        g_last = [g[CHUNK - 1:CHUNK, :] for g in g_col]
        decay = [jnp.where(causal, jnp.exp(gc - gr), 0.0) for gc, gr in zip(g_col, g_row)]
        e_g = [jnp.exp(g) for g in g_col]
        q = [l2norm(conv_silu(r0, slice(h * GDN_DK, (h + 1) * GDN_DK))) * scale for h in heads]
        k = [l2norm(conv_silu(r0, slice(GDN_WIDTH + h * GDN_DK, GDN_WIDTH + (h + 1) * GDN_DK)))
             for h in heads]
        v = [conv_silu(r0, slice(2 * GDN_WIDTH + h * GDN_DV, 2 * GDN_WIDTH + (h + 1) * GDN_DV))
             for h in heads]
        kb = [kk * b for kk, b in zip(k, b_col)]
        k16 = [kk.astype(BF16) for kk in k]
        kk_t = [_dot_nt(a.astype(BF16), b) for a, b in zip(kb, k16)]
        qk_t = [_dot_nt(a.astype(BF16), b) for a, b in zip(q, k16)]
        lower = [jnp.where(strict, m * d, 0.0) for m, d in zip(kk_t, decay)]
        attn = [jnp.where(causal, m * d, 0.0).astype(BF16) for m, d in zip(qk_t, decay)]
        xinv = _unit_lower_inverse_minus_eye(lower, inv_base, inv_merges)
        rhs = [jnp.concatenate([vv * b, kbb * e], axis=-1)
               for vv, b, kbb, e in zip(v, b_col, kb, e_g)]
        sol = [r + xr for r, xr in zip(rhs, _dot_many(xinv, rhs))]
        pending.append((
            rows, attn,
            [so[:, :GDN_DV] for so in sol],
            [so[:, GDN_DV:].astype(BF16) for so in sol],
            [(qq * e).astype(BF16) for qq, e in zip(q, e_g)],
            [(kk * jnp.exp(gl - gc)).astype(BF16) for kk, gl, gc in zip(k, g_last, g_col)],
            [jnp.exp(gl) for gl in g_last]))
    for rows, attn, sol_v, sol_k, q_g, k_dec, e_last in pending:
        s = [s_ref[0, h] for h in heads]
        s16 = [ss.astype(BF16) for ss in s]
        k_s = [_dot(a, ss) for a, ss in zip(sol_k, s16)]
        q_s = [_dot(a, ss) for a, ss in zip(q_g, s16)]
        u16 = [(sv - ks).astype(BF16) for sv, ks in zip(sol_v, k_s)]
        a_u = [_dot(a, uu) for a, uu in zip(attn, u16)]
        k_u = [_dot_tn(kd, uu) for kd, uu in zip(k_dec, u16)]
        for h in heads:
            hc = slice(h * GDN_DV, (h + 1) * GDN_DV)
            s_ref[0, h] = e_last[h] * s[h] + k_u[h]
            o = q_s[h] + a_u[h]
            o = o * lax.rsqrt(jnp.mean(o * o, axis=-1, keepdims=True) + EPS) * gn_ref[...]
            o = o * _silu(g_ref[0, rows, hc])
            o_ref[0, rows, hc] = o.astype(o_ref.dtype)


def _gdn_mix(proj, small, c0, s0, conv_w, gate_vec, gdn_norm, lt):
    b, l, _ = proj.shape
    return pl.pallas_call(
        _gdn_kernel,
        grid=(b, l // lt),
        in_specs=[
            pl.BlockSpec((1, lt, GDN_CONV_DIM), lambda i, t: (i, t, 1)),
            pl.BlockSpec((1, lt, GDN_WIDTH), lambda i, t: (i, t, 6)),
            pl.BlockSpec((1, lt, SMALL_COLS), lambda i, t: (i, t, 0)),
            pl.BlockSpec((GDN_CONV, GDN_CONV_DIM), lambda i, t: (0, 0)),
            pl.BlockSpec((2, SMALL_COLS), lambda i, t: (0, 0)),
            pl.BlockSpec((1, GDN_DV), lambda i, t: (0, 0)),
            pl.BlockSpec((1, GDN_CONV - 1, GDN_CONV_DIM), lambda i, t: (i, 0, 0)),
            pl.BlockSpec((1, GDN_HEADS, GDN_DK, GDN_DV), lambda i, t: (i, 0, 0, 0)),
        ],
        out_specs=[
            pl.BlockSpec((1, lt, GDN_WIDTH), lambda i, t: (i, t, 0)),
            pl.BlockSpec((1, GDN_HEADS, GDN_DK, GDN_DV), lambda i, t: (i, 0, 0, 0)),
            pl.BlockSpec((1, GDN_CONV - 1, GDN_CONV_DIM), lambda i, t: (i, 0, 0)),
        ],
        out_shape=[
            jax.ShapeDtypeStruct((b, l, GDN_WIDTH), BF16),
            jax.ShapeDtypeStruct((b, GDN_HEADS, GDN_DK, GDN_DV), F32),
            jax.ShapeDtypeStruct((b, GDN_CONV - 1, GDN_CONV_DIM), F32),
        ],
        scratch_shapes=[pltpu.VMEM((lt + CONV_PAD, GDN_CONV_DIM), F32)],
        compiler_params=_params("arbitrary", "arbitrary"),
        name="gdn_mix",
    )(proj, proj, small, conv_w, gate_vec, gdn_norm.reshape(1, GDN_DV), c0, s0)


def _out_proj_kernel(x_ref, oa_ref, ob_ref, w_ref, g1_ref, sc_ref, sh_ref, n2_ref, x1_ref, h2_ref,
                     *, n_sub):
    bt, lt, d = x_ref.shape
    if bt == 1:
        ls = lt // n_sub
        subs = [(slice(0, 1), slice(i * ls, (i + 1) * ls)) for i in range(n_sub)]
    else:
        bs = bt // n_sub
        subs = [(slice(i * bs, (i + 1) * bs), slice(0, lt)) for i in range(n_sub)]
    mixes = []
    for bsl, lsl in subs:
        oa = oa_ref[bsl, lsl, :]
        nb, nl, _ = oa.shape
        oa = oa.reshape(nb * nl, GLA_WIDTH)
        ob = ob_ref[bsl, lsl, :].reshape(nb * nl, GDN_WIDTH)
        mix = _dot(oa, w_ref[0:GLA_WIDTH, :]) + _dot(ob, w_ref[GLA_WIDTH:GLA_WIDTH + GDN_WIDTH, :])
        mixes.append(mix.reshape(nb, nl, d))
    for (bsl, lsl), mix in zip(subs, mixes):
        x1 = x_ref[bsl, lsl, :] + g1_ref[bsl] * mix
        x1_ref[bsl, lsl, :] = x1
        y = x1 * lax.rsqrt(jnp.mean(x1 * x1, axis=-1, keepdims=True) + EPS) * n2_ref[...]
        h2_ref[bsl, lsl, :] = (y * (1.0 + sc_ref[bsl]) + sh_ref[bsl]).astype(h2_ref.dtype)


def _out_proj(x, o_a, o_b, mod3, mod_row0, w_out16, norm2, bt, lt):
    b, l, d = x.shape
    mrow = mod_row0 // bt
    return pl.pallas_call(
        functools.partial(_out_proj_kernel, n_sub=2),
        grid=(b // bt, l // lt),
        in_specs=[
            pl.BlockSpec((bt, lt, d), lambda i, t: (i, t, 0)),
            pl.BlockSpec((bt, lt, GLA_WIDTH), lambda i, t: (i, t, 0)),
            pl.BlockSpec((bt, lt, GDN_WIDTH), lambda i, t: (i, t, 0)),
            pl.BlockSpec((GLA_WIDTH + GDN_WIDTH, d), lambda i, t: (0, 0)),
            pl.BlockSpec((bt, 1, d), lambda i, t: (mrow + i, 0, 2)),
            pl.BlockSpec((bt, 1, d), lambda i, t: (mrow + i, 0, 4)),
            pl.BlockSpec((bt, 1, d), lambda i, t: (mrow + i, 0, 3)),
            pl.BlockSpec((1, 1, d), lambda i, t: (0, 0, 0)),
        ],
        out_specs=[
            pl.BlockSpec((bt, lt, d), lambda i, t: (i, t, 0)),
            pl.BlockSpec((bt, lt, d), lambda i, t: (i, t, 0)),
        ],
        out_shape=[
            jax.ShapeDtypeStruct((b, l, d), F32),
            jax.ShapeDtypeStruct((b, l, d), BF16),
        ],
        compiler_params=_params("arbitrary", "arbitrary"),
        name="out_proj",
    )(x, o_a, o_b, w_out16, mod3, mod3, mod3, norm2.reshape(1, 1, d))


FFN_PAD = SUBLANES


def _ffn_kernel(h_ref, halo_ref, wg_ref, wv_ref, wd_ref, cw_ref, cb_ref, st_ref, x1_ref, g2_ref,
                fn_ref, y_ref, so_ref, acc, gbuf, *, has_halo, n_sub):
    t = pl.program_id(1)
    f = pl.program_id(2)
    bt, lt, d = h_ref.shape
    tf = wg_ref.shape[1]
    if bt == 1:
        ls = lt // n_sub
        subs = [(slice(0, 1), slice(i * ls, (i + 1) * ls)) for i in range(n_sub)]
    else:
        bs = bt // n_sub
        subs = [(slice(i * bs, (i + 1) * bs), slice(0, lt)) for i in range(n_sub)]

    @pl.when(f == 0)
    def _():
        acc[...] = jnp.zeros(acc.shape, F32)

    prev = st_ref[...]
    if has_halo:
        halo_gate = _dot(halo_ref[0], wg_ref[...])
        prev = jnp.where(t == 0, prev, halo_gate[SUBLANES - (FFN_CONV - 1):SUBLANES, :][None])
    gbuf[:, FFN_PAD - (FFN_CONV - 1):FFN_PAD, :] = prev
    gates, vals = [], []
    for bsl, lsl in subs:
        hb = h_ref[bsl, lsl, :]
        nb, nl, _ = hb.shape
        hb = hb.reshape(nb * nl, d)
        gate = _dot(hb, wg_ref[...]).reshape(nb, nl, tf)
        gbuf[bsl, FFN_PAD + lsl.start:FFN_PAD + lsl.stop, :] = gate
        gates.append(gate)
        vals.append(_dot(hb, wv_ref[...]))
    so_ref[...] = gbuf[:, FFN_PAD + lt - (FFN_CONV - 1):FFN_PAD + lt, :].reshape(so_ref.shape)
    acts = []
    for (bsl, lsl), gate, val in zip(subs, gates, vals):
        nb, nl, _ = gate.shape
        conv = cw_ref[FFN_CONV - 1:FFN_CONV, :] * gate
        for tap in range(FFN_CONV - 1):
            back = FFN_CONV - 1 - tap
            conv = conv + cw_ref[tap:tap + 1, :] * gbuf[bsl, FFN_PAD + lsl.start - back:FFN_PAD + lsl.stop - back, :]
        acts.append((_silu(conv + cb_ref[...]).reshape(nb * nl, tf) * val).astype(BF16))
    parts = [_dot(act, wd_ref[...]) for act in acts]
    r0 = 0
    for part in parts:
        acc[r0:r0 + part.shape[0], :] += part
        r0 += part.shape[0]

    @pl.when(f == pl.num_programs(2) - 1)
    def _():
        x2 = x1_ref[...] + g2_ref[...] * acc[...].reshape(bt, lt, d)
        y = x2 * lax.rsqrt(jnp.mean(x2 * x2, axis=-1, keepdims=True) + EPS) * fn_ref[...]
        y_ref[...] = y


def _ffn(h2, x1, st0, mod3, mod_row0, w_up16, w_down16, conv_w, conv_b, final_norm, bt, lt):
    b, l, d = h2.shape
    tf = 512
    nf = D_FF // tf
    mrow = mod_row0 // bt
    has_halo = lt < l
    halo_blocks = lt // SUBLANES
    kern = functools.partial(_ffn_kernel, has_halo=has_halo, n_sub=2)
    return pl.pallas_call(
        kern,
        grid=(b // bt, l // lt, nf),
        in_specs=[
            pl.BlockSpec((bt, lt, d), lambda i, t, f: (i, t, 0)),
            pl.BlockSpec((1, SUBLANES, d), lambda i, t, f: (i, jnp.maximum(t * halo_blocks - 1, 0), 0)),
            pl.BlockSpec((d, tf), lambda i, t, f: (0, f)),
            pl.BlockSpec((d, tf), lambda i, t, f: (0, nf + f)),
            pl.BlockSpec((tf, d), lambda i, t, f: (f, 0)),
            pl.BlockSpec((FFN_CONV, tf), lambda i, t, f: (0, f)),
            pl.BlockSpec((1, tf), lambda i, t, f: (0, f)),
            pl.BlockSpec((bt, FFN_CONV - 1, tf), lambda i, t, f: (i, 0, f)),
            pl.BlockSpec((bt, lt, d), lambda i, t, f: (i, t, 0)),
            pl.BlockSpec((bt, 1, d), lambda i, t, f: (mrow + i, 0, 5)),
            pl.BlockSpec((1, 1, d), lambda i, t, f: (0, 0, 0)),
        ],
        out_specs=[
            pl.BlockSpec((bt, lt, d), lambda i, t, f: (i, t, 0)),
            pl.BlockSpec((bt, 1, FFN_CONV - 1, tf), lambda i, t, f: (i, t, 0, f)),
        ],
        out_shape=[
            jax.ShapeDtypeStruct((b, l, d), F32),
            jax.ShapeDtypeStruct((b, l // lt, FFN_CONV - 1, D_FF), F32),
        ],
        scratch_shapes=[
            pltpu.VMEM((bt * lt, d), F32),
            pltpu.VMEM((bt, lt + FFN_PAD, tf), F32),
        ],
        compiler_params=_params("arbitrary", "arbitrary", "arbitrary"),
        name="ffn",
    )(h2, h2, w_up16, w_up16, w_down16, conv_w, conv_b.reshape(1, D_FF), st0, x1, mod3,
      final_norm.reshape(1, 1, d))


def _trunk(x, mod3, mod_row0, states, weights, cfg):
    (w_gla16, w_gdn16, w_small, norm1, gla_wg, gla_bg, gla_norm, gdn_conv_w, gate_vec, gdn_norm, w_out16,
     norm2, w_up16, w_down16, ffn_conv_w, ffn_conv_b, final_norm) = weights
    s_gla, s_gdn, s_conv, s_ffn = states
    proj, small = _in_proj(x, mod3, mod_row0, norm1, w_gla16, w_gdn16, w_small, cfg["in_bt"], cfg["in_lt"])
    o_a, n_gla = _gla_mix(proj, small, s_gla, gla_wg, gla_bg, gla_norm, cfg["mix_lt"])
    o_b, n_gdn, n_conv = _gdn_mix(proj, small, s_conv, s_gdn, gdn_conv_w, gate_vec, gdn_norm,
                                  cfg["mix_lt"])
    x1, h2 = _out_proj(x, o_a, o_b, mod3, mod_row0, w_out16, norm2, cfg["ffn_bt"], cfg["ffn_lt"])
    y, n_ffn = _ffn(h2, x1, s_ffn, mod3, mod_row0, w_up16, w_down16, ffn_conv_w, ffn_conv_b,
                    final_norm, cfg["ffn_bt"], cfg["ffn_lt"])
    return y, n_gla[None], n_gdn[None], n_conv[None], n_ffn[:, -1][None]


def kernel(x_prompt, x_sample, c_prompt, c_sample, state_gla, state_gdn, state_gdn_conv, state_ffn_conv, w_ada, b_ada, norm1, w_in, gla_wg, gla_bg, gla_norm, gdn_conv_w, gdn_a_log, gdn_dt_bias, gdn_norm, w_out, norm2, w_up, ffn_conv_w, ffn_conv_b, w_down, final_norm):
    bp = x_prompt.shape[0]
    bs = x_sample.shape[0]

    w_in0 = w_in[0]
    w_gla16 = w_in0[:, :GLA_COLS].astype(BF16)
    w_gdn16 = w_in0[:, IN_OFF_GDN:IN_OFF_BETA].astype(BF16)
    w_small = jnp.concatenate(
        [w_in0[:, IN_OFF_LR:IN_OFF_GDN], w_in0[:, IN_OFF_BETA:],
         jnp.zeros((D_MODEL, SMALL_COLS - GLA_GATE_RANK - 2 * GDN_HEADS), F32)], axis=1).astype(BF16)
    w_out16 = w_out[0].astype(BF16)
    w_up16 = w_up[0].astype(BF16)
    w_down16 = w_down[0].astype(BF16)
    gate_vec = jnp.zeros((2, SMALL_COLS), F32)
    gate_vec = gate_vec.at[0, ALPHA_OFF:ALPHA_OFF + GDN_HEADS].set(gdn_a_log[0])
    gate_vec = gate_vec.at[1, ALPHA_OFF:ALPHA_OFF + GDN_HEADS].set(gdn_dt_bias[0])

    c_all = jnp.concatenate([c_sample, c_prompt], axis=0)
    mod = _ada_mod(c_all, w_ada[0], b_ada[0])
    mod3 = mod.reshape(bs + bp, 1, N_MOD * D_MODEL)

    weights = (w_gla16, w_gdn16, w_small, norm1[0], gla_wg[0], gla_bg[0], gla_norm[0], gdn_conv_w[0], gate_vec,
               gdn_norm[0], w_out16, norm2[0], w_up16, w_down16, ffn_conv_w[0], ffn_conv_b[0],
               final_norm)

    fresh = (jnp.zeros((bp, GLA_HEADS, GLA_DK, GLA_DV), F32),
             jnp.zeros((bp, GDN_HEADS, GDN_DK, GDN_DV), F32),
             jnp.zeros((bp, GDN_CONV - 1, GDN_CONV_DIM), F32),
             jnp.zeros((bp, FFN_CONV - 1, D_FF), F32))
    cfg_p = dict(in_bt=1, in_lt=1024, mix_lt=256, ffn_bt=1, ffn_lt=512)
    y_p, p_gla, p_gdn, p_conv, p_ffn = _trunk(x_prompt, mod3, bs, fresh, weights, cfg_p)

    carried = (state_gla[0], state_gdn[0], state_gdn_conv[0], state_ffn_conv[0])
    cfg_s = dict(in_bt=bs, in_lt=x_sample.shape[1], mix_lt=x_sample.shape[1], ffn_bt=8,
                 ffn_lt=x_sample.shape[1])
    y_s, s_gla, s_gdn, s_conv, s_ffn = _trunk(x_sample, mod3, 0, carried, weights, cfg_s)
    return (y_p, y_s, p_gla, p_gdn, p_conv, p_ffn, s_gla, s_gdn, s_conv, s_ffn)
```

```python
import functools

import jax
import jax.numpy as jnp
from jax import lax
from jax.experimental import pallas as pl
from jax.experimental.pallas import tpu as pltpu

F32 = jnp.float32
BF16 = jnp.bfloat16

D_MODEL = 2048
CHUNK = 64
GLA_HEADS = 4
GLA_DK = 128
GLA_DV = 256
GLA_WIDTH = GLA_HEADS * GLA_DV
GLA_GATE_RANK = 16
GLA_GATE_NORM = 16.0
GLA_MILD_LOG_DECAY = 60.0
GDN_HEADS = 8
GDN_DK = 128
GDN_DV = 128
GDN_WIDTH = GDN_HEADS * GDN_DV
GDN_CONV = 4
GDN_CONV_DIM = 3 * GDN_WIDTH
D_FF = 5632
FFN_CONV = 3
N_MOD = 6
EPS = 1e-6

GLA_COLS = 2 * GLA_HEADS * GLA_DK + 2 * GLA_WIDTH
GDN_COLS = 4 * GDN_WIDTH
MAIN_COLS = GLA_COLS + GDN_COLS
SMALL_COLS = 128
LR_OFF, BETA_OFF, ALPHA_OFF = 0, GLA_GATE_RANK, GLA_GATE_RANK + GDN_HEADS
IN_OFF_LR = GLA_COLS
IN_OFF_GDN = IN_OFF_LR + GLA_GATE_RANK
IN_OFF_BETA = IN_OFF_GDN + GDN_COLS

SUBLANES = 8
VMEM_LIMIT_BYTES = 56 * 1024 * 1024
FFN_VMEM_LIMIT_BYTES = 60 * 1024 * 1024

NT_DIMS = (((1,), (1,)), ((), ()))
TN_DIMS = (((0,), (0,)), ((), ()))


def _dot(a, b):
    return jnp.dot(a, b, preferred_element_type=F32)


def _dot_nt(a, b):
    return lax.dot_general(a, b, NT_DIMS, preferred_element_type=F32)


def _dot_tn(a, b):
    return lax.dot_general(a, b, TN_DIMS, preferred_element_type=F32)


def _dot_f32(a, b):
    return jnp.dot(a, b, preferred_element_type=F32, precision=lax.Precision.HIGHEST)


def _dot_many(a_list, b_list):
    return [_dot(a.astype(BF16), b.astype(BF16)) for a, b in zip(a_list, b_list)]


def _silu(x):
    return x * jax.nn.sigmoid(x)


def _softplus(x):
    return jnp.maximum(x, 0.0) + jnp.log1p(jnp.exp(-jnp.abs(x)))


def _log_sigmoid(x):
    return jnp.minimum(x, 0.0) - jnp.log1p(jnp.exp(-jnp.abs(x)))


def _params(*semantics, vmem_limit_bytes=VMEM_LIMIT_BYTES):
    return pltpu.CompilerParams(dimension_semantics=semantics, vmem_limit_bytes=vmem_limit_bytes)


def _tri_masks():
    row = lax.broadcasted_iota(jnp.int32, (CHUNK, CHUNK), 0)
    col = lax.broadcasted_iota(jnp.int32, (CHUNK, CHUNK), 1)
    return row >= col, row > col


def _ada_kernel(c_ref, w_ref, b_ref, o_ref):
    s = _silu(c_ref[...]).astype(BF16)
    o_ref[...] = _dot(s, w_ref[...].astype(BF16)) + b_ref[...]


def _ada_mod(c_all, w_ada, b_ada):
    rows = c_all.shape[0]
    n = w_ada.shape[1]
    tn = 1024
    return pl.pallas_call(
        _ada_kernel,
        grid=(n // tn,),
        in_specs=[
            pl.BlockSpec((rows, D_MODEL), lambda j: (0, 0)),
            pl.BlockSpec((D_MODEL, tn), lambda j: (0, j)),
            pl.BlockSpec((1, tn), lambda j: (0, j)),
        ],
        out_specs=pl.BlockSpec((rows, tn), lambda j: (0, j)),
        out_shape=jax.ShapeDtypeStruct((rows, n), F32),
        compiler_params=_params("arbitrary"),
        name="ada_mod",
    )(c_all, w_ada, b_ada.reshape(1, n))


def _in_proj_kernel(x_ref, sc_ref, sh_ref, n1_ref, wa_ref, wb_ref, ws_ref, o_ref, os_ref, h_scr,
                    *, n_a):
    bt, lt, d = x_ref.shape
    j = pl.program_id(2)

    @pl.when(j == 0)
    def _():
        x = x_ref[...]
        y = x * lax.rsqrt(jnp.mean(x * x, axis=-1, keepdims=True) + EPS) * n1_ref[...]
        h = y * (1.0 + sc_ref[...]) + sh_ref[...]
        hb = h.reshape(bt * lt, d).astype(BF16)
        h_scr[...] = hb
        os_ref[...] = _dot(hb, ws_ref[...]).reshape(os_ref.shape)

    @pl.when(j < n_a)
    def _():
        o_ref[...] = _dot(h_scr[...], wa_ref[...]).reshape(o_ref.shape)

    @pl.when(j >= n_a)
    def _():
        o_ref[...] = _dot(h_scr[...], wb_ref[...]).reshape(o_ref.shape)


def _in_proj(x, mod3, mod_row0, norm1, w_gla16, w_gdn16, w_small, bt, lt):
    b, l, d = x.shape
    tn = 1024
    n_a = GLA_COLS // tn
    mrow = mod_row0 // bt
    grid = (b // bt, l // lt, MAIN_COLS // tn)
    return pl.pallas_call(
        functools.partial(_in_proj_kernel, n_a=n_a),
        grid=grid,
        in_specs=[
            pl.BlockSpec((bt, lt, d), lambda i, t, j: (i, t, 0)),
            pl.BlockSpec((bt, 1, d), lambda i, t, j: (mrow + i, 0, 1)),
            pl.BlockSpec((bt, 1, d), lambda i, t, j: (mrow + i, 0, 0)),
            pl.BlockSpec((1, 1, d), lambda i, t, j: (0, 0, 0)),
            pl.BlockSpec((d, tn), lambda i, t, j: (0, jnp.minimum(j, n_a - 1))),
            pl.BlockSpec((d, tn), lambda i, t, j: (0, jnp.maximum(j - n_a, 0))),
            pl.BlockSpec((d, SMALL_COLS), lambda i, t, j: (0, 0)),
        ],
        out_specs=[
            pl.BlockSpec((bt, lt, tn), lambda i, t, j: (i, t, j)),
            pl.BlockSpec((bt, lt, SMALL_COLS), lambda i, t, j: (i, t, 0)),
        ],
        out_shape=[
            jax.ShapeDtypeStruct((b, l, MAIN_COLS), F32),
            jax.ShapeDtypeStruct((b, l, SMALL_COLS), F32),
        ],
        scratch_shapes=[pltpu.VMEM((bt * lt, d), BF16)],
        compiler_params=_params("arbitrary", "arbitrary", "arbitrary"),
        name="in_proj",
    )(x, mod3, mod3, norm1.reshape(1, 1, d), w_gla16, w_gdn16, w_small)


def _gla_levels():
    row = lax.broadcasted_iota(jnp.int32, (CHUNK, CHUNK), 0)
    col = lax.broadcasted_iota(jnp.int32, (CHUNK, CHUNK), 1)
    ops, masks = [], []
    half = CHUNK // 2
    while half >= 1:
        shift = half.bit_length()
        parent_row = jnp.left_shift(jnp.right_shift(row, shift), shift)
        parent_col = jnp.left_shift(jnp.right_shift(col, shift), shift)
        m_row = parent_row + (half - 1)
        right_row = row > m_row
        in_span = (right_row & (col > m_row) & (col <= row)) | (
            jnp.logical_not(right_row) & (col > row) & (col <= m_row))
        ops.append(jnp.where(in_span, 1.0, 0.0))
        masks.append((parent_row == parent_col) & right_row & (col <= m_row))
        half //= 2
    return ops, masks, row == col


def _gla_kernel(q_ref, k_ref, v_ref, r_ref, sm_ref, wg_ref, bg_ref, gn_ref, s0_ref,
                o_ref, s_ref, st_scr, attn_scr):
    t = pl.program_id(1)
    lt = q_ref.shape[1]
    n_chunks = lt // CHUNK
    causal, _ = _tri_masks()
    tri = causal.astype(F32)
    scale = GLA_DK ** -0.5

    @pl.when(t == 0)
    def _():
        for h in range(GLA_HEADS):
            st_scr[h] = s0_ref[0, h].T

    wg = wg_ref[...].astype(BF16)
    heads = range(GLA_HEADS)
    kcs = [slice(h * GLA_DK, (h + 1) * GLA_DK) for h in heads]
    vcs = [slice(h * GLA_DV, (h + 1) * GLA_DV) for h in heads]
    chunk_rows = [pl.ds(c * CHUNK, CHUNK) for c in range(n_chunks)]

    def scaled_q(rows):
        return [q_ref[0, rows, kc] * scale for kc in kcs]

    log_a, big_g = [], []
    for rows in chunk_rows:
        a_lr = sm_ref[0, rows, LR_OFF:LR_OFF + GLA_GATE_RANK].astype(BF16)
        la = _log_sigmoid(_dot(a_lr, wg) + bg_ref[...]) / GLA_GATE_NORM
        log_a.append(la)
        big_g.append(_dot_f32(tri, la))
    total = big_g[0][CHUNK - 1:CHUNK, :]
    for g in big_g[1:]:
        total = jnp.minimum(total, g[CHUNK - 1:CHUNK, :])
    mild = jnp.min(total) > -GLA_MILD_LOG_DECAY

    @pl.when(mild)
    def _():
        for c, rows in enumerate(chunk_rows):
            g_mid = big_g[c][CHUNK // 2 - 1:CHUNK // 2, :]
            e_q = jnp.exp(big_g[c] - g_mid)
            e_k = jnp.exp(g_mid - big_g[c])
            q = scaled_q(rows)
            scores = [_dot_nt((qq * e_q[:, kc]).astype(BF16), (k_ref[0, rows, kc] * e_k[:, kc]).astype(BF16))
                      for qq, kc in zip(q, kcs)]
            for h in heads:
                attn_scr[c, h] = jnp.where(causal, scores[h], 0.0)

    @pl.when(jnp.logical_not(mild))
    def _():
        level_ops, level_masks, diag = _gla_levels()
        level_op = jnp.concatenate(level_ops, axis=0)
        for c, rows in enumerate(chunk_rows):
            sums = _dot_f32(level_op, log_a[c])
            q = scaled_q(rows)
            k = [k_ref[0, rows, kc] for kc in kcs]
            scores = [jnp.where(diag, _dot_nt(qq.astype(BF16), kk.astype(BF16)), 0.0)
                      for qq, kk in zip(q, k)]
            for i, mask in enumerate(level_masks):
                e = jnp.exp(sums[i * CHUNK:(i + 1) * CHUNK])
                part = [_dot_nt((qq * e[:, kc]).astype(BF16), (kk * e[:, kc]).astype(BF16))
                        for qq, kk, kc in zip(q, k, kcs)]
                scores = [jnp.where(mask, p, a) for p, a in zip(part, scores)]
            for h in heads:
                attn_scr[c, h] = scores[h]

    pending = []
    for c, rows in enumerate(chunk_rows):
        g_last = big_g[c][CHUNK - 1:CHUNK, :]
        e_g = jnp.exp(big_g[c])
        e_kl = jnp.exp(g_last - big_g[c])
        e_l = jnp.exp(g_last)
        q = scaled_q(rows)
        v = [v_ref[0, rows, vc].astype(BF16) for vc in vcs]
        v_k = [_dot_tn(vv, (k_ref[0, rows, kc] * e_kl[:, kc]).astype(BF16)) for vv, kc in zip(v, kcs)]
        a_v = [_dot(attn_scr[c, h].astype(BF16), v[h]) for h in heads]
        q_g = [(qq * e_g[:, kc]).astype(BF16) for qq, kc in zip(q, kcs)]
        pending.append((rows, q_g, a_v, v_k, [e_l[:, kc] for kc in kcs]))
    for rows, q_g, a_v, v_k, e_l in pending:
        st = [st_scr[h] for h in heads]
        q_s = [_dot_nt(qq, ss.astype(BF16)) for qq, ss in zip(q_g, st)]
        for h in heads:
            st_scr[h] = st[h] * e_l[h] + v_k[h]
            o = a_v[h] + q_s[h]
            o = o * lax.rsqrt(jnp.mean(o * o, axis=-1, keepdims=True) + EPS) * gn_ref[...]
            o = o * _silu(r_ref[0, rows, vcs[h]])
            o_ref[0, rows, vcs[h]] = o.astype(o_ref.dtype)

    @pl.when(t == pl.num_programs(1) - 1)
    def _():
        for h in range(GLA_HEADS):
            s_ref[0, h] = st_scr[h].T


def _gla_mix(proj, small, s0, gla_wg, gla_bg, gla_norm, lt):
    b, l, _ = proj.shape
    kw = GLA_HEADS * GLA_DK
    return pl.pallas_call(
        _gla_kernel,
        grid=(b, l // lt),
        in_specs=[
            pl.BlockSpec((1, lt, kw), lambda i, t: (i, t, 0)),
            pl.BlockSpec((1, lt, kw), lambda i, t: (i, t, 1)),
            pl.BlockSpec((1, lt, GLA_WIDTH), lambda i, t: (i, t, 1)),
            pl.BlockSpec((1, lt, GLA_WIDTH), lambda i, t: (i, t, 2)),
            pl.BlockSpec((1, lt, SMALL_COLS), lambda i, t: (i, t, 0)),
            pl.BlockSpec((GLA_GATE_RANK, kw), lambda i, t: (0, 0)),
            pl.BlockSpec((1, kw), lambda i, t: (0, 0)),
            pl.BlockSpec((1, GLA_DV), lambda i, t: (0, 0)),
            pl.BlockSpec((1, GLA_HEADS, GLA_DK, GLA_DV), lambda i, t: (i, 0, 0, 0)),
        ],
        out_specs=[
            pl.BlockSpec((1, lt, GLA_WIDTH), lambda i, t: (i, t, 0)),
            pl.BlockSpec((1, GLA_HEADS, GLA_DK, GLA_DV), lambda i, t: (i, 0, 0, 0)),
        ],
        out_shape=[
            jax.ShapeDtypeStruct((b, l, GLA_WIDTH), BF16),
            jax.ShapeDtypeStruct((b, GLA_HEADS, GLA_DK, GLA_DV), F32),
        ],
        scratch_shapes=[pltpu.VMEM((GLA_HEADS, GLA_DV, GLA_DK), F32),
                        pltpu.VMEM((lt // CHUNK, GLA_HEADS, CHUNK, CHUNK), F32)],
        compiler_params=_params("arbitrary", "arbitrary"),
        name="gla_mix",
    )(proj, proj, proj, proj, small, gla_wg, gla_bg.reshape(1, kw), gla_norm.reshape(1, GLA_DV), s0)


CONV_PAD = SUBLANES
INV_BASE_LOG2 = 3


def _inverse_masks():
    row = lax.broadcasted_iota(jnp.int32, (CHUNK, CHUNK), 0)
    col = lax.broadcasted_iota(jnp.int32, (CHUNK, CHUNK), 1)

    def same_block(log2):
        return jnp.right_shift(row, log2) == jnp.right_shift(col, log2)

    base = same_block(INV_BASE_LOG2)
    merges = []
    log2 = INV_BASE_LOG2
    while (1 << log2) < CHUNK:
        merges.append(same_block(log2 + 1) & jnp.logical_not(same_block(log2)))
        log2 += 1
    return base, merges


def _unit_lower_inverse_minus_eye(lower, base, merges):
    neg = [jnp.where(base, -l, 0.0) for l in lower]
    x = neg
    p = neg
    for _ in range(INV_BASE_LOG2 - 1):
        p = _dot_many(p, p)
        xp = _dot_many(x, p)
        x = [a + b + c for a, b, c in zip(x, p, xp)]
    for m in merges:
        c = [jnp.where(m, l, 0.0) for l in lower]
        w = [a + b for a, b in zip(c, _dot_many(x, c))]
        wx = _dot_many(w, x)
        x = [a - (b + d) for a, b, d in zip(x, w, wx)]
    return x


def _gdn_kernel(x_ref, g_ref, sm_ref, cw_ref, ga_ref, gn_ref, c0_ref, s0_ref,
                o_ref, s_ref, c_ref, xbuf):
    t = pl.program_id(1)
    lt = x_ref.shape[1]
    causal, strict = _tri_masks()
    tri = causal.astype(F32)
    inv_base, inv_merges = _inverse_masks()
    scale = GDN_DK ** -0.5

    @pl.when(t == 0)
    def _():
        xbuf[0:CONV_PAD, :] = jnp.zeros((CONV_PAD, GDN_CONV_DIM), F32)
        xbuf[CONV_PAD - (GDN_CONV - 1):CONV_PAD, :] = c0_ref[0]
        s_ref[...] = s0_ref[...]

    @pl.when(t > 0)
    def _():
        xbuf[0:CONV_PAD, :] = xbuf[lt:lt + CONV_PAD, :]

    xbuf[CONV_PAD:CONV_PAD + lt, :] = x_ref[0]
    c_ref[0] = xbuf[lt + CONV_PAD - (GDN_CONV - 1):lt + CONV_PAD, :]

    def conv_silu(r0, cols):
        acc = cw_ref[GDN_CONV - 1:GDN_CONV, cols] * xbuf[r0 + CONV_PAD:r0 + CONV_PAD + CHUNK, cols]
        for tap in range(GDN_CONV - 1):
            back = GDN_CONV - 1 - tap
            acc = acc + cw_ref[tap:tap + 1, cols] * xbuf[r0 + CONV_PAD - back:r0 + CONV_PAD - back + CHUNK, cols]
        return _silu(acc)

    def l2norm(x):
        return x * lax.rsqrt(jnp.sum(x * x, axis=-1, keepdims=True) + EPS)

    heads = range(GDN_HEADS)
    pending = []
    for c in range(lt // CHUNK):
        r0 = c * CHUNK
        rows = pl.ds(r0, CHUNK)
        sm = sm_ref[0, rows, :]
        gate = -jnp.exp(ga_ref[0:1, :]) * _softplus(sm + ga_ref[1:2, :])
        beta = jax.nn.sigmoid(sm)
        big_g = _dot_f32(tri, gate)
        big_gt = big_g.T
        g_col = [big_g[:, ALPHA_OFF + h:ALPHA_OFF + h + 1] for h in heads]
        g_row = [big_gt[ALPHA_OFF + h:ALPHA_OFF + h + 1, :] for h in heads]
        b_col = [beta[:, BETA_OFF + h:BETA_OFF + h + 1] for h in heads]
        g_last = [g[CHUNK - 1:CHUNK, :] for g in g_col]
        decay = [jnp.where(causal, jnp.exp(gc - gr), 0.0) for gc, gr in zip(g_col, g_row)]
        e_g = [jnp.exp(g) for g in g_col]
        q = [l2norm(conv_silu(r0, slice(h * GDN_DK, (h + 1) * GDN_DK))) * scale for h in heads]
        k = [l2norm(conv_silu(r0, slice(GDN_WIDTH + h * GDN_DK, GDN_WIDTH + (h + 1) * GDN_DK)))
             for h in heads]
        v = [conv_silu(r0, slice(2 * GDN_WIDTH + h * GDN_DV, 2 * GDN_WIDTH + (h + 1) * GDN_DV))
             for h in heads]
        kb = [kk * b for kk, b in zip(k, b_col)]
        k16 = [kk.astype(BF16) for kk in k]
        kk_t = [_dot_nt(a.astype(BF16), b) for a, b in zip(kb, k16)]
        qk_t = [_dot_nt(a.astype(BF16), b) for a, b in zip(q, k16)]
        lower = [jnp.where(strict, m * d, 0.0) for m, d in zip(kk_t, decay)]
        attn = [jnp.where(causal, m * d, 0.0).astype(BF16) for m, d in zip(qk_t, decay)]
        xinv = _unit_lower_inverse_minus_eye(lower, inv_base, inv_merges)
        rhs = [jnp.concatenate([vv * b, kbb * e], axis=-1)
               for vv, b, kbb, e in zip(v, b_col, kb, e_g)]
        sol = [r + xr for r, xr in zip(rhs, _dot_many(xinv, rhs))]
        pending.append((
            rows, attn,
            [so[:, :GDN_DV] for so in sol],
            [so[:, GDN_DV:].astype(BF16) for so in sol],
            [(qq * e).astype(BF16) for qq, e in zip(q, e_g)],
            [(kk * jnp.exp(gl - gc)).astype(BF16) for kk, gl, gc in zip(k, g_last, g_col)],
            [jnp.exp(gl) for gl in g_last]))
    for rows, attn, sol_v, sol_k, q_g, k_dec, e_last in pending:
        s = [s_ref[0, h] for h in heads]
        s16 = [ss.astype(BF16) for ss in s]
        k_s = [_dot(a, ss) for a, ss in zip(sol_k, s16)]
        q_s = [_dot(a, ss) for a, ss in zip(q_g, s16)]
        u16 = [(sv - ks).astype(BF16) for sv, ks in zip(sol_v, k_s)]
        a_u = [_dot(a, uu) for a, uu in zip(attn, u16)]
        k_u = [_dot_tn(kd, uu) for kd, uu in zip(k_dec, u16)]
        for h in heads:
            hc = slice(h * GDN_DV, (h + 1) * GDN_DV)
            s_ref[0, h] = e_last[h] * s[h] + k_u[h]
            o = q_s[h] + a_u[h]
            o = o * lax.rsqrt(jnp.mean(o * o, axis=-1, keepdims=True) + EPS) * gn_ref[...]
            o = o * _silu(g_ref[0, rows, hc])
            o_ref[0, rows, hc] = o.astype(o_ref.dtype)


def _gdn_mix(proj, small, c0, s0, conv_w, gate_vec, gdn_norm, lt):
    b, l, _ = proj.shape
    return pl.pallas_call(
        _gdn_kernel,
        grid=(b, l // lt),
        in_specs=[
            pl.BlockSpec((1, lt, GDN_CONV_DIM), lambda i, t: (i, t, 1)),
            pl.BlockSpec((1, lt, GDN_WIDTH), lambda i, t: (i, t, 6)),
            pl.BlockSpec((1, lt, SMALL_COLS), lambda i, t: (i, t, 0)),
            pl.BlockSpec((GDN_CONV, GDN_CONV_DIM), lambda i, t: (0, 0)),
            pl.BlockSpec((2, SMALL_COLS), lambda i, t: (0, 0)),
            pl.BlockSpec((1, GDN_DV), lambda i, t: (0, 0)),
            pl.BlockSpec((1, GDN_CONV - 1, GDN_CONV_DIM), lambda i, t: (i, 0, 0)),
            pl.BlockSpec((1, GDN_HEADS, GDN_DK, GDN_DV), lambda i, t: (i, 0, 0, 0)),
        ],
        out_specs=[
            pl.BlockSpec((1, lt, GDN_WIDTH), lambda i, t: (i, t, 0)),
            pl.BlockSpec((1, GDN_HEADS, GDN_DK, GDN_DV), lambda i, t: (i, 0, 0, 0)),
            pl.BlockSpec((1, GDN_CONV - 1, GDN_CONV_DIM), lambda i, t: (i, 0, 0)),
        ],
        out_shape=[
            jax.ShapeDtypeStruct((b, l, GDN_WIDTH), BF16),
            jax.ShapeDtypeStruct((b, GDN_HEADS, GDN_DK, GDN_DV), F32),
            jax.ShapeDtypeStruct((b, GDN_CONV - 1, GDN_CONV_DIM), F32),
        ],
        scratch_shapes=[pltpu.VMEM((lt + CONV_PAD, GDN_CONV_DIM), F32)],
        compiler_params=_params("arbitrary", "arbitrary"),
        name="gdn_mix",
    )(proj, proj, small, conv_w, gate_vec, gdn_norm.reshape(1, GDN_DV), c0, s0)


def _out_proj_kernel(x_ref, oa_ref, ob_ref, w_ref, g1_ref, sc_ref, sh_ref, n2_ref, x1_ref, h2_ref,
                     *, n_sub):
    bt, lt, d = x_ref.shape
    if bt == 1:
        ls = lt // n_sub
        subs = [(slice(0, 1), slice(i * ls, (i + 1) * ls)) for i in range(n_sub)]
    else:
        bs = bt // n_sub
        subs = [(slice(i * bs, (i + 1) * bs), slice(0, lt)) for i in range(n_sub)]
    mixes = []
    for bsl, lsl in subs:
        oa = oa_ref[bsl, lsl, :]
        nb, nl, _ = oa.shape
        oa = oa.reshape(nb * nl, GLA_WIDTH)
        ob = ob_ref[bsl, lsl, :].reshape(nb * nl, GDN_WIDTH)
        mix = _dot(oa, w_ref[0:GLA_WIDTH, :]) + _dot(ob, w_ref[GLA_WIDTH:GLA_WIDTH + GDN_WIDTH, :])
        mixes.append(mix.reshape(nb, nl, d))
    for (bsl, lsl), mix in zip(subs, mixes):
        x1 = x_ref[bsl, lsl, :] + g1_ref[bsl] * mix
        x1_ref[bsl, lsl, :] = x1
        y = x1 * lax.rsqrt(jnp.mean(x1 * x1, axis=-1, keepdims=True) + EPS) * n2_ref[...]
        h2_ref[bsl, lsl, :] = (y * (1.0 + sc_ref[bsl]) + sh_ref[bsl]).astype(h2_ref.dtype)


def _out_proj(x, o_a, o_b, mod3, mod_row0, w_out16, norm2, bt, lt):
    b, l, d = x.shape
    mrow = mod_row0 // bt
    return pl.pallas_call(
        functools.partial(_out_proj_kernel, n_sub=2),
        grid=(b // bt, l // lt),
        in_specs=[
            pl.BlockSpec((bt, lt, d), lambda i, t: (i, t, 0)),
            pl.BlockSpec((bt, lt, GLA_WIDTH), lambda i, t: (i, t, 0)),
            pl.BlockSpec((bt, lt, GDN_WIDTH), lambda i, t: (i, t, 0)),
            pl.BlockSpec((GLA_WIDTH + GDN_WIDTH, d), lambda i, t: (0, 0)),
            pl.BlockSpec((bt, 1, d), lambda i, t: (mrow + i, 0, 2)),
            pl.BlockSpec((bt, 1, d), lambda i, t: (mrow + i, 0, 4)),
            pl.BlockSpec((bt, 1, d), lambda i, t: (mrow + i, 0, 3)),
            pl.BlockSpec((1, 1, d), lambda i, t: (0, 0, 0)),
        ],
        out_specs=[
            pl.BlockSpec((bt, lt, d), lambda i, t: (i, t, 0)),
            pl.BlockSpec((bt, lt, d), lambda i, t: (i, t, 0)),
        ],
        out_shape=[
            jax.ShapeDtypeStruct((b, l, d), F32),
            jax.ShapeDtypeStruct((b, l, d), BF16),
        ],
        compiler_params=_params("arbitrary", "arbitrary"),
        name="out_proj",
    )(x, o_a, o_b, w_out16, mod3, mod3, mod3, norm2.reshape(1, 1, d))


FFN_PAD = SUBLANES


def _ffn_kernel(h_ref, halo_ref, wg_ref, wv_ref, wd_ref, cw_ref, cb_ref, st_ref, x1_hbm, g2_ref,
                fn_ref, y_ref, so_ref, gbuf, x1_buf, x1_sem, *, has_halo, n_sub):
    ib = pl.program_id(0)
    t = pl.program_id(1)
    f = pl.program_id(2)
    bt, lt, d = h_ref.shape
    tf = wg_ref.shape[1]
    x1_copy = pltpu.make_async_copy(
        x1_hbm.at[pl.ds(ib * bt, bt), pl.ds(t * lt, lt), :], x1_buf, x1_sem)
    if bt == 1:
        ls = lt // n_sub
        subs = [(slice(0, 1), slice(i * ls, (i + 1) * ls)) for i in range(n_sub)]
    else:
        bs = bt // n_sub
        subs = [(slice(i * bs, (i + 1) * bs), slice(0, lt)) for i in range(n_sub)]

    @pl.when(f == 0)
    def _():
        x1_copy.start()
        y_ref[...] = jnp.zeros(y_ref.shape, F32)

    prev = st_ref[...]
    if has_halo:
        halo_gate = _dot(halo_ref[0], wg_ref[...])
        prev = jnp.where(t == 0, prev, halo_gate[SUBLANES - (FFN_CONV - 1):SUBLANES, :][None])
    gbuf[:, FFN_PAD - (FFN_CONV - 1):FFN_PAD, :] = prev
    gates, vals = [], []
    for bsl, lsl in subs:
        hb = h_ref[bsl, lsl, :]
        nb, nl, _ = hb.shape
        hb = hb.reshape(nb * nl, d)
        gate = _dot(hb, wg_ref[...]).reshape(nb, nl, tf)
        gbuf[bsl, FFN_PAD + lsl.start:FFN_PAD + lsl.stop, :] = gate
        gates.append(gate)
        vals.append(_dot(hb, wv_ref[...]))
    so_ref[...] = gbuf[:, FFN_PAD + lt - (FFN_CONV - 1):FFN_PAD + lt, :].reshape(so_ref.shape)
    acts = []
    for (bsl, lsl), gate, val in zip(subs, gates, vals):
        nb, nl, _ = gate.shape
        conv = cw_ref[FFN_CONV - 1:FFN_CONV, :] * gate
        for tap in range(FFN_CONV - 1):
            back = FFN_CONV - 1 - tap
            conv = conv + cw_ref[tap:tap + 1, :] * gbuf[bsl, FFN_PAD + lsl.start - back:FFN_PAD + lsl.stop - back, :]
        acts.append((_silu(conv + cb_ref[...]).reshape(nb * nl, tf) * val).astype(BF16))
    parts = [_dot(act, wd_ref[...]) for act in acts]
    for (bsl, lsl), gate, part in zip(subs, gates, parts):
        nb, nl, _ = gate.shape
        y_ref[bsl, lsl, :] += part.reshape(nb, nl, d)

    @pl.when(f == pl.num_programs(2) - 1)
    def _():
        x1_copy.wait()
        x2 = x1_buf[...] + g2_ref[...] * y_ref[...]
        y = x2 * lax.rsqrt(jnp.mean(x2 * x2, axis=-1, keepdims=True) + EPS) * fn_ref[...]
        y_ref[...] = y


def _ffn(h2, x1, st0, mod3, mod_row0, w_up16, w_down16, conv_w, conv_b, final_norm, bt, lt):
    b, l, d = h2.shape
    tf = 512
    nf = D_FF // tf
    mrow = mod_row0 // bt
    has_halo = lt < l
    halo_blocks = lt // SUBLANES
    kern = functools.partial(_ffn_kernel, has_halo=has_halo, n_sub=4)
    return pl.pallas_call(
        kern,
        grid=(b // bt, l // lt, nf),
        in_specs=[
            pl.BlockSpec((bt, lt, d), lambda i, t, f: (i, t, 0)),
            pl.BlockSpec((1, SUBLANES, d), lambda i, t, f: (i, jnp.maximum(t * halo_blocks - 1, 0), 0)),
            pl.BlockSpec((d, tf), lambda i, t, f: (0, f)),
            pl.BlockSpec((d, tf), lambda i, t, f: (0, nf + f)),
            pl.BlockSpec((tf, d), lambda i, t, f: (f, 0)),
            pl.BlockSpec((FFN_CONV, tf), lambda i, t, f: (0, f)),
            pl.BlockSpec((1, tf), lambda i, t, f: (0, f)),
            pl.BlockSpec((bt, FFN_CONV - 1, tf), lambda i, t, f: (i, 0, f)),
            pl.BlockSpec(memory_space=pl.ANY),
            pl.BlockSpec((bt, 1, d), lambda i, t, f: (mrow + i, 0, 5)),
            pl.BlockSpec((1, 1, d), lambda i, t, f: (0, 0, 0)),
        ],
        out_specs=[
            pl.BlockSpec((bt, lt, d), lambda i, t, f: (i, t, 0)),
            pl.BlockSpec((bt, 1, FFN_CONV - 1, tf), lambda i, t, f: (i, t, 0, f)),
        ],
        out_shape=[
            jax.ShapeDtypeStruct((b, l, d), F32),
            jax.ShapeDtypeStruct((b, l // lt, FFN_CONV - 1, D_FF), F32),
        ],
        scratch_shapes=[
            pltpu.VMEM((bt, lt + FFN_PAD, tf), F32),
            pltpu.VMEM((bt, lt, d), F32),
            pltpu.SemaphoreType.DMA(()),
        ],
        compiler_params=_params("arbitrary", "arbitrary", "arbitrary",
                                vmem_limit_bytes=FFN_VMEM_LIMIT_BYTES),
        name="ffn",
    )(h2, h2, w_up16, w_up16, w_down16, conv_w, conv_b.reshape(1, D_FF), st0, x1, mod3,
      final_norm.reshape(1, 1, d))


def _trunk(x, mod3, mod_row0, states, weights, cfg):
    (w_gla16, w_gdn16, w_small, norm1, gla_wg, gla_bg, gla_norm, gdn_conv_w, gate_vec, gdn_norm, w_out16,
     norm2, w_up16, w_down16, ffn_conv_w, ffn_conv_b, final_norm) = weights
    s_gla, s_gdn, s_conv, s_ffn = states
    proj, small = _in_proj(x, mod3, mod_row0, norm1, w_gla16, w_gdn16, w_small, cfg["in_bt"], cfg["in_lt"])
    o_a, n_gla = _gla_mix(proj, small, s_gla, gla_wg, gla_bg, gla_norm, cfg["mix_lt"])
    o_b, n_gdn, n_conv = _gdn_mix(proj, small, s_conv, s_gdn, gdn_conv_w, gate_vec, gdn_norm,
                                  cfg["mix_lt"])
    x1, h2 = _out_proj(x, o_a, o_b, mod3, mod_row0, w_out16, norm2, cfg["out_bt"], cfg["out_lt"])
    y, n_ffn = _ffn(h2, x1, s_ffn, mod3, mod_row0, w_up16, w_down16, ffn_conv_w, ffn_conv_b,
                    final_norm, cfg["ffn_bt"], cfg["ffn_lt"])
    return y, n_gla[None], n_gdn[None], n_conv[None], n_ffn[:, -1][None]


def kernel(x_prompt, x_sample, c_prompt, c_sample, state_gla, state_gdn, state_gdn_conv, state_ffn_conv, w_ada, b_ada, norm1, w_in, gla_wg, gla_bg, gla_norm, gdn_conv_w, gdn_a_log, gdn_dt_bias, gdn_norm, w_out, norm2, w_up, ffn_conv_w, ffn_conv_b, w_down, final_norm):
    bp = x_prompt.shape[0]
    bs = x_sample.shape[0]

    w_in16 = w_in[0].astype(BF16)
    w_gla16 = w_in16
    w_gdn16 = w_in16[:, IN_OFF_GDN:IN_OFF_BETA]
    w_small = jnp.concatenate(
        [w_in16[:, IN_OFF_LR:IN_OFF_GDN], w_in16[:, IN_OFF_BETA:],
         jnp.zeros((D_MODEL, SMALL_COLS - GLA_GATE_RANK - 2 * GDN_HEADS), BF16)], axis=1)
    w_out16 = w_out[0].astype(BF16)
    w_up16 = w_up[0].astype(BF16)
    w_down16 = w_down[0].astype(BF16)
    gate_vec = jnp.zeros((2, SMALL_COLS), F32)
    gate_vec = gate_vec.at[0, ALPHA_OFF:ALPHA_OFF + GDN_HEADS].set(gdn_a_log[0])
    gate_vec = gate_vec.at[1, ALPHA_OFF:ALPHA_OFF + GDN_HEADS].set(gdn_dt_bias[0])

    c_all = jnp.concatenate([c_sample, c_prompt], axis=0)
    mod = _ada_mod(c_all, w_ada[0], b_ada[0])
    mod3 = mod.reshape(bs + bp, 1, N_MOD * D_MODEL)

    weights = (w_gla16, w_gdn16, w_small, norm1[0], gla_wg[0], gla_bg[0], gla_norm[0], gdn_conv_w[0], gate_vec,
               gdn_norm[0], w_out16, norm2[0], w_up16, w_down16, ffn_conv_w[0], ffn_conv_b[0],
               final_norm)

    fresh = (jnp.zeros((bp, GLA_HEADS, GLA_DK, GLA_DV), F32),
             jnp.zeros((bp, GDN_HEADS, GDN_DK, GDN_DV), F32),
             jnp.zeros((bp, GDN_CONV - 1, GDN_CONV_DIM), F32),
             jnp.zeros((bp, FFN_CONV - 1, D_FF), F32))
    cfg_p = dict(in_bt=1, in_lt=1024, mix_lt=256, out_bt=1, out_lt=512, ffn_bt=1, ffn_lt=1024)
    y_p, p_gla, p_gdn, p_conv, p_ffn = _trunk(x_prompt, mod3, bs, fresh, weights, cfg_p)

    carried = (state_gla[0], state_gdn[0], state_gdn_conv[0], state_ffn_conv[0])
    ls = x_sample.shape[1]
    cfg_s = dict(in_bt=bs, in_lt=ls, mix_lt=ls, out_bt=bs // 2, out_lt=ls, ffn_bt=bs, ffn_lt=ls)
    y_s, s_gla, s_gdn, s_conv, s_ffn = _trunk(x_sample, mod3, 0, carried, weights, cfg_s)
    return (y_p, y_s, p_gla, p_gdn, p_conv, p_ffn, s_gla, s_gdn, s_conv, s_ffn)
```

```python
import functools

import jax
import jax.numpy as jnp
from jax import lax
from jax.experimental import pallas as pl
from jax.experimental.pallas import tpu as pltpu

F32 = jnp.float32
BF16 = jnp.bfloat16

D_MODEL = 2048
CHUNK = 64
GLA_HEADS = 4
GLA_DK = 128
GLA_DV = 256
GLA_WIDTH = GLA_HEADS * GLA_DV
GLA_GATE_RANK = 16
GLA_GATE_NORM = 16.0
GLA_MILD_LOG_DECAY = 60.0
GDN_HEADS = 8
GDN_DK = 128
GDN_DV = 128
GDN_WIDTH = GDN_HEADS * GDN_DV
GDN_CONV = 4
GDN_CONV_DIM = 3 * GDN_WIDTH
D_FF = 5632
FFN_CONV = 3
N_MOD = 6
EPS = 1e-6

GLA_COLS = 2 * GLA_HEADS * GLA_DK + 2 * GLA_WIDTH
GDN_COLS = 4 * GDN_WIDTH
MAIN_COLS = GLA_COLS + GDN_COLS
SMALL_COLS = 128
LR_OFF, BETA_OFF, ALPHA_OFF = 0, GLA_GATE_RANK, GLA_GATE_RANK + GDN_HEADS
IN_OFF_LR = GLA_COLS
IN_OFF_GDN = IN_OFF_LR + GLA_GATE_RANK
IN_OFF_BETA = IN_OFF_GDN + GDN_COLS

SUBLANES = 8
VMEM_LIMIT_BYTES = 56 * 1024 * 1024
FFN_VMEM_LIMIT_BYTES = 60 * 1024 * 1024

NT_DIMS = (((1,), (1,)), ((), ()))
TN_DIMS = (((0,), (0,)), ((), ()))


def _dot(a, b):
    return jnp.dot(a, b, preferred_element_type=F32)


def _dot_nt(a, b):
    return lax.dot_general(a, b, NT_DIMS, preferred_element_type=F32)


def _dot_tn(a, b):
    return lax.dot_general(a, b, TN_DIMS, preferred_element_type=F32)


def _dot_f32(a, b):
    return jnp.dot(a, b, preferred_element_type=F32, precision=lax.Precision.HIGHEST)


def _dot_many(a_list, b_list):
    return [_dot(a.astype(BF16), b.astype(BF16)) for a, b in zip(a_list, b_list)]


def _silu(x):
    return x * jax.nn.sigmoid(x)


def _softplus(x):
    return jnp.maximum(x, 0.0) + jnp.log1p(jnp.exp(-jnp.abs(x)))


def _log_sigmoid(x):
    return jnp.minimum(x, 0.0) - jnp.log1p(jnp.exp(-jnp.abs(x)))


def _params(*semantics, vmem_limit_bytes=VMEM_LIMIT_BYTES):
    return pltpu.CompilerParams(dimension_semantics=semantics, vmem_limit_bytes=vmem_limit_bytes)


def _tri_masks():
    row = lax.broadcasted_iota(jnp.int32, (CHUNK, CHUNK), 0)
    col = lax.broadcasted_iota(jnp.int32, (CHUNK, CHUNK), 1)
    return row >= col, row > col


def _ada_kernel(c_ref, w_ref, b_ref, o_ref):
    s = _silu(c_ref[...]).astype(BF16)
    o_ref[...] = _dot(s, w_ref[...].astype(BF16)) + b_ref[...]


def _ada_mod(c_all, w_ada, b_ada):
    rows = c_all.shape[0]
    n = w_ada.shape[1]
    tn = 1024
    return pl.pallas_call(
        _ada_kernel,
        grid=(n // tn,),
        in_specs=[
            pl.BlockSpec((rows, D_MODEL), lambda j: (0, 0)),
            pl.BlockSpec((D_MODEL, tn), lambda j: (0, j)),
            pl.BlockSpec((1, tn), lambda j: (0, j)),
        ],
        out_specs=pl.BlockSpec((rows, tn), lambda j: (0, j)),
        out_shape=jax.ShapeDtypeStruct((rows, n), F32),
        compiler_params=_params("arbitrary"),
        name="ada_mod",
    )(c_all, w_ada, b_ada.reshape(1, n))


def _in_proj_kernel(x_ref, sc_ref, sh_ref, n1_ref, wa_ref, wb_ref, ws_ref, o_ref, os_ref, h_scr,
                    *, n_a):
    bt, lt, d = x_ref.shape
    j = pl.program_id(2)

    @pl.when(j == 0)
    def _():
        x = x_ref[...]
        y = x * lax.rsqrt(jnp.mean(x * x, axis=-1, keepdims=True) + EPS) * n1_ref[...]
        h = y * (1.0 + sc_ref[...]) + sh_ref[...]
        hb = h.reshape(bt * lt, d).astype(BF16)
        h_scr[...] = hb
        os_ref[...] = _dot(hb, ws_ref[...]).reshape(os_ref.shape)

    @pl.when(j < n_a)
    def _():
        o_ref[...] = _dot(h_scr[...], wa_ref[...]).reshape(o_ref.shape)

    @pl.when(j >= n_a)
    def _():
        o_ref[...] = _dot(h_scr[...], wb_ref[...]).reshape(o_ref.shape)


def _in_proj(x, mod3, mod_row0, norm1, w_gla16, w_gdn16, w_small, bt, lt):
    b, l, d = x.shape
    tn = 1024
    n_a = GLA_COLS // tn
    mrow = mod_row0 // bt
    grid = (b // bt, l // lt, MAIN_COLS // tn)
    return pl.pallas_call(
        functools.partial(_in_proj_kernel, n_a=n_a),
        grid=grid,
        in_specs=[
            pl.BlockSpec((bt, lt, d), lambda i, t, j: (i, t, 0)),
            pl.BlockSpec((bt, 1, d), lambda i, t, j: (mrow + i, 0, 1)),
            pl.BlockSpec((bt, 1, d), lambda i, t, j: (mrow + i, 0, 0)),
            pl.BlockSpec((1, 1, d), lambda i, t, j: (0, 0, 0)),
            pl.BlockSpec((d, tn), lambda i, t, j: (0, jnp.minimum(j, n_a - 1))),
            pl.BlockSpec((d, tn), lambda i, t, j: (0, jnp.maximum(j - n_a, 0))),
            pl.BlockSpec((d, SMALL_COLS), lambda i, t, j: (0, 0)),
        ],
        out_specs=[
            pl.BlockSpec((bt, lt, tn), lambda i, t, j: (i, t, j)),
            pl.BlockSpec((bt, lt, SMALL_COLS), lambda i, t, j: (i, t, 0)),
        ],
        out_shape=[
            jax.ShapeDtypeStruct((b, l, MAIN_COLS), F32),
            jax.ShapeDtypeStruct((b, l, SMALL_COLS), F32),
        ],
        scratch_shapes=[pltpu.VMEM((bt * lt, d), BF16)],
        compiler_params=_params("arbitrary", "arbitrary", "arbitrary"),
        name="in_proj",
    )(x, mod3, mod3, norm1.reshape(1, 1, d), w_gla16, w_gdn16, w_small)


def _gla_levels():
    row = lax.broadcasted_iota(jnp.int32, (CHUNK, CHUNK), 0)
    col = lax.broadcasted_iota(jnp.int32, (CHUNK, CHUNK), 1)
    ops, masks = [], []
    half = CHUNK // 2
    while half >= 1:
        shift = half.bit_length()
        parent_row = jnp.left_shift(jnp.right_shift(row, shift), shift)
        parent_col = jnp.left_shift(jnp.right_shift(col, shift), shift)
        m_row = parent_row + (half - 1)
        right_row = row > m_row
        in_span = (right_row & (col > m_row) & (col <= row)) | (
            jnp.logical_not(right_row) & (col > row) & (col <= m_row))
        ops.append(jnp.where(in_span, 1.0, 0.0))
        masks.append((parent_row == parent_col) & right_row & (col <= m_row))
        half //= 2
    return ops, masks, row == col


def _gla_kernel(q_ref, k_ref, v_ref, r_ref, sm_ref, wg_ref, bg_ref, gn_ref, s0_ref,
                o_ref, s_ref, st_scr, attn_scr):
    t = pl.program_id(1)
    lt = q_ref.shape[1]
    n_chunks = lt // CHUNK
    causal, _ = _tri_masks()
    tri = causal.astype(F32)
    scale = GLA_DK ** -0.5

    @pl.when(t == 0)
    def _():
        for h in range(GLA_HEADS):
            st_scr[h] = s0_ref[0, h].T

    wg = wg_ref[...].astype(BF16)
    heads = range(GLA_HEADS)
    kcs = [slice(h * GLA_DK, (h + 1) * GLA_DK) for h in heads]
    vcs = [slice(h * GLA_DV, (h + 1) * GLA_DV) for h in heads]
    chunk_rows = [pl.ds(c * CHUNK, CHUNK) for c in range(n_chunks)]

    def scaled_q(rows):
        return [q_ref[0, rows, kc] * scale for kc in kcs]

    log_a, big_g = [], []
    for rows in chunk_rows:
        a_lr = sm_ref[0, rows, LR_OFF:LR_OFF + GLA_GATE_RANK].astype(BF16)
        la = _log_sigmoid(_dot(a_lr, wg) + bg_ref[...]) / GLA_GATE_NORM
        log_a.append(la)
        big_g.append(_dot_f32(tri, la))
    total = big_g[0][CHUNK - 1:CHUNK, :]
    for g in big_g[1:]:
        total = jnp.minimum(total, g[CHUNK - 1:CHUNK, :])
    mild = jnp.min(total) > -GLA_MILD_LOG_DECAY

    @pl.when(mild)
    def _():
        for c, rows in enumerate(chunk_rows):
            g_mid = big_g[c][CHUNK // 2 - 1:CHUNK // 2, :]
            e_q = jnp.exp(big_g[c] - g_mid)
            e_k = jnp.exp(g_mid - big_g[c])
            q = scaled_q(rows)
            scores = [_dot_nt((qq * e_q[:, kc]).astype(BF16), (k_ref[0, rows, kc] * e_k[:, kc]).astype(BF16))
                      for qq, kc in zip(q, kcs)]
            for h in heads:
                attn_scr[c, h] = jnp.where(causal, scores[h], 0.0)

    @pl.when(jnp.logical_not(mild))
    def _():
        level_ops, level_masks, diag = _gla_levels()
        level_op = jnp.concatenate(level_ops, axis=0)
        for c, rows in enumerate(chunk_rows):
            sums = _dot_f32(level_op, log_a[c])
            q = scaled_q(rows)
            k = [k_ref[0, rows, kc] for kc in kcs]
            scores = [jnp.where(diag, _dot_nt(qq.astype(BF16), kk.astype(BF16)), 0.0)
                      for qq, kk in zip(q, k)]
            for i, mask in enumerate(level_masks):
                e = jnp.exp(sums[i * CHUNK:(i + 1) * CHUNK])
                part = [_dot_nt((qq * e[:, kc]).astype(BF16), (kk * e[:, kc]).astype(BF16))
                        for qq, kk, kc in zip(q, k, kcs)]
                scores = [jnp.where(mask, p, a) for p, a in zip(part, scores)]
            for h in heads:
                attn_scr[c, h] = scores[h]

    pending = []
    for c, rows in enumerate(chunk_rows):
        g_last = big_g[c][CHUNK - 1:CHUNK, :]
        e_g = jnp.exp(big_g[c])
        e_kl = jnp.exp(g_last - big_g[c])
        e_l = jnp.exp(g_last)
        q = scaled_q(rows)
        v = [v_ref[0, rows, vc].astype(BF16) for vc in vcs]
        v_k = [_dot_tn(vv, (k_ref[0, rows, kc] * e_kl[:, kc]).astype(BF16)) for vv, kc in zip(v, kcs)]
        a_v = [_dot(attn_scr[c, h].astype(BF16), v[h]) for h in heads]
        q_g = [(qq * e_g[:, kc]).astype(BF16) for qq, kc in zip(q, kcs)]
        pending.append((rows, q_g, a_v, v_k, [e_l[:, kc] for kc in kcs]))
    for rows, q_g, a_v, v_k, e_l in pending:
        st = [st_scr[h] for h in heads]
        q_s = [_dot_nt(qq, ss.astype(BF16)) for qq, ss in zip(q_g, st)]
        for h in heads:
            st_scr[h] = st[h] * e_l[h] + v_k[h]
            o = a_v[h] + q_s[h]
            o = o * lax.rsqrt(jnp.mean(o * o, axis=-1, keepdims=True) + EPS) * gn_ref[...]
            o = o * _silu(r_ref[0, rows, vcs[h]])
            o_ref[0, rows, vcs[h]] = o.astype(o_ref.dtype)

    @pl.when(t == pl.num_programs(1) - 1)
    def _():
        for h in range(GLA_HEADS):
            s_ref[0, h] = st_scr[h].T


def _gla_mix(proj, small, s0, gla_wg, gla_bg, gla_norm, lt):
    b, l, _ = proj.shape
    kw = GLA_HEADS * GLA_DK
    return pl.pallas_call(
        _gla_kernel,
        grid=(b, l // lt),
        in_specs=[
            pl.BlockSpec((1, lt, kw), lambda i, t: (i, t, 0)),
            pl.BlockSpec((1, lt, kw), lambda i, t: (i, t, 1)),
            pl.BlockSpec((1, lt, GLA_WIDTH), lambda i, t: (i, t, 1)),
            pl.BlockSpec((1, lt, GLA_WIDTH), lambda i, t: (i, t, 2)),
            pl.BlockSpec((1, lt, SMALL_COLS), lambda i, t: (i, t, 0)),
            pl.BlockSpec((GLA_GATE_RANK, kw), lambda i, t: (0, 0)),
            pl.BlockSpec((1, kw), lambda i, t: (0, 0)),
            pl.BlockSpec((1, GLA_DV), lambda i, t: (0, 0)),
            pl.BlockSpec((1, GLA_HEADS, GLA_DK, GLA_DV), lambda i, t: (i, 0, 0, 0)),
        ],
        out_specs=[
            pl.BlockSpec((1, lt, GLA_WIDTH), lambda i, t: (i, t, 0)),
            pl.BlockSpec((1, GLA_HEADS, GLA_DK, GLA_DV), lambda i, t: (i, 0, 0, 0)),
        ],
        out_shape=[
            jax.ShapeDtypeStruct((b, l, GLA_WIDTH), BF16),
            jax.ShapeDtypeStruct((b, GLA_HEADS, GLA_DK, GLA_DV), F32),
        ],
        scratch_shapes=[pltpu.VMEM((GLA_HEADS, GLA_DV, GLA_DK), F32),
                        pltpu.VMEM((lt // CHUNK, GLA_HEADS, CHUNK, CHUNK), F32)],
        compiler_params=_params("arbitrary", "arbitrary"),
        name="gla_mix",
    )(proj, proj, proj, proj, small, gla_wg, gla_bg.reshape(1, kw), gla_norm.reshape(1, GLA_DV), s0)


CONV_PAD = SUBLANES
INV_BASE_LOG2 = 3
GDN_CHUNK_GROUP = 4


def _inverse_masks():
    row = lax.broadcasted_iota(jnp.int32, (CHUNK, CHUNK), 0)
    col = lax.broadcasted_iota(jnp.int32, (CHUNK, CHUNK), 1)

    def same_block(log2):
        return jnp.right_shift(row, log2) == jnp.right_shift(col, log2)

    base = same_block(INV_BASE_LOG2)
    merges = []
    log2 = INV_BASE_LOG2
    while (1 << log2) < CHUNK:
        merges.append(same_block(log2 + 1) & jnp.logical_not(same_block(log2)))
        log2 += 1
    return base, merges


def _unit_lower_inverse_minus_eye(lower, base, merges):
    neg = [jnp.where(base, -l, 0.0) for l in lower]
    x = neg
    p = neg
    for _ in range(INV_BASE_LOG2 - 1):
        p = _dot_many(p, p)
        xp = _dot_many(x, p)
        x = [a + b + c for a, b, c in zip(x, p, xp)]
    for m in merges:
        c = [jnp.where(m, l, 0.0) for l in lower]
        w = [a + b for a, b in zip(c, _dot_many(x, c))]
        wx = _dot_many(w, x)
        x = [a - (b + d) for a, b, d in zip(x, w, wx)]
    return x


def _gdn_kernel(x_ref, g_ref, sm_ref, cw_ref, ga_ref, gn_ref, c0_ref, s0_ref,
                o_ref, s_ref, c_ref, xbuf):
    t = pl.program_id(1)
    lt = x_ref.shape[1]
    causal, strict = _tri_masks()
    tri = causal.astype(F32)
    inv_base, inv_merges = _inverse_masks()
    scale = GDN_DK ** -0.5

    @pl.when(t == 0)
    def _():
        xbuf[0:CONV_PAD, :] = jnp.zeros((CONV_PAD, GDN_CONV_DIM), F32)
        xbuf[CONV_PAD - (GDN_CONV - 1):CONV_PAD, :] = c0_ref[0]
        s_ref[...] = s0_ref[...]

    @pl.when(t > 0)
    def _():
        xbuf[0:CONV_PAD, :] = xbuf[lt:lt + CONV_PAD, :]

    xbuf[CONV_PAD:CONV_PAD + lt, :] = x_ref[0]
    c_ref[0] = xbuf[lt + CONV_PAD - (GDN_CONV - 1):lt + CONV_PAD, :]

    def conv_silu(r0, cols):
        acc = cw_ref[GDN_CONV - 1:GDN_CONV, cols] * xbuf[r0 + CONV_PAD:r0 + CONV_PAD + CHUNK, cols]
        for tap in range(GDN_CONV - 1):
            back = GDN_CONV - 1 - tap
            acc = acc + cw_ref[tap:tap + 1, cols] * xbuf[r0 + CONV_PAD - back:r0 + CONV_PAD - back + CHUNK, cols]
        return _silu(acc)

    def l2norm(x):
        return x * lax.rsqrt(jnp.sum(x * x, axis=-1, keepdims=True) + EPS)

    heads = range(GDN_HEADS)
    def prep_gates(c):
        sm = sm_ref[0, pl.ds(c * CHUNK, CHUNK), :]
        gate = -jnp.exp(ga_ref[0:1, :]) * _softplus(sm + ga_ref[1:2, :])
        big_g = _dot_f32(tri, gate)
        return jax.nn.sigmoid(sm), big_g, big_g.T

    def prep_head(c, gates, h):
        beta, big_g, big_gt = gates
        r0 = c * CHUNK
        g_col = big_g[:, ALPHA_OFF + h:ALPHA_OFF + h + 1]
        g_row = big_gt[ALPHA_OFF + h:ALPHA_OFF + h + 1, :]
        b_col = beta[:, BETA_OFF + h:BETA_OFF + h + 1]
        g_last = g_col[CHUNK - 1:CHUNK, :]
        e_g = jnp.exp(g_col)
        q = l2norm(conv_silu(r0, slice(h * GDN_DK, (h + 1) * GDN_DK))) * scale
        k = l2norm(conv_silu(r0, slice(GDN_WIDTH + h * GDN_DK, GDN_WIDTH + (h + 1) * GDN_DK)))
        v = conv_silu(r0, slice(2 * GDN_WIDTH + h * GDN_DV, 2 * GDN_WIDTH + (h + 1) * GDN_DV))
        kb = k * b_col
        return dict(
            decay=jnp.where(causal, jnp.exp(g_col - g_row), 0.0),
            rhs=jnp.concatenate([v * b_col, kb * e_g], axis=-1),
            q16=q.astype(BF16), k16=k.astype(BF16), kb16=kb.astype(BF16),
            q_g=(q * e_g).astype(BF16),
            k_dec=(k * jnp.exp(g_last - g_col)).astype(BF16),
            e_last=jnp.exp(g_last))

    def key_products(p):
        kk_t = [_dot_nt(ph["kb16"], ph["k16"]) for ph in p]
        qk_t = [_dot_nt(ph["q16"], ph["k16"]) for ph in p]
        lower = [jnp.where(strict, m * ph["decay"], 0.0) for m, ph in zip(kk_t, p)]
        attn = [jnp.where(causal, m * ph["decay"], 0.0).astype(BF16) for m, ph in zip(qk_t, p)]
        return lower, attn

    n_chunks = lt // CHUNK
    pending = []
    for c0 in range(0, n_chunks, GDN_CHUNK_GROUP):
        chunks = range(c0, min(c0 + GDN_CHUNK_GROUP, n_chunks))
        gates = [prep_gates(c) for c in chunks]
        p = [prep_head(c, g, h) for c, g in zip(chunks, gates) for h in heads]
        lower, attn = key_products(p)
        xinv = _unit_lower_inverse_minus_eye(lower, inv_base, inv_merges)
        rhs = [ph["rhs"] for ph in p]
        sol = [r + xr for r, xr in zip(rhs, _dot_many(xinv, rhs))]
        for n, c in enumerate(chunks):
            sl = slice(n * GDN_HEADS, (n + 1) * GDN_HEADS)
            pending.append((pl.ds(c * CHUNK, CHUNK), attn[sl],
                            [so[:, :GDN_DV] for so in sol[sl]],
                            [so[:, GDN_DV:].astype(BF16) for so in sol[sl]],
                            [ph["q_g"] for ph in p[sl]], [ph["k_dec"] for ph in p[sl]],
                            [ph["e_last"] for ph in p[sl]]))
    for rows, attn, sol_v, sol_k, q_g, k_dec, e_last in pending:
        s = [s_ref[0, h] for h in heads]
        s16 = [ss.astype(BF16) for ss in s]
        k_s = [_dot(a, ss) for a, ss in zip(sol_k, s16)]
        q_s = [_dot(a, ss) for a, ss in zip(q_g, s16)]
        u16 = [(sv - ks).astype(BF16) for sv, ks in zip(sol_v, k_s)]
        a_u = [_dot(a, uu) for a, uu in zip(attn, u16)]
        k_u = [_dot_tn(kd, uu) for kd, uu in zip(k_dec, u16)]
        for h in heads:
            hc = slice(h * GDN_DV, (h + 1) * GDN_DV)
            s_ref[0, h] = e_last[h] * s[h] + k_u[h]
            o = q_s[h] + a_u[h]
            o = o * lax.rsqrt(jnp.mean(o * o, axis=-1, keepdims=True) + EPS) * gn_ref[...]
            o = o * _silu(g_ref[0, rows, hc])
            o_ref[0, rows, hc] = o.astype(o_ref.dtype)


def _gdn_mix(proj, small, c0, s0, conv_w, gate_vec, gdn_norm, lt):
    b, l, _ = proj.shape
    return pl.pallas_call(
        _gdn_kernel,
        grid=(b, l // lt),
        in_specs=[
            pl.BlockSpec((1, lt, GDN_CONV_DIM), lambda i, t: (i, t, 1)),
            pl.BlockSpec((1, lt, GDN_WIDTH), lambda i, t: (i, t, 6)),
            pl.BlockSpec((1, lt, SMALL_COLS), lambda i, t: (i, t, 0)),
            pl.BlockSpec((GDN_CONV, GDN_CONV_DIM), lambda i, t: (0, 0)),
            pl.BlockSpec((2, SMALL_COLS), lambda i, t: (0, 0)),
            pl.BlockSpec((1, GDN_DV), lambda i, t: (0, 0)),
            pl.BlockSpec((1, GDN_CONV - 1, GDN_CONV_DIM), lambda i, t: (i, 0, 0)),
            pl.BlockSpec((1, GDN_HEADS, GDN_DK, GDN_DV), lambda i, t: (i, 0, 0, 0)),
        ],
        out_specs=[
            pl.BlockSpec((1, lt, GDN_WIDTH), lambda i, t: (i, t, 0)),
            pl.BlockSpec((1, GDN_HEADS, GDN_DK, GDN_DV), lambda i, t: (i, 0, 0, 0)),
            pl.BlockSpec((1, GDN_CONV - 1, GDN_CONV_DIM), lambda i, t: (i, 0, 0)),
        ],
        out_shape=[
            jax.ShapeDtypeStruct((b, l, GDN_WIDTH), BF16),
            jax.ShapeDtypeStruct((b, GDN_HEADS, GDN_DK, GDN_DV), F32),
            jax.ShapeDtypeStruct((b, GDN_CONV - 1, GDN_CONV_DIM), F32),
        ],
        scratch_shapes=[pltpu.VMEM((lt + CONV_PAD, GDN_CONV_DIM), F32)],
        compiler_params=_params("arbitrary", "arbitrary"),
        name="gdn_mix",
    )(proj, proj, small, conv_w, gate_vec, gdn_norm.reshape(1, GDN_DV), c0, s0)


def _out_proj_kernel(x_ref, oa_ref, ob_ref, w_ref, g1_ref, sc_ref, sh_ref, n2_ref, x1_ref, h2_ref,
                     *, n_sub):
    bt, lt, d = x_ref.shape
    if bt == 1:
        ls = lt // n_sub
        subs = [(slice(0, 1), slice(i * ls, (i + 1) * ls)) for i in range(n_sub)]
    else:
        bs = bt // n_sub
        subs = [(slice(i * bs, (i + 1) * bs), slice(0, lt)) for i in range(n_sub)]
    mixes = []
    for bsl, lsl in subs:
        oa = oa_ref[bsl, lsl, :]
        nb, nl, _ = oa.shape
        oa = oa.reshape(nb * nl, GLA_WIDTH)
        ob = ob_ref[bsl, lsl, :].reshape(nb * nl, GDN_WIDTH)
        mix = _dot(oa, w_ref[0:GLA_WIDTH, :]) + _dot(ob, w_ref[GLA_WIDTH:GLA_WIDTH + GDN_WIDTH, :])
        mixes.append(mix.reshape(nb, nl, d))
    for (bsl, lsl), mix in zip(subs, mixes):
        x1 = x_ref[bsl, lsl, :] + g1_ref[bsl] * mix
        x1_ref[bsl, lsl, :] = x1
        y = x1 * lax.rsqrt(jnp.mean(x1 * x1, axis=-1, keepdims=True) + EPS) * n2_ref[...]
        h2_ref[bsl, lsl, :] = (y * (1.0 + sc_ref[bsl]) + sh_ref[bsl]).astype(h2_ref.dtype)


def _out_proj(x, o_a, o_b, mod3, mod_row0, w_out16, norm2, bt, lt):
    b, l, d = x.shape
    mrow = mod_row0 // bt
    return pl.pallas_call(
        functools.partial(_out_proj_kernel, n_sub=2),
        grid=(b // bt, l // lt),
        in_specs=[
            pl.BlockSpec((bt, lt, d), lambda i, t: (i, t, 0)),
            pl.BlockSpec((bt, lt, GLA_WIDTH), lambda i, t: (i, t, 0)),
            pl.BlockSpec((bt, lt, GDN_WIDTH), lambda i, t: (i, t, 0)),
            pl.BlockSpec((GLA_WIDTH + GDN_WIDTH, d), lambda i, t: (0, 0)),
            pl.BlockSpec((bt, 1, d), lambda i, t: (mrow + i, 0, 2)),
            pl.BlockSpec((bt, 1, d), lambda i, t: (mrow + i, 0, 4)),
            pl.BlockSpec((bt, 1, d), lambda i, t: (mrow + i, 0, 3)),
            pl.BlockSpec((1, 1, d), lambda i, t: (0, 0, 0)),
        ],
        out_specs=[
            pl.BlockSpec((bt, lt, d), lambda i, t: (i, t, 0)),
            pl.BlockSpec((bt, lt, d), lambda i, t: (i, t, 0)),
        ],
        out_shape=[
            jax.ShapeDtypeStruct((b, l, d), F32),
            jax.ShapeDtypeStruct((b, l, d), BF16),
        ],
        compiler_params=_params("arbitrary", "arbitrary"),
        name="out_proj",
    )(x, o_a, o_b, w_out16, mod3, mod3, mod3, norm2.reshape(1, 1, d))


FFN_PAD = SUBLANES


def _ffn_kernel(h_ref, halo_ref, wg_ref, wv_ref, wd_ref, cw_ref, cb_ref, st_ref, x1_hbm, g2_ref,
                fn_ref, y_ref, so_ref, gbuf, x1_buf, x1_sem, *, has_halo, n_sub):
    ib = pl.program_id(0)
    t = pl.program_id(1)
    f = pl.program_id(2)
    bt, lt, d = h_ref.shape
    tf = wg_ref.shape[1]
    x1_copy = pltpu.make_async_copy(
        x1_hbm.at[pl.ds(ib * bt, bt), pl.ds(t * lt, lt), :], x1_buf, x1_sem)
    if bt == 1:
        ls = lt // n_sub
        subs = [(slice(0, 1), slice(i * ls, (i + 1) * ls)) for i in range(n_sub)]
    else:
        bs = bt // n_sub
        subs = [(slice(i * bs, (i + 1) * bs), slice(0, lt)) for i in range(n_sub)]

    @pl.when(f == 0)
    def _():
        x1_copy.start()
        y_ref[...] = jnp.zeros(y_ref.shape, F32)

    prev = st_ref[...]
    if has_halo:
        halo_gate = _dot(halo_ref[0], wg_ref[...])
        prev = jnp.where(t == 0, prev, halo_gate[SUBLANES - (FFN_CONV - 1):SUBLANES, :][None])
    gbuf[:, FFN_PAD - (FFN_CONV - 1):FFN_PAD, :] = prev
    gates, vals = [], []
    for bsl, lsl in subs:
        hb = h_ref[bsl, lsl, :]
        nb, nl, _ = hb.shape
        hb = hb.reshape(nb * nl, d)
        gate = _dot(hb, wg_ref[...]).reshape(nb, nl, tf)
        gbuf[bsl, FFN_PAD + lsl.start:FFN_PAD + lsl.stop, :] = gate
        gates.append(gate)
        vals.append(_dot(hb, wv_ref[...]))
    so_ref[...] = gbuf[:, FFN_PAD + lt - (FFN_CONV - 1):FFN_PAD + lt, :].reshape(so_ref.shape)
    acts = []
    for (bsl, lsl), gate, val in zip(subs, gates, vals):
        nb, nl, _ = gate.shape
        conv = cw_ref[FFN_CONV - 1:FFN_CONV, :] * gate
        for tap in range(FFN_CONV - 1):
            back = FFN_CONV - 1 - tap
            conv = conv + cw_ref[tap:tap + 1, :] * gbuf[bsl, FFN_PAD + lsl.start - back:FFN_PAD + lsl.stop - back, :]
        acts.append((_silu(conv + cb_ref[...]).reshape(nb * nl, tf) * val).astype(BF16))
    parts = [_dot(act, wd_ref[...]) for act in acts]
    for (bsl, lsl), gate, part in zip(subs, gates, parts):
        nb, nl, _ = gate.shape
        y_ref[bsl, lsl, :] += part.reshape(nb, nl, d)

    @pl.when(f == pl.num_programs(2) - 1)
    def _():
        x1_copy.wait()
        x2 = x1_buf[...] + g2_ref[...] * y_ref[...]
        y = x2 * lax.rsqrt(jnp.mean(x2 * x2, axis=-1, keepdims=True) + EPS) * fn_ref[...]
        y_ref[...] = y


def _ffn(h2, x1, st0, mod3, mod_row0, w_up16, w_down16, conv_w, conv_b, final_norm, bt, lt):
    b, l, d = h2.shape
    tf = 512
    nf = D_FF // tf
    mrow = mod_row0 // bt
    has_halo = lt < l
    halo_blocks = lt // SUBLANES
    kern = functools.partial(_ffn_kernel, has_halo=has_halo, n_sub=4)
    return pl.pallas_call(
        kern,
        grid=(b // bt, l // lt, nf),
        in_specs=[
            pl.BlockSpec((bt, lt, d), lambda i, t, f: (i, t, 0)),
            pl.BlockSpec((1, SUBLANES, d), lambda i, t, f: (i, jnp.maximum(t * halo_blocks - 1, 0), 0)),
            pl.BlockSpec((d, tf), lambda i, t, f: (0, f)),
            pl.BlockSpec((d, tf), lambda i, t, f: (0, nf + f)),
            pl.BlockSpec((tf, d), lambda i, t, f: (f, 0)),
            pl.BlockSpec((FFN_CONV, tf), lambda i, t, f: (0, f)),
            pl.BlockSpec((1, tf), lambda i, t, f: (0, f)),
            pl.BlockSpec((bt, FFN_CONV - 1, tf), lambda i, t, f: (i, 0, f)),
            pl.BlockSpec(memory_space=pl.ANY),
            pl.BlockSpec((bt, 1, d), lambda i, t, f: (mrow + i, 0, 5)),
            pl.BlockSpec((1, 1, d), lambda i, t, f: (0, 0, 0)),
        ],
        out_specs=[
            pl.BlockSpec((bt, lt, d), lambda i, t, f: (i, t, 0)),
            pl.BlockSpec((bt, 1, FFN_CONV - 1, tf), lambda i, t, f: (i, t, 0, f)),
        ],
        out_shape=[
            jax.ShapeDtypeStruct((b, l, d), F32),
            jax.ShapeDtypeStruct((b, l // lt, FFN_CONV - 1, D_FF), F32),
        ],
        scratch_shapes=[
            pltpu.VMEM((bt, lt + FFN_PAD, tf), F32),
            pltpu.VMEM((bt, lt, d), F32),
            pltpu.SemaphoreType.DMA(()),
        ],
        compiler_params=_params("arbitrary", "arbitrary", "arbitrary",
                                vmem_limit_bytes=FFN_VMEM_LIMIT_BYTES),
        name="ffn",
    )(h2, h2, w_up16, w_up16, w_down16, conv_w, conv_b.reshape(1, D_FF), st0, x1, mod3,
      final_norm.reshape(1, 1, d))


def _trunk(x, mod3, mod_row0, states, weights, cfg):
    (w_gla16, w_gdn16, w_small, norm1, gla_wg, gla_bg, gla_norm, gdn_conv_w, gate_vec, gdn_norm, w_out16,
     norm2, w_up16, w_down16, ffn_conv_w, ffn_conv_b, final_norm) = weights
    s_gla, s_gdn, s_conv, s_ffn = states
    proj, small = _in_proj(x, mod3, mod_row0, norm1, w_gla16, w_gdn16, w_small, cfg["in_bt"], cfg["in_lt"])
    o_a, n_gla = _gla_mix(proj, small, s_gla, gla_wg, gla_bg, gla_norm, cfg["gla_lt"])
    o_b, n_gdn, n_conv = _gdn_mix(proj, small, s_conv, s_gdn, gdn_conv_w, gate_vec, gdn_norm,
                                  cfg["gdn_lt"])
    x1, h2 = _out_proj(x, o_a, o_b, mod3, mod_row0, w_out16, norm2, cfg["out_bt"], cfg["out_lt"])
    y, n_ffn = _ffn(h2, x1, s_ffn, mod3, mod_row0, w_up16, w_down16, ffn_conv_w, ffn_conv_b,
                    final_norm, cfg["ffn_bt"], cfg["ffn_lt"])
    return y, n_gla[None], n_gdn[None], n_conv[None], n_ffn[:, -1][None]


def kernel(x_prompt, x_sample, c_prompt, c_sample, state_gla, state_gdn, state_gdn_conv, state_ffn_conv, w_ada, b_ada, norm1, w_in, gla_wg, gla_bg, gla_norm, gdn_conv_w, gdn_a_log, gdn_dt_bias, gdn_norm, w_out, norm2, w_up, ffn_conv_w, ffn_conv_b, w_down, final_norm):
    bp = x_prompt.shape[0]
    bs = x_sample.shape[0]

    w_in16 = w_in[0].astype(BF16)
    w_gla16 = w_in16
    w_gdn16 = w_in16[:, IN_OFF_GDN:IN_OFF_BETA]
    w_small = jnp.concatenate(
        [w_in16[:, IN_OFF_LR:IN_OFF_GDN], w_in16[:, IN_OFF_BETA:],
         jnp.zeros((D_MODEL, SMALL_COLS - GLA_GATE_RANK - 2 * GDN_HEADS), BF16)], axis=1)
    w_out16 = w_out[0].astype(BF16)
    w_up16 = w_up[0].astype(BF16)
    w_down16 = w_down[0].astype(BF16)
    gate_vec = jnp.zeros((2, SMALL_COLS), F32)
    gate_vec = gate_vec.at[0, ALPHA_OFF:ALPHA_OFF + GDN_HEADS].set(gdn_a_log[0])
    gate_vec = gate_vec.at[1, ALPHA_OFF:ALPHA_OFF + GDN_HEADS].set(gdn_dt_bias[0])

    c_all = jnp.concatenate([c_sample, c_prompt], axis=0)
    mod = _ada_mod(c_all, w_ada[0], b_ada[0])
    mod3 = mod.reshape(bs + bp, 1, N_MOD * D_MODEL)

    weights = (w_gla16, w_gdn16, w_small, norm1[0], gla_wg[0], gla_bg[0], gla_norm[0], gdn_conv_w[0], gate_vec,
               gdn_norm[0], w_out16, norm2[0], w_up16, w_down16, ffn_conv_w[0], ffn_conv_b[0],
               final_norm)

    fresh = (jnp.zeros((bp, GLA_HEADS, GLA_DK, GLA_DV), F32),
             jnp.zeros((bp, GDN_HEADS, GDN_DK, GDN_DV), F32),
             jnp.zeros((bp, GDN_CONV - 1, GDN_CONV_DIM), F32),
             jnp.zeros((bp, FFN_CONV - 1, D_FF), F32))
    cfg_p = dict(in_bt=1, in_lt=1024, gla_lt=512, gdn_lt=256, out_bt=1, out_lt=512, ffn_bt=1, ffn_lt=1024)
    y_p, p_gla, p_gdn, p_conv, p_ffn = _trunk(x_prompt, mod3, bs, fresh, weights, cfg_p)

    carried = (state_gla[0], state_gdn[0], state_gdn_conv[0], state_ffn_conv[0])
    ls = x_sample.shape[1]
    cfg_s = dict(in_bt=bs, in_lt=ls, gla_lt=ls, gdn_lt=ls, out_bt=bs // 2, out_lt=ls, ffn_bt=bs, ffn_lt=ls)
    y_s, s_gla, s_gdn, s_conv, s_ffn = _trunk(x_sample, mod3, 0, carried, weights, cfg_s)
    return (y_p, y_s, p_gla, p_gdn, p_conv, p_ffn, s_gla, s_gdn, s_conv, s_ffn)
```

```python
import functools

import jax
import jax.numpy as jnp
from jax import lax
from jax.experimental import pallas as pl
from jax.experimental.pallas import tpu as pltpu

F32 = jnp.float32
BF16 = jnp.bfloat16

D_MODEL = 2048
CHUNK = 64
GLA_HEADS = 4
GLA_DK = 128
GLA_DV = 256
GLA_WIDTH = GLA_HEADS * GLA_DV
GLA_GATE_RANK = 16
GLA_GATE_NORM = 16.0
GLA_MILD_LOG_DECAY = 60.0
GDN_HEADS = 8
GDN_DK = 128
GDN_DV = 128
GDN_WIDTH = GDN_HEADS * GDN_DV
GDN_CONV = 4
GDN_CONV_DIM = 3 * GDN_WIDTH
D_FF = 5632
FFN_CONV = 3
N_MOD = 6
EPS = 1e-6

GLA_COLS = 2 * GLA_HEADS * GLA_DK + 2 * GLA_WIDTH
GDN_COLS = 4 * GDN_WIDTH
MAIN_COLS = GLA_COLS + GDN_COLS
SMALL_COLS = 128
PROJ_DTYPE = BF16
LR_OFF, BETA_OFF, ALPHA_OFF = 0, GLA_GATE_RANK, GLA_GATE_RANK + GDN_HEADS
IN_OFF_LR = GLA_COLS
IN_OFF_GDN = IN_OFF_LR + GLA_GATE_RANK
IN_OFF_BETA = IN_OFF_GDN + GDN_COLS

SUBLANES = 8
VMEM_LIMIT_BYTES = 56 * 1024 * 1024
FFN_VMEM_LIMIT_BYTES = 60 * 1024 * 1024

NT_DIMS = (((1,), (1,)), ((), ()))
TN_DIMS = (((0,), (0,)), ((), ()))


def _dot(a, b):
    return jnp.dot(a, b, preferred_element_type=F32)


def _dot_nt(a, b):
    return lax.dot_general(a, b, NT_DIMS, preferred_element_type=F32)


def _dot_tn(a, b):
    return lax.dot_general(a, b, TN_DIMS, preferred_element_type=F32)


def _dot_f32(a, b):
    return jnp.dot(a, b, preferred_element_type=F32, precision=lax.Precision.HIGHEST)


def _dot_many(a_list, b_list):
    return [_dot(a.astype(BF16), b.astype(BF16)) for a, b in zip(a_list, b_list)]


def _silu(x):
    return x * jax.nn.sigmoid(x)


def _softplus(x):
    return jnp.maximum(x, 0.0) + jnp.log1p(jnp.exp(-jnp.abs(x)))


def _log_sigmoid(x):
    return jnp.minimum(x, 0.0) - jnp.log1p(jnp.exp(-jnp.abs(x)))


def _params(*semantics, vmem_limit_bytes=VMEM_LIMIT_BYTES):
    return pltpu.CompilerParams(dimension_semantics=semantics, vmem_limit_bytes=vmem_limit_bytes)


def _tri_masks():
    row = lax.broadcasted_iota(jnp.int32, (CHUNK, CHUNK), 0)
    col = lax.broadcasted_iota(jnp.int32, (CHUNK, CHUNK), 1)
    return row >= col, row > col


def _ada_kernel(c_ref, w_ref, b_ref, o_ref):
    s = _silu(c_ref[...]).astype(BF16)
    o_ref[...] = _dot(s, w_ref[...].astype(BF16)) + b_ref[...]


def _ada_mod(c_all, w_ada, b_ada):
    rows = c_all.shape[0]
    n = w_ada.shape[1]
    tn = 1024
    return pl.pallas_call(
        _ada_kernel,
        grid=(n // tn,),
        in_specs=[
            pl.BlockSpec((rows, D_MODEL), lambda j: (0, 0)),
            pl.BlockSpec((D_MODEL, tn), lambda j: (0, j)),
            pl.BlockSpec((1, tn), lambda j: (0, j)),
        ],
        out_specs=pl.BlockSpec((rows, tn), lambda j: (0, j)),
        out_shape=jax.ShapeDtypeStruct((rows, n), F32),
        compiler_params=_params("arbitrary"),
        name="ada_mod",
    )(c_all, w_ada, b_ada.reshape(1, n))


def _in_proj_kernel(x_ref, sc_ref, sh_ref, n1_ref, wa_ref, wb_ref, ws_ref, o_ref, os_ref, h_scr,
                    *, n_a):
    bt, lt, d = x_ref.shape
    j = pl.program_id(2)

    @pl.when(j == 0)
    def _():
        x = x_ref[...]
        y = x * lax.rsqrt(jnp.mean(x * x, axis=-1, keepdims=True) + EPS) * n1_ref[...]
        h = y * (1.0 + sc_ref[...]) + sh_ref[...]
        hb = h.reshape(bt * lt, d).astype(BF16)
        h_scr[...] = hb
        os_ref[...] = _dot(hb, ws_ref[...]).reshape(os_ref.shape)

    @pl.when(j < n_a)
    def _():
        o_ref[...] = _dot(h_scr[...], wa_ref[...]).reshape(o_ref.shape).astype(o_ref.dtype)

    @pl.when(j >= n_a)
    def _():
        o_ref[...] = _dot(h_scr[...], wb_ref[...]).reshape(o_ref.shape).astype(o_ref.dtype)


def _in_proj(x, mod3, mod_row0, norm1, w_gla16, w_gdn16, w_small, bt, lt):
    b, l, d = x.shape
    tn = 1024
    n_a = GLA_COLS // tn
    mrow = mod_row0 // bt
    grid = (b // bt, l // lt, MAIN_COLS // tn)
    return pl.pallas_call(
        functools.partial(_in_proj_kernel, n_a=n_a),
        grid=grid,
        in_specs=[
            pl.BlockSpec((bt, lt, d), lambda i, t, j: (i, t, 0)),
            pl.BlockSpec((bt, 1, d), lambda i, t, j: (mrow + i, 0, 1)),
            pl.BlockSpec((bt, 1, d), lambda i, t, j: (mrow + i, 0, 0)),
            pl.BlockSpec((1, 1, d), lambda i, t, j: (0, 0, 0)),
            pl.BlockSpec((d, tn), lambda i, t, j: (0, jnp.minimum(j, n_a - 1))),
            pl.BlockSpec((d, tn), lambda i, t, j: (0, jnp.maximum(j - n_a, 0))),
            pl.BlockSpec((d, SMALL_COLS), lambda i, t, j: (0, 0)),
        ],
        out_specs=[
            pl.BlockSpec((bt, lt, tn), lambda i, t, j: (i, t, j)),
            pl.BlockSpec((bt, lt, SMALL_COLS), lambda i, t, j: (i, t, 0)),
        ],
        out_shape=[
            jax.ShapeDtypeStruct((b, l, MAIN_COLS), PROJ_DTYPE),
            jax.ShapeDtypeStruct((b, l, SMALL_COLS), F32),
        ],
        scratch_shapes=[pltpu.VMEM((bt * lt, d), BF16)],
        compiler_params=_params("arbitrary", "arbitrary", "arbitrary"),
        name="in_proj",
    )(x, mod3, mod3, norm1.reshape(1, 1, d), w_gla16, w_gdn16, w_small)


def _gla_levels():
    row = lax.broadcasted_iota(jnp.int32, (CHUNK, CHUNK), 0)
    col = lax.broadcasted_iota(jnp.int32, (CHUNK, CHUNK), 1)
    ops, masks = [], []
    half = CHUNK // 2
    while half >= 1:
        shift = half.bit_length()
        parent_row = jnp.left_shift(jnp.right_shift(row, shift), shift)
        parent_col = jnp.left_shift(jnp.right_shift(col, shift), shift)
        m_row = parent_row + (half - 1)
        right_row = row > m_row
        in_span = (right_row & (col > m_row) & (col <= row)) | (
            jnp.logical_not(right_row) & (col > row) & (col <= m_row))
        ops.append(jnp.where(in_span, 1.0, 0.0))
        masks.append((parent_row == parent_col) & right_row & (col <= m_row))
        half //= 2
    return ops, masks, row == col


def _gla_kernel(q_ref, k_ref, v_ref, r_ref, sm_ref, wg_ref, bg_ref, gn_ref, s0_ref,
                o_ref, s_ref, st_scr, attn_scr):
    t = pl.program_id(1)
    lt = q_ref.shape[1]
    n_chunks = lt // CHUNK
    causal, _ = _tri_masks()
    tri = causal.astype(F32)
    scale = GLA_DK ** -0.5

    @pl.when(t == 0)
    def _():
        for h in range(GLA_HEADS):
            st_scr[h] = s0_ref[0, h].T

    wg = wg_ref[...].astype(BF16)
    heads = range(GLA_HEADS)
    kcs = [slice(h * GLA_DK, (h + 1) * GLA_DK) for h in heads]
    vcs = [slice(h * GLA_DV, (h + 1) * GLA_DV) for h in heads]
    chunk_rows = [pl.ds(c * CHUNK, CHUNK) for c in range(n_chunks)]

    def scaled_q(rows):
        return [q_ref[0, rows, kc].astype(F32) * scale for kc in kcs]

    log_a, big_g = [], []
    for rows in chunk_rows:
        a_lr = sm_ref[0, rows, LR_OFF:LR_OFF + GLA_GATE_RANK].astype(BF16)
        la = _log_sigmoid(_dot(a_lr, wg) + bg_ref[...]) / GLA_GATE_NORM
        log_a.append(la)
        big_g.append(_dot_f32(tri, la))
    total = big_g[0][CHUNK - 1:CHUNK, :]
    for g in big_g[1:]:
        total = jnp.minimum(total, g[CHUNK - 1:CHUNK, :])
    mild = jnp.min(total) > -GLA_MILD_LOG_DECAY

    @pl.when(mild)
    def _():
        for c, rows in enumerate(chunk_rows):
            g_mid = big_g[c][CHUNK // 2 - 1:CHUNK // 2, :]
            e_q = jnp.exp(big_g[c] - g_mid)
            e_k = jnp.exp(g_mid - big_g[c])
            q = scaled_q(rows)
            scores = [_dot_nt((qq * e_q[:, kc]).astype(BF16), (k_ref[0, rows, kc].astype(F32) * e_k[:, kc]).astype(BF16))
                      for qq, kc in zip(q, kcs)]
            for h in heads:
                attn_scr[c, h] = jnp.where(causal, scores[h], 0.0)

    @pl.when(jnp.logical_not(mild))
    def _():
        level_ops, level_masks, diag = _gla_levels()
        level_op = jnp.concatenate(level_ops, axis=0)
        for c, rows in enumerate(chunk_rows):
            sums = _dot_f32(level_op, log_a[c])
            q = scaled_q(rows)
            k = [k_ref[0, rows, kc].astype(F32) for kc in kcs]
            scores = [jnp.where(diag, _dot_nt(qq.astype(BF16), kk.astype(BF16)), 0.0)
                      for qq, kk in zip(q, k)]
            for i, mask in enumerate(level_masks):
                e = jnp.exp(sums[i * CHUNK:(i + 1) * CHUNK])
                part = [_dot_nt((qq * e[:, kc]).astype(BF16), (kk * e[:, kc]).astype(BF16))
                        for qq, kk, kc in zip(q, k, kcs)]
                scores = [jnp.where(mask, p, a) for p, a in zip(part, scores)]
            for h in heads:
                attn_scr[c, h] = scores[h]

    pending = []
    for c, rows in enumerate(chunk_rows):
        g_last = big_g[c][CHUNK - 1:CHUNK, :]
        e_g = jnp.exp(big_g[c])
        e_kl = jnp.exp(g_last - big_g[c])
        e_l = jnp.exp(g_last)
        q = scaled_q(rows)
        v = [v_ref[0, rows, vc].astype(BF16) for vc in vcs]
        v_k = [_dot_tn(vv, (k_ref[0, rows, kc].astype(F32) * e_kl[:, kc]).astype(BF16)) for vv, kc in zip(v, kcs)]
        a_v = [_dot(attn_scr[c, h].astype(BF16), v[h]) for h in heads]
        q_g = [(qq * e_g[:, kc]).astype(BF16) for qq, kc in zip(q, kcs)]
        pending.append((rows, q_g, a_v, v_k, [e_l[:, kc] for kc in kcs]))
    for rows, q_g, a_v, v_k, e_l in pending:
        st = [st_scr[h] for h in heads]
        q_s = [_dot_nt(qq, ss.astype(BF16)) for qq, ss in zip(q_g, st)]
        for h in heads:
            st_scr[h] = st[h] * e_l[h] + v_k[h]
            o = a_v[h] + q_s[h]
            o = o * lax.rsqrt(jnp.mean(o * o, axis=-1, keepdims=True) + EPS) * gn_ref[...]
            o = o * _silu(r_ref[0, rows, vcs[h]].astype(F32))
            o_ref[0, rows, vcs[h]] = o.astype(o_ref.dtype)

    @pl.when(t == pl.num_programs(1) - 1)
    def _():
        for h in range(GLA_HEADS):
            s_ref[0, h] = st_scr[h].T


def _gla_mix(proj, small, s0, gla_wg, gla_bg, gla_norm, lt):
    b, l, _ = proj.shape
    kw = GLA_HEADS * GLA_DK
    return pl.pallas_call(
        _gla_kernel,
        grid=(b, l // lt),
        in_specs=[
            pl.BlockSpec((1, lt, kw), lambda i, t: (i, t, 0)),
            pl.BlockSpec((1, lt, kw), lambda i, t: (i, t, 1)),
            pl.BlockSpec((1, lt, GLA_WIDTH), lambda i, t: (i, t, 1)),
            pl.BlockSpec((1, lt, GLA_WIDTH), lambda i, t: (i, t, 2)),
            pl.BlockSpec((1, lt, SMALL_COLS), lambda i, t: (i, t, 0)),
            pl.BlockSpec((GLA_GATE_RANK, kw), lambda i, t: (0, 0)),
            pl.BlockSpec((1, kw), lambda i, t: (0, 0)),
            pl.BlockSpec((1, GLA_DV), lambda i, t: (0, 0)),
            pl.BlockSpec((1, GLA_HEADS, GLA_DK, GLA_DV), lambda i, t: (i, 0, 0, 0)),
        ],
        out_specs=[
            pl.BlockSpec((1, lt, GLA_WIDTH), lambda i, t: (i, t, 0)),
            pl.BlockSpec((1, GLA_HEADS, GLA_DK, GLA_DV), lambda i, t: (i, 0, 0, 0)),
        ],
        out_shape=[
            jax.ShapeDtypeStruct((b, l, GLA_WIDTH), BF16),
            jax.ShapeDtypeStruct((b, GLA_HEADS, GLA_DK, GLA_DV), F32),
        ],
        scratch_shapes=[pltpu.VMEM((GLA_HEADS, GLA_DV, GLA_DK), F32),
                        pltpu.VMEM((lt // CHUNK, GLA_HEADS, CHUNK, CHUNK), F32)],
        compiler_params=_params("arbitrary", "arbitrary"),
        name="gla_mix",
    )(proj, proj, proj, proj, small, gla_wg, gla_bg.reshape(1, kw), gla_norm.reshape(1, GLA_DV), s0)


CONV_PAD = SUBLANES
INV_BASE_LOG2 = 3
GDN_CHUNK_GROUP = 4


def _inverse_masks():
    row = lax.broadcasted_iota(jnp.int32, (CHUNK, CHUNK), 0)
    col = lax.broadcasted_iota(jnp.int32, (CHUNK, CHUNK), 1)

    def same_block(log2):
        return jnp.right_shift(row, log2) == jnp.right_shift(col, log2)

    base = same_block(INV_BASE_LOG2)
    merges = []
    log2 = INV_BASE_LOG2
    while (1 << log2) < CHUNK:
        merges.append(same_block(log2 + 1) & jnp.logical_not(same_block(log2)))
        log2 += 1
    return base, merges


def _unit_lower_inverse_minus_eye(lower, base, merges):
    neg = [jnp.where(base, -l, 0.0) for l in lower]
    x = neg
    p = neg
    for _ in range(INV_BASE_LOG2 - 1):
        p = _dot_many(p, p)
        xp = _dot_many(x, p)
        x = [a + b + c for a, b, c in zip(x, p, xp)]
    for m in merges:
        c = [jnp.where(m, l, 0.0) for l in lower]
        w = [a + b for a, b in zip(c, _dot_many(x, c))]
        wx = _dot_many(w, x)
        x = [a - (b + d) for a, b, d in zip(x, w, wx)]
    return x


def _gdn_kernel(x_ref, g_ref, sm_ref, cw_ref, ga_ref, gn_ref, c0_ref, s0_ref,
                o_ref, s_ref, c_ref, xbuf):
    t = pl.program_id(1)
    lt = x_ref.shape[1]
    causal, strict = _tri_masks()
    tri = causal.astype(F32)
    inv_base, inv_merges = _inverse_masks()
    scale = GDN_DK ** -0.5

    @pl.when(t == 0)
    def _():
        xbuf[0:CONV_PAD, :] = jnp.zeros((CONV_PAD, GDN_CONV_DIM), F32)
        xbuf[CONV_PAD - (GDN_CONV - 1):CONV_PAD, :] = c0_ref[0]
        s_ref[...] = s0_ref[...]

    @pl.when(t > 0)
    def _():
        xbuf[0:CONV_PAD, :] = xbuf[lt:lt + CONV_PAD, :]

    xbuf[CONV_PAD:CONV_PAD + lt, :] = x_ref[0].astype(F32)
    c_ref[0] = xbuf[lt + CONV_PAD - (GDN_CONV - 1):lt + CONV_PAD, :]

    def conv_silu(r0, cols):
        acc = cw_ref[GDN_CONV - 1:GDN_CONV, cols] * xbuf[r0 + CONV_PAD:r0 + CONV_PAD + CHUNK, cols]
        for tap in range(GDN_CONV - 1):
            back = GDN_CONV - 1 - tap
            acc = acc + cw_ref[tap:tap + 1, cols] * xbuf[r0 + CONV_PAD - back:r0 + CONV_PAD - back + CHUNK, cols]
        return _silu(acc)

    def l2norm(x):
        return x * lax.rsqrt(jnp.sum(x * x, axis=-1, keepdims=True) + EPS)

    heads = range(GDN_HEADS)
    def prep_gates(c):
        sm = sm_ref[0, pl.ds(c * CHUNK, CHUNK), :]
        gate = -jnp.exp(ga_ref[0:1, :]) * _softplus(sm + ga_ref[1:2, :])
        big_g = _dot_f32(tri, gate)
        return jax.nn.sigmoid(sm), big_g, big_g.T

    def prep_head(c, gates, h):
        beta, big_g, big_gt = gates
        r0 = c * CHUNK
        g_col = big_g[:, ALPHA_OFF + h:ALPHA_OFF + h + 1]
        g_row = big_gt[ALPHA_OFF + h:ALPHA_OFF + h + 1, :]
        b_col = beta[:, BETA_OFF + h:BETA_OFF + h + 1]
        g_last = g_col[CHUNK - 1:CHUNK, :]
        e_g = jnp.exp(g_col)
        q = l2norm(conv_silu(r0, slice(h * GDN_DK, (h + 1) * GDN_DK))) * scale
        k = l2norm(conv_silu(r0, slice(GDN_WIDTH + h * GDN_DK, GDN_WIDTH + (h + 1) * GDN_DK)))
        v = conv_silu(r0, slice(2 * GDN_WIDTH + h * GDN_DV, 2 * GDN_WIDTH + (h + 1) * GDN_DV))
        kb = k * b_col
        return dict(
            decay=jnp.where(causal, jnp.exp(g_col - g_row), 0.0),
            rhs=jnp.concatenate([v * b_col, kb * e_g], axis=-1),
            q16=q.astype(BF16), k16=k.astype(BF16), kb16=kb.astype(BF16),
            q_g=(q * e_g).astype(BF16),
            k_dec=(k * jnp.exp(g_last - g_col)).astype(BF16),
            e_last=jnp.exp(g_last))

    def key_products(p):
        kk_t = [_dot_nt(ph["kb16"], ph["k16"]) for ph in p]
        qk_t = [_dot_nt(ph["q16"], ph["k16"]) for ph in p]
        lower = [jnp.where(strict, m * ph["decay"], 0.0) for m, ph in zip(kk_t, p)]
        attn = [jnp.where(causal, m * ph["decay"], 0.0).astype(BF16) for m, ph in zip(qk_t, p)]
        return lower, attn

    n_chunks = lt // CHUNK
    pending = []
    for c0 in range(0, n_chunks, GDN_CHUNK_GROUP):
        chunks = range(c0, min(c0 + GDN_CHUNK_GROUP, n_chunks))
        gates = [prep_gates(c) for c in chunks]
        p = [prep_head(c, g, h) for c, g in zip(chunks, gates) for h in heads]
        lower, attn = key_products(p)
        xinv = _unit_lower_inverse_minus_eye(lower, inv_base, inv_merges)
        rhs = [ph["rhs"] for ph in p]
        sol = [r + xr for r, xr in zip(rhs, _dot_many(xinv, rhs))]
        for n, c in enumerate(chunks):
            sl = slice(n * GDN_HEADS, (n + 1) * GDN_HEADS)
            pending.append((pl.ds(c * CHUNK, CHUNK), attn[sl],
                            [so[:, :GDN_DV] for so in sol[sl]],
                            [so[:, GDN_DV:].astype(BF16) for so in sol[sl]],
                            [ph["q_g"] for ph in p[sl]], [ph["k_dec"] for ph in p[sl]],
                            [ph["e_last"] for ph in p[sl]]))
    for rows, attn, sol_v, sol_k, q_g, k_dec, e_last in pending:
        s = [s_ref[0, h] for h in heads]
        s16 = [ss.astype(BF16) for ss in s]
        k_s = [_dot(a, ss) for a, ss in zip(sol_k, s16)]
        q_s = [_dot(a, ss) for a, ss in zip(q_g, s16)]
        u16 = [(sv - ks).astype(BF16) for sv, ks in zip(sol_v, k_s)]
        a_u = [_dot(a, uu) for a, uu in zip(attn, u16)]
        k_u = [_dot_tn(kd, uu) for kd, uu in zip(k_dec, u16)]
        for h in heads:
            hc = slice(h * GDN_DV, (h + 1) * GDN_DV)
            s_ref[0, h] = e_last[h] * s[h] + k_u[h]
            o = q_s[h] + a_u[h]
            o = o * lax.rsqrt(jnp.mean(o * o, axis=-1, keepdims=True) + EPS) * gn_ref[...]
            o = o * _silu(g_ref[0, rows, hc].astype(F32))
            o_ref[0, rows, hc] = o.astype(o_ref.dtype)


def _gdn_mix(proj, small, c0, s0, conv_w, gate_vec, gdn_norm, lt):
    b, l, _ = proj.shape
    return pl.pallas_call(
        _gdn_kernel,
        grid=(b, l // lt),
        in_specs=[
            pl.BlockSpec((1, lt, GDN_CONV_DIM), lambda i, t: (i, t, 1)),
            pl.BlockSpec((1, lt, GDN_WIDTH), lambda i, t: (i, t, 6)),
            pl.BlockSpec((1, lt, SMALL_COLS), lambda i, t: (i, t, 0)),
            pl.BlockSpec((GDN_CONV, GDN_CONV_DIM), lambda i, t: (0, 0)),
            pl.BlockSpec((2, SMALL_COLS), lambda i, t: (0, 0)),
            pl.BlockSpec((1, GDN_DV), lambda i, t: (0, 0)),
            pl.BlockSpec((1, GDN_CONV - 1, GDN_CONV_DIM), lambda i, t: (i, 0, 0)),
            pl.BlockSpec((1, GDN_HEADS, GDN_DK, GDN_DV), lambda i, t: (i, 0, 0, 0)),
        ],
        out_specs=[
            pl.BlockSpec((1, lt, GDN_WIDTH), lambda i, t: (i, t, 0)),
            pl.BlockSpec((1, GDN_HEADS, GDN_DK, GDN_DV), lambda i, t: (i, 0, 0, 0)),
            pl.BlockSpec((1, GDN_CONV - 1, GDN_CONV_DIM), lambda i, t: (i, 0, 0)),
        ],
        out_shape=[
            jax.ShapeDtypeStruct((b, l, GDN_WIDTH), BF16),
            jax.ShapeDtypeStruct((b, GDN_HEADS, GDN_DK, GDN_DV), F32),
            jax.ShapeDtypeStruct((b, GDN_CONV - 1, GDN_CONV_DIM), F32),
        ],
        scratch_shapes=[pltpu.VMEM((lt + CONV_PAD, GDN_CONV_DIM), F32)],
        compiler_params=_params("arbitrary", "arbitrary"),
        name="gdn_mix",
    )(proj, proj, small, conv_w, gate_vec, gdn_norm.reshape(1, GDN_DV), c0, s0)


def _out_proj_kernel(x_ref, oa_ref, ob_ref, w_ref, g1_ref, sc_ref, sh_ref, n2_ref, x1_ref, h2_ref,
                     *, n_sub):
    bt, lt, d = x_ref.shape
    if bt == 1:
        ls = lt // n_sub
        subs = [(slice(0, 1), slice(i * ls, (i + 1) * ls)) for i in range(n_sub)]
    else:
        bs = bt // n_sub
        subs = [(slice(i * bs, (i + 1) * bs), slice(0, lt)) for i in range(n_sub)]
    mixes = []
    for bsl, lsl in subs:
        oa = oa_ref[bsl, lsl, :]
        nb, nl, _ = oa.shape
        oa = oa.reshape(nb * nl, GLA_WIDTH)
        ob = ob_ref[bsl, lsl, :].reshape(nb * nl, GDN_WIDTH)
        mix = _dot(oa, w_ref[0:GLA_WIDTH, :]) + _dot(ob, w_ref[GLA_WIDTH:GLA_WIDTH + GDN_WIDTH, :])
        mixes.append(mix.reshape(nb, nl, d))
    for (bsl, lsl), mix in zip(subs, mixes):
        x1 = x_ref[bsl, lsl, :] + g1_ref[bsl] * mix
        x1_ref[bsl, lsl, :] = x1
        y = x1 * lax.rsqrt(jnp.mean(x1 * x1, axis=-1, keepdims=True) + EPS) * n2_ref[...]
        h2_ref[bsl, lsl, :] = (y * (1.0 + sc_ref[bsl]) + sh_ref[bsl]).astype(h2_ref.dtype)


def _out_proj(x, o_a, o_b, mod3, mod_row0, w_out16, norm2, bt, lt):
    b, l, d = x.shape
    mrow = mod_row0 // bt
    return pl.pallas_call(
        functools.partial(_out_proj_kernel, n_sub=2),
        grid=(b // bt, l // lt),
        in_specs=[
            pl.BlockSpec((bt, lt, d), lambda i, t: (i, t, 0)),
            pl.BlockSpec((bt, lt, GLA_WIDTH), lambda i, t: (i, t, 0)),
            pl.BlockSpec((bt, lt, GDN_WIDTH), lambda i, t: (i, t, 0)),
            pl.BlockSpec((GLA_WIDTH + GDN_WIDTH, d), lambda i, t: (0, 0)),
            pl.BlockSpec((bt, 1, d), lambda i, t: (mrow + i, 0, 2)),
            pl.BlockSpec((bt, 1, d), lambda i, t: (mrow + i, 0, 4)),
            pl.BlockSpec((bt, 1, d), lambda i, t: (mrow + i, 0, 3)),
            pl.BlockSpec((1, 1, d), lambda i, t: (0, 0, 0)),
        ],
        out_specs=[
            pl.BlockSpec((bt, lt, d), lambda i, t: (i, t, 0)),
            pl.BlockSpec((bt, lt, d), lambda i, t: (i, t, 0)),
        ],
        out_shape=[
            jax.ShapeDtypeStruct((b, l, d), F32),
            jax.ShapeDtypeStruct((b, l, d), BF16),
        ],
        compiler_params=_params("arbitrary", "arbitrary"),
        name="out_proj",
    )(x, o_a, o_b, w_out16, mod3, mod3, mod3, norm2.reshape(1, 1, d))


FFN_PAD = SUBLANES


def _ffn_kernel(h_ref, halo_ref, wg_ref, wv_ref, wd_ref, cw_ref, cb_ref, st_ref, x1_hbm, g2_ref,
                fn_ref, y_ref, so_ref, gbuf, x1_buf, x1_sem, *, has_halo, n_sub):
    ib = pl.program_id(0)
    t = pl.program_id(1)
    f = pl.program_id(2)
    bt, lt, d = h_ref.shape
    tf = wg_ref.shape[1]
    x1_copy = pltpu.make_async_copy(
        x1_hbm.at[pl.ds(ib * bt, bt), pl.ds(t * lt, lt), :], x1_buf, x1_sem)
    if bt == 1:
        ls = lt // n_sub
        subs = [(slice(0, 1), slice(i * ls, (i + 1) * ls)) for i in range(n_sub)]
    else:
        bs = bt // n_sub
        subs = [(slice(i * bs, (i + 1) * bs), slice(0, lt)) for i in range(n_sub)]

    @pl.when(f == 0)
    def _():
        x1_copy.start()
        y_ref[...] = jnp.zeros(y_ref.shape, F32)

    prev = st_ref[...]
    if has_halo:
        halo_gate = _dot(halo_ref[0], wg_ref[...])
        prev = jnp.where(t == 0, prev, halo_gate[SUBLANES - (FFN_CONV - 1):SUBLANES, :][None])
    gbuf[:, FFN_PAD - (FFN_CONV - 1):FFN_PAD, :] = prev
    gates, vals = [], []
    for bsl, lsl in subs:
        hb = h_ref[bsl, lsl, :]
        nb, nl, _ = hb.shape
        hb = hb.reshape(nb * nl, d)
        gate = _dot(hb, wg_ref[...]).reshape(nb, nl, tf)
        gbuf[bsl, FFN_PAD + lsl.start:FFN_PAD + lsl.stop, :] = gate
        gates.append(gate)
        vals.append(_dot(hb, wv_ref[...]))
    so_ref[...] = gbuf[:, FFN_PAD + lt - (FFN_CONV - 1):FFN_PAD + lt, :].reshape(so_ref.shape)
    acts = []
    for (bsl, lsl), gate, val in zip(subs, gates, vals):
        nb, nl, _ = gate.shape
        conv = cw_ref[FFN_CONV - 1:FFN_CONV, :] * gate
        for tap in range(FFN_CONV - 1):
            back = FFN_CONV - 1 - tap
            conv = conv + cw_ref[tap:tap + 1, :] * gbuf[bsl, FFN_PAD + lsl.start - back:FFN_PAD + lsl.stop - back, :]
        acts.append((_silu(conv + cb_ref[...]).reshape(nb * nl, tf) * val).astype(BF16))
    parts = [_dot(act, wd_ref[...]) for act in acts]
    for (bsl, lsl), gate, part in zip(subs, gates, parts):
        nb, nl, _ = gate.shape
        y_ref[bsl, lsl, :] += part.reshape(nb, nl, d)

    @pl.when(f == pl.num_programs(2) - 1)
    def _():
        x1_copy.wait()
        x2 = x1_buf[...] + g2_ref[...] * y_ref[...]
        y = x2 * lax.rsqrt(jnp.mean(x2 * x2, axis=-1, keepdims=True) + EPS) * fn_ref[...]
        y_ref[...] = y


def _ffn(h2, x1, st0, mod3, mod_row0, w_up16, w_down16, conv_w, conv_b, final_norm, bt, lt):
    b, l, d = h2.shape
    tf = 512
    nf = D_FF // tf
    mrow = mod_row0 // bt
    has_halo = lt < l
    halo_blocks = lt // SUBLANES
    kern = functools.partial(_ffn_kernel, has_halo=has_halo, n_sub=4)
    return pl.pallas_call(
        kern,
        grid=(b // bt, l // lt, nf),
        in_specs=[
            pl.BlockSpec((bt, lt, d), lambda i, t, f: (i, t, 0)),
            pl.BlockSpec((1, SUBLANES, d), lambda i, t, f: (i, jnp.maximum(t * halo_blocks - 1, 0), 0)),
            pl.BlockSpec((d, tf), lambda i, t, f: (0, f)),
            pl.BlockSpec((d, tf), lambda i, t, f: (0, nf + f)),
            pl.BlockSpec((tf, d), lambda i, t, f: (f, 0)),
            pl.BlockSpec((FFN_CONV, tf), lambda i, t, f: (0, f)),
            pl.BlockSpec((1, tf), lambda i, t, f: (0, f)),
            pl.BlockSpec((bt, FFN_CONV - 1, tf), lambda i, t, f: (i, 0, f)),
            pl.BlockSpec(memory_space=pl.ANY),
            pl.BlockSpec((bt, 1, d), lambda i, t, f: (mrow + i, 0, 5)),
            pl.BlockSpec((1, 1, d), lambda i, t, f: (0, 0, 0)),
        ],
        out_specs=[
            pl.BlockSpec((bt, lt, d), lambda i, t, f: (i, t, 0)),
            pl.BlockSpec((bt, 1, FFN_CONV - 1, tf), lambda i, t, f: (i, t, 0, f)),
        ],
        out_shape=[
            jax.ShapeDtypeStruct((b, l, d), F32),
            jax.ShapeDtypeStruct((b, l // lt, FFN_CONV - 1, D_FF), F32),
        ],
        scratch_shapes=[
            pltpu.VMEM((bt, lt + FFN_PAD, tf), F32),
            pltpu.VMEM((bt, lt, d), F32),
            pltpu.SemaphoreType.DMA(()),
        ],
        compiler_params=_params("arbitrary", "arbitrary", "arbitrary",
                                vmem_limit_bytes=FFN_VMEM_LIMIT_BYTES),
        name="ffn",
    )(h2, h2, w_up16, w_up16, w_down16, conv_w, conv_b.reshape(1, D_FF), st0, x1, mod3,
      final_norm.reshape(1, 1, d))


def _trunk(x, mod3, mod_row0, states, weights, cfg):
    (w_gla16, w_gdn16, w_small, norm1, gla_wg, gla_bg, gla_norm, gdn_conv_w, gate_vec, gdn_norm, w_out16,
     norm2, w_up16, w_down16, ffn_conv_w, ffn_conv_b, final_norm) = weights
    s_gla, s_gdn, s_conv, s_ffn = states
    proj, small = _in_proj(x, mod3, mod_row0, norm1, w_gla16, w_gdn16, w_small, cfg["in_bt"], cfg["in_lt"])
    o_a, n_gla = _gla_mix(proj, small, s_gla, gla_wg, gla_bg, gla_norm, cfg["gla_lt"])
    o_b, n_gdn, n_conv = _gdn_mix(proj, small, s_conv, s_gdn, gdn_conv_w, gate_vec, gdn_norm,
                                  cfg["gdn_lt"])
    x1, h2 = _out_proj(x, o_a, o_b, mod3, mod_row0, w_out16, norm2, cfg["out_bt"], cfg["out_lt"])
    y, n_ffn = _ffn(h2, x1, s_ffn, mod3, mod_row0, w_up16, w_down16, ffn_conv_w, ffn_conv_b,
                    final_norm, cfg["ffn_bt"], cfg["ffn_lt"])
    return y, n_gla[None], n_gdn[None], n_conv[None], n_ffn[:, -1][None]


def kernel(x_prompt, x_sample, c_prompt, c_sample, state_gla, state_gdn, state_gdn_conv, state_ffn_conv, w_ada, b_ada, norm1, w_in, gla_wg, gla_bg, gla_norm, gdn_conv_w, gdn_a_log, gdn_dt_bias, gdn_norm, w_out, norm2, w_up, ffn_conv_w, ffn_conv_b, w_down, final_norm):
    bp = x_prompt.shape[0]
    bs = x_sample.shape[0]

    w_in16 = w_in[0].astype(BF16)
    w_gla16 = w_in16
    w_gdn16 = w_in16[:, IN_OFF_GDN:IN_OFF_BETA]
    w_small = jnp.concatenate(
        [w_in16[:, IN_OFF_LR:IN_OFF_GDN], w_in16[:, IN_OFF_BETA:],
         jnp.zeros((D_MODEL, SMALL_COLS - GLA_GATE_RANK - 2 * GDN_HEADS), BF16)], axis=1)
    w_out16 = w_out[0].astype(BF16)
    w_up16 = w_up[0].astype(BF16)
    w_down16 = w_down[0].astype(BF16)
    gate_vec = jnp.zeros((2, SMALL_COLS), F32)
    gate_vec = gate_vec.at[0, ALPHA_OFF:ALPHA_OFF + GDN_HEADS].set(gdn_a_log[0])
    gate_vec = gate_vec.at[1, ALPHA_OFF:ALPHA_OFF + GDN_HEADS].set(gdn_dt_bias[0])

    c_all = jnp.concatenate([c_sample, c_prompt], axis=0)
    mod = _ada_mod(c_all, w_ada[0], b_ada[0])
    mod3 = mod.reshape(bs + bp, 1, N_MOD * D_MODEL)

    weights = (w_gla16, w_gdn16, w_small, norm1[0], gla_wg[0], gla_bg[0], gla_norm[0], gdn_conv_w[0], gate_vec,
               gdn_norm[0], w_out16, norm2[0], w_up16, w_down16, ffn_conv_w[0], ffn_conv_b[0],
               final_norm)

    fresh = (jnp.zeros((bp, GLA_HEADS, GLA_DK, GLA_DV), F32),
             jnp.zeros((bp, GDN_HEADS, GDN_DK, GDN_DV), F32),
             jnp.zeros((bp, GDN_CONV - 1, GDN_CONV_DIM), F32),
             jnp.zeros((bp, FFN_CONV - 1, D_FF), F32))
    cfg_p = dict(in_bt=1, in_lt=1024, gla_lt=512, gdn_lt=256, out_bt=1, out_lt=512, ffn_bt=1, ffn_lt=1024)
    y_p, p_gla, p_gdn, p_conv, p_ffn = _trunk(x_prompt, mod3, bs, fresh, weights, cfg_p)

    carried = (state_gla[0], state_gdn[0], state_gdn_conv[0], state_ffn_conv[0])
    ls = x_sample.shape[1]
    cfg_s = dict(in_bt=bs, in_lt=ls, gla_lt=ls, gdn_lt=ls, out_bt=bs // 2, out_lt=ls, ffn_bt=bs, ffn_lt=ls)
    y_s, s_gla, s_gdn, s_conv, s_ffn = _trunk(x_sample, mod3, 0, carried, weights, cfg_s)
    return (y_p, y_s, p_gla, p_gdn, p_conv, p_ffn, s_gla, s_gdn, s_conv, s_ffn)
```

```python
import functools

import jax
import jax.numpy as jnp
from jax import lax
from jax.experimental import pallas as pl
from jax.experimental.pallas import tpu as pltpu

F32 = jnp.float32
BF16 = jnp.bfloat16

D_MODEL = 2048
CHUNK = 64
GLA_HEADS = 4
GLA_DK = 128
GLA_DV = 256
GLA_WIDTH = GLA_HEADS * GLA_DV
GLA_GATE_RANK = 16
GLA_GATE_NORM = 16.0
GLA_MILD_LOG_DECAY = 60.0
GDN_HEADS = 8
GDN_DK = 128
GDN_DV = 128
GDN_WIDTH = GDN_HEADS * GDN_DV
GDN_CONV = 4
GDN_CONV_DIM = 3 * GDN_WIDTH
D_FF = 5632
FFN_CONV = 3
N_MOD = 6
EPS = 1e-6

GLA_COLS = 2 * GLA_HEADS * GLA_DK + 2 * GLA_WIDTH
GDN_COLS = 4 * GDN_WIDTH
MAIN_COLS = GLA_COLS + GDN_COLS
SMALL_COLS = 128
LR_OFF, BETA_OFF, ALPHA_OFF = 0, GLA_GATE_RANK, GLA_GATE_RANK + GDN_HEADS
IN_OFF_LR = GLA_COLS
IN_OFF_GDN = IN_OFF_LR + GLA_GATE_RANK
IN_OFF_BETA = IN_OFF_GDN + GDN_COLS

SUBLANES = 8
VMEM_LIMIT_BYTES = 56 * 1024 * 1024
FFN_VMEM_LIMIT_BYTES = 60 * 1024 * 1024

NT_DIMS = (((1,), (1,)), ((), ()))
TN_DIMS = (((0,), (0,)), ((), ()))


def _dot(a, b):
    return jnp.dot(a, b, preferred_element_type=F32)


def _dot_nt(a, b):
    return lax.dot_general(a, b, NT_DIMS, preferred_element_type=F32)


def _dot_tn(a, b):
    return lax.dot_general(a, b, TN_DIMS, preferred_element_type=F32)


def _dot_f32(a, b):
    return jnp.dot(a, b, preferred_element_type=F32, precision=lax.Precision.HIGHEST)


def _dot_many(a_list, b_list):
    return [_dot(a.astype(BF16), b.astype(BF16)) for a, b in zip(a_list, b_list)]


def _silu(x):
    return x * jax.nn.sigmoid(x)


def _softplus(x):
    return jnp.maximum(x, 0.0) + jnp.log1p(jnp.exp(-jnp.abs(x)))


def _log_sigmoid(x):
    return jnp.minimum(x, 0.0) - jnp.log1p(jnp.exp(-jnp.abs(x)))


def _params(*semantics, vmem_limit_bytes=VMEM_LIMIT_BYTES):
    return pltpu.CompilerParams(dimension_semantics=semantics, vmem_limit_bytes=vmem_limit_bytes)


def _tri_masks():
    row = lax.broadcasted_iota(jnp.int32, (CHUNK, CHUNK), 0)
    col = lax.broadcasted_iota(jnp.int32, (CHUNK, CHUNK), 1)
    return row >= col, row > col


def _ada_kernel(c_ref, w_ref, b_ref, o_ref):
    s = _silu(c_ref[...]).astype(BF16)
    o_ref[...] = _dot(s, w_ref[...].astype(BF16)) + b_ref[...]


def _ada_mod(c_all, w_ada, b_ada):
    rows = c_all.shape[0]
    n = w_ada.shape[1]
    tn = 1024
    return pl.pallas_call(
        _ada_kernel,
        grid=(n // tn,),
        in_specs=[
            pl.BlockSpec((rows, D_MODEL), lambda j: (0, 0)),
            pl.BlockSpec((D_MODEL, tn), lambda j: (0, j)),
            pl.BlockSpec((1, tn), lambda j: (0, j)),
        ],
        out_specs=pl.BlockSpec((rows, tn), lambda j: (0, j)),
        out_shape=jax.ShapeDtypeStruct((rows, n), F32),
        compiler_params=_params("arbitrary"),
        name="ada_mod",
    )(c_all, w_ada, b_ada.reshape(1, n))


def _in_proj_kernel(x_ref, sc_ref, sh_ref, n1_ref, wa_ref, wb_ref, ws_ref, o_ref, os_ref, h_scr,
                    *, n_a):
    bt, lt, d = x_ref.shape
    j = pl.program_id(2)

    @pl.when(j == 0)
    def _():
        x = x_ref[...]
        y = x * lax.rsqrt(jnp.mean(x * x, axis=-1, keepdims=True) + EPS) * n1_ref[...]
        h = y * (1.0 + sc_ref[...]) + sh_ref[...]
        hb = h.reshape(bt * lt, d).astype(BF16)
        h_scr[...] = hb
        os_ref[...] = _dot(hb, ws_ref[...]).reshape(os_ref.shape)

    @pl.when(j < n_a)
    def _():
        o_ref[...] = _dot(h_scr[...], wa_ref[...]).reshape(o_ref.shape)

    @pl.when(j >= n_a)
    def _():
        o_ref[...] = _dot(h_scr[...], wb_ref[...]).reshape(o_ref.shape)


def _in_proj(x, mod3, mod_row0, norm1, w_gla16, w_gdn16, w_small, bt, lt):
    b, l, d = x.shape
    tn = 1024
    n_a = GLA_COLS // tn
    mrow = mod_row0 // bt
    grid = (b // bt, l // lt, MAIN_COLS // tn)
    return pl.pallas_call(
        functools.partial(_in_proj_kernel, n_a=n_a),
        grid=grid,
        in_specs=[
            pl.BlockSpec((bt, lt, d), lambda i, t, j: (i, t, 0)),
            pl.BlockSpec((bt, 1, d), lambda i, t, j: (mrow + i, 0, 1)),
            pl.BlockSpec((bt, 1, d), lambda i, t, j: (mrow + i, 0, 0)),
            pl.BlockSpec((1, 1, d), lambda i, t, j: (0, 0, 0)),
            pl.BlockSpec((d, tn), lambda i, t, j: (0, jnp.minimum(j, n_a - 1))),
            pl.BlockSpec((d, tn), lambda i, t, j: (0, jnp.maximum(j - n_a, 0))),
            pl.BlockSpec((d, SMALL_COLS), lambda i, t, j: (0, 0)),
        ],
        out_specs=[
            pl.BlockSpec((bt, lt, tn), lambda i, t, j: (i, t, j)),
            pl.BlockSpec((bt, lt, SMALL_COLS), lambda i, t, j: (i, t, 0)),
        ],
        out_shape=[
            jax.ShapeDtypeStruct((b, l, MAIN_COLS), F32),
            jax.ShapeDtypeStruct((b, l, SMALL_COLS), F32),
        ],
        scratch_shapes=[pltpu.VMEM((bt * lt, d), BF16)],
        compiler_params=_params("arbitrary", "arbitrary", "arbitrary"),
        name="in_proj",
    )(x, mod3, mod3, norm1.reshape(1, 1, d), w_gla16, w_gdn16, w_small)


def _gla_levels():
    row = lax.broadcasted_iota(jnp.int32, (CHUNK, CHUNK), 0)
    col = lax.broadcasted_iota(jnp.int32, (CHUNK, CHUNK), 1)
    ops, masks = [], []
    half = CHUNK // 2
    while half >= 1:
        shift = half.bit_length()
        parent_row = jnp.left_shift(jnp.right_shift(row, shift), shift)
        parent_col = jnp.left_shift(jnp.right_shift(col, shift), shift)
        m_row = parent_row + (half - 1)
        right_row = row > m_row
        in_span = (right_row & (col > m_row) & (col <= row)) | (
            jnp.logical_not(right_row) & (col > row) & (col <= m_row))
        ops.append(jnp.where(in_span, 1.0, 0.0))
        masks.append((parent_row == parent_col) & right_row & (col <= m_row))
        half //= 2
    return ops, masks, row == col


def _gla_kernel(q_ref, k_ref, v_ref, r_ref, sm_ref, wg_ref, bg_ref, gn_ref, s0_ref,
                o_ref, s_ref, st_scr, attn_scr):
    t = pl.program_id(1)
    bt, lt = q_ref.shape[0], q_ref.shape[1]
    n_chunks = lt // CHUNK
    causal, _ = _tri_masks()
    tri = causal.astype(F32)
    scale = GLA_DK ** -0.5

    @pl.when(t == 0)
    def _():
        for b in range(bt):
            for h in range(GLA_HEADS):
                st_scr[b, h] = s0_ref[b, h].T

    wg = wg_ref[...].astype(BF16)
    heads = range(GLA_HEADS)
    kcs = [slice(h * GLA_DK, (h + 1) * GLA_DK) for h in heads]
    vcs = [slice(h * GLA_DV, (h + 1) * GLA_DV) for h in heads]
    items = [(b, pl.ds(c * CHUNK, CHUNK)) for c in range(n_chunks) for b in range(bt)]

    def scaled_q(b, rows):
        return [q_ref[b, rows, kc] * scale for kc in kcs]

    log_a, big_g = [], []
    for b, rows in items:
        a_lr = sm_ref[b, rows, LR_OFF:LR_OFF + GLA_GATE_RANK].astype(BF16)
        la = _log_sigmoid(_dot(a_lr, wg) + bg_ref[...]) / GLA_GATE_NORM
        log_a.append(la)
        big_g.append(_dot_f32(tri, la))
    total = big_g[0][CHUNK - 1:CHUNK, :]
    for g in big_g[1:]:
        total = jnp.minimum(total, g[CHUNK - 1:CHUNK, :])
    mild = jnp.min(total) > -GLA_MILD_LOG_DECAY

    @pl.when(mild)
    def _():
        for n, (b, rows) in enumerate(items):
            g_mid = big_g[n][CHUNK // 2 - 1:CHUNK // 2, :]
            e_q = jnp.exp(big_g[n] - g_mid)
            e_k = jnp.exp(g_mid - big_g[n])
            q = scaled_q(b, rows)
            scores = [_dot_nt((qq * e_q[:, kc]).astype(BF16),
                              (k_ref[b, rows, kc] * e_k[:, kc]).astype(BF16))
                      for qq, kc in zip(q, kcs)]
            for h in heads:
                attn_scr[n, h] = jnp.where(causal, scores[h], 0.0)

    @pl.when(jnp.logical_not(mild))
    def _():
        level_ops, level_masks, diag = _gla_levels()
        level_op = jnp.concatenate(level_ops, axis=0)
        for n, (b, rows) in enumerate(items):
            sums = _dot_f32(level_op, log_a[n])
            q = scaled_q(b, rows)
            k = [k_ref[b, rows, kc] for kc in kcs]
            scores = [jnp.where(diag, _dot_nt(qq.astype(BF16), kk.astype(BF16)), 0.0)
                      for qq, kk in zip(q, k)]
            for i, mask in enumerate(level_masks):
                e = jnp.exp(sums[i * CHUNK:(i + 1) * CHUNK])
                part = [_dot_nt((qq * e[:, kc]).astype(BF16), (kk * e[:, kc]).astype(BF16))
                        for qq, kk, kc in zip(q, k, kcs)]
                scores = [jnp.where(mask, p, a) for p, a in zip(part, scores)]
            for h in heads:
                attn_scr[n, h] = scores[h]

    pending = []
    for n, (b, rows) in enumerate(items):
        g_last = big_g[n][CHUNK - 1:CHUNK, :]
        e_g = jnp.exp(big_g[n])
        e_kl = jnp.exp(g_last - big_g[n])
        e_l = jnp.exp(g_last)
        q = scaled_q(b, rows)
        v = [v_ref[b, rows, vc].astype(BF16) for vc in vcs]
        v_k = [_dot_tn(vv, (k_ref[b, rows, kc] * e_kl[:, kc]).astype(BF16)) for vv, kc in zip(v, kcs)]
        a_v = [_dot(attn_scr[n, h].astype(BF16), v[h]) for h in heads]
        q_g = [(qq * e_g[:, kc]).astype(BF16) for qq, kc in zip(q, kcs)]
        pending.append((q_g, a_v, v_k, [e_l[:, kc] for kc in kcs]))
    for c in range(n_chunks):
        group = range(c * bt, (c + 1) * bt)
        st = [[st_scr[items[n][0], h] for h in heads] for n in group]
        q_s = [[_dot_nt(qq, ss.astype(BF16)) for qq, ss in zip(pending[n][0], st_n)]
               for n, st_n in zip(group, st)]
        for n, st_n, q_s_n in zip(group, st, q_s):
            b, rows = items[n]
            _, a_v, v_k, e_l = pending[n]
            for h in heads:
                st_scr[b, h] = st_n[h] * e_l[h] + v_k[h]
                o = a_v[h] + q_s_n[h]
                o = o * lax.rsqrt(jnp.mean(o * o, axis=-1, keepdims=True) + EPS) * gn_ref[...]
                o = o * _silu(r_ref[b, rows, vcs[h]])
                o_ref[b, rows, vcs[h]] = o.astype(o_ref.dtype)

    @pl.when(t == pl.num_programs(1) - 1)
    def _():
        for b in range(bt):
            for h in range(GLA_HEADS):
                s_ref[b, h] = st_scr[b, h].T


def _gla_mix(proj, small, s0, gla_wg, gla_bg, gla_norm, bt, lt):
    b, l, _ = proj.shape
    kw = GLA_HEADS * GLA_DK
    return pl.pallas_call(
        _gla_kernel,
        grid=(b // bt, l // lt),
        in_specs=[
            pl.BlockSpec((bt, lt, kw), lambda i, t: (i, t, 0)),
            pl.BlockSpec((bt, lt, kw), lambda i, t: (i, t, 1)),
            pl.BlockSpec((bt, lt, GLA_WIDTH), lambda i, t: (i, t, 1)),
            pl.BlockSpec((bt, lt, GLA_WIDTH), lambda i, t: (i, t, 2)),
            pl.BlockSpec((bt, lt, SMALL_COLS), lambda i, t: (i, t, 0)),
            pl.BlockSpec((GLA_GATE_RANK, kw), lambda i, t: (0, 0)),
            pl.BlockSpec((1, kw), lambda i, t: (0, 0)),
            pl.BlockSpec((1, GLA_DV), lambda i, t: (0, 0)),
            pl.BlockSpec((bt, GLA_HEADS, GLA_DK, GLA_DV), lambda i, t: (i, 0, 0, 0)),
        ],
        out_specs=[
            pl.BlockSpec((bt, lt, GLA_WIDTH), lambda i, t: (i, t, 0)),
            pl.BlockSpec((bt, GLA_HEADS, GLA_DK, GLA_DV), lambda i, t: (i, 0, 0, 0)),
        ],
        out_shape=[
            jax.ShapeDtypeStruct((b, l, GLA_WIDTH), BF16),
            jax.ShapeDtypeStruct((b, GLA_HEADS, GLA_DK, GLA_DV), F32),
        ],
        scratch_shapes=[pltpu.VMEM((bt, GLA_HEADS, GLA_DV, GLA_DK), F32),
                        pltpu.VMEM((bt * (lt // CHUNK), GLA_HEADS, CHUNK, CHUNK), F32)],
        compiler_params=_params("arbitrary", "arbitrary"),
        name="gla_mix",
    )(proj, proj, proj, proj, small, gla_wg, gla_bg.reshape(1, kw), gla_norm.reshape(1, GLA_DV), s0)


CONV_PAD = SUBLANES
INV_BASE_LOG2 = 3
GDN_CHUNK_GROUP = 4


def _inverse_masks():
    row = lax.broadcasted_iota(jnp.int32, (CHUNK, CHUNK), 0)
    col = lax.broadcasted_iota(jnp.int32, (CHUNK, CHUNK), 1)

    def same_block(log2):
        return jnp.right_shift(row, log2) == jnp.right_shift(col, log2)

    base = same_block(INV_BASE_LOG2)
    merges = []
    log2 = INV_BASE_LOG2
    while (1 << log2) < CHUNK:
        merges.append(same_block(log2 + 1) & jnp.logical_not(same_block(log2)))
        log2 += 1
    return base, merges


def _unit_lower_inverse_minus_eye(lower, base, merges):
    neg = [jnp.where(base, -l, 0.0) for l in lower]
    x = neg
    p = neg
    for _ in range(INV_BASE_LOG2 - 1):
        p = _dot_many(p, p)
        xp = _dot_many(x, p)
        x = [a + b + c for a, b, c in zip(x, p, xp)]
    for m in merges:
        c = [jnp.where(m, l, 0.0) for l in lower]
        w = [a + b for a, b in zip(c, _dot_many(x, c))]
        wx = _dot_many(w, x)
        x = [a - (b + d) for a, b, d in zip(x, w, wx)]
    return x


def _gdn_kernel(x_ref, g_ref, sm_ref, cw_ref, ga_ref, gn_ref, c0_ref, s0_ref,
                o_ref, s_ref, c_ref, xbuf):
    t = pl.program_id(1)
    bt, lt = x_ref.shape[0], x_ref.shape[1]
    causal, strict = _tri_masks()
    tri = causal.astype(F32)
    inv_base, inv_merges = _inverse_masks()
    scale = GDN_DK ** -0.5

    @pl.when(t == 0)
    def _():
        xbuf[:, 0:CONV_PAD, :] = jnp.zeros((bt, CONV_PAD, GDN_CONV_DIM), F32)
        xbuf[:, CONV_PAD - (GDN_CONV - 1):CONV_PAD, :] = c0_ref[...]
        s_ref[...] = s0_ref[...]

    @pl.when(t > 0)
    def _():
        xbuf[:, 0:CONV_PAD, :] = xbuf[:, lt:lt + CONV_PAD, :]

    xbuf[:, CONV_PAD:CONV_PAD + lt, :] = x_ref[...]
    c_ref[...] = xbuf[:, lt + CONV_PAD - (GDN_CONV - 1):lt + CONV_PAD, :]

    def conv_silu(b, r0, cols):
        acc = cw_ref[GDN_CONV - 1:GDN_CONV, cols] * xbuf[b, r0 + CONV_PAD:r0 + CONV_PAD + CHUNK, cols]
        for tap in range(GDN_CONV - 1):
            start = r0 + CONV_PAD - (GDN_CONV - 1 - tap)
            acc = acc + cw_ref[tap:tap + 1, cols] * xbuf[b, start:start + CHUNK, cols]
        return _silu(acc)

    def l2norm(x):
        return x * lax.rsqrt(jnp.sum(x * x, axis=-1, keepdims=True) + EPS)

    heads = range(GDN_HEADS)
    def prep_gates(b, c):
        sm = sm_ref[b, pl.ds(c * CHUNK, CHUNK), :]
        gate = -jnp.exp(ga_ref[0:1, :]) * _softplus(sm + ga_ref[1:2, :])
        big_g = _dot_f32(tri, gate)
        return jax.nn.sigmoid(sm), big_g, big_g.T

    def prep_head(b, c, gates, h):
        beta, big_g, big_gt = gates
        r0 = c * CHUNK
        g_col = big_g[:, ALPHA_OFF + h:ALPHA_OFF + h + 1]
        g_row = big_gt[ALPHA_OFF + h:ALPHA_OFF + h + 1, :]
        b_col = beta[:, BETA_OFF + h:BETA_OFF + h + 1]
        g_last = g_col[CHUNK - 1:CHUNK, :]
        e_g = jnp.exp(g_col)
        q = l2norm(conv_silu(b, r0, slice(h * GDN_DK, (h + 1) * GDN_DK))) * scale
        k = l2norm(conv_silu(b, r0, slice(GDN_WIDTH + h * GDN_DK, GDN_WIDTH + (h + 1) * GDN_DK)))
        v = conv_silu(b, r0, slice(2 * GDN_WIDTH + h * GDN_DV, 2 * GDN_WIDTH + (h + 1) * GDN_DV))
        kb = k * b_col
        return dict(
            decay=jnp.where(causal, jnp.exp(g_col - g_row), 0.0),
            rhs=jnp.concatenate([v * b_col, kb * e_g], axis=-1),
            q16=q.astype(BF16), k16=k.astype(BF16), kb16=kb.astype(BF16),
            q_g=(q * e_g).astype(BF16),
            k_dec=(k * jnp.exp(g_last - g_col)).astype(BF16),
            e_last=jnp.exp(g_last))

    def key_products(p):
        kk_t = [_dot_nt(ph["kb16"], ph["k16"]) for ph in p]
        qk_t = [_dot_nt(ph["q16"], ph["k16"]) for ph in p]
        lower = [jnp.where(strict, m * ph["decay"], 0.0) for m, ph in zip(kk_t, p)]
        attn = [jnp.where(causal, m * ph["decay"], 0.0).astype(BF16) for m, ph in zip(qk_t, p)]
        return lower, attn

    n_chunks = lt // CHUNK
    items = [(b, c) for c in range(n_chunks) for b in range(bt)]
    pending = []
    for i0 in range(0, len(items), GDN_CHUNK_GROUP):
        group = items[i0:i0 + GDN_CHUNK_GROUP]
        gates = [prep_gates(b, c) for b, c in group]
        p = [prep_head(b, c, g, h) for (b, c), g in zip(group, gates) for h in heads]
        lower, attn = key_products(p)
        xinv = _unit_lower_inverse_minus_eye(lower, inv_base, inv_merges)
        rhs = [ph["rhs"] for ph in p]
        sol = [r + xr for r, xr in zip(rhs, _dot_many(xinv, rhs))]
        for n in range(len(group)):
            sl = slice(n * GDN_HEADS, (n + 1) * GDN_HEADS)
            pending.append((attn[sl], [so[:, :GDN_DV] for so in sol[sl]],
                            [so[:, GDN_DV:].astype(BF16) for so in sol[sl]],
                            [ph["q_g"] for ph in p[sl]], [ph["k_dec"] for ph in p[sl]],
                            [ph["e_last"] for ph in p[sl]]))
    for c in range(n_chunks):
        group = range(c * bt, (c + 1) * bt)
        attn, sol_v, sol_k, q_g, k_dec, e_last = (
            [x for n in group for x in pending[n][field]] for field in range(6))
        where = [(items[n][0], h) for n in group for h in heads]
        s = [s_ref[b, h] for b, h in where]
        s16 = [ss.astype(BF16) for ss in s]
        k_s = [_dot(a, ss) for a, ss in zip(sol_k, s16)]
        q_s = [_dot(a, ss) for a, ss in zip(q_g, s16)]
        u16 = [(sv - ks).astype(BF16) for sv, ks in zip(sol_v, k_s)]
        a_u = [_dot(a, uu) for a, uu in zip(attn, u16)]
        k_u = [_dot_tn(kd, uu) for kd, uu in zip(k_dec, u16)]
        rows = pl.ds(c * CHUNK, CHUNK)
        for i, (b, h) in enumerate(where):
            hc = slice(h * GDN_DV, (h + 1) * GDN_DV)
            s_ref[b, h] = e_last[i] * s[i] + k_u[i]
            o = q_s[i] + a_u[i]
            o = o * lax.rsqrt(jnp.mean(o * o, axis=-1, keepdims=True) + EPS) * gn_ref[...]
            o = o * _silu(g_ref[b, rows, hc])
            o_ref[b, rows, hc] = o.astype(o_ref.dtype)


def _gdn_mix(proj, small, c0, s0, conv_w, gate_vec, gdn_norm, bt, lt):
    b, l, _ = proj.shape
    return pl.pallas_call(
        _gdn_kernel,
        grid=(b // bt, l // lt),
        in_specs=[
            pl.BlockSpec((bt, lt, GDN_CONV_DIM), lambda i, t: (i, t, 1)),
            pl.BlockSpec((bt, lt, GDN_WIDTH), lambda i, t: (i, t, 6)),
            pl.BlockSpec((bt, lt, SMALL_COLS), lambda i, t: (i, t, 0)),
            pl.BlockSpec((GDN_CONV, GDN_CONV_DIM), lambda i, t: (0, 0)),
            pl.BlockSpec((2, SMALL_COLS), lambda i, t: (0, 0)),
            pl.BlockSpec((1, GDN_DV), lambda i, t: (0, 0)),
            pl.BlockSpec((bt, GDN_CONV - 1, GDN_CONV_DIM), lambda i, t: (i, 0, 0)),
            pl.BlockSpec((bt, GDN_HEADS, GDN_DK, GDN_DV), lambda i, t: (i, 0, 0, 0)),
        ],
        out_specs=[
            pl.BlockSpec((bt, lt, GDN_WIDTH), lambda i, t: (i, t, 0)),
            pl.BlockSpec((bt, GDN_HEADS, GDN_DK, GDN_DV), lambda i, t: (i, 0, 0, 0)),
            pl.BlockSpec((bt, GDN_CONV - 1, GDN_CONV_DIM), lambda i, t: (i, 0, 0)),
        ],
        out_shape=[
            jax.ShapeDtypeStruct((b, l, GDN_WIDTH), BF16),
            jax.ShapeDtypeStruct((b, GDN_HEADS, GDN_DK, GDN_DV), F32),
            jax.ShapeDtypeStruct((b, GDN_CONV - 1, GDN_CONV_DIM), F32),
        ],
        scratch_shapes=[pltpu.VMEM((bt, lt + CONV_PAD, GDN_CONV_DIM), F32)],
        compiler_params=_params("arbitrary", "arbitrary"),
        name="gdn_mix",
    )(proj, proj, small, conv_w, gate_vec, gdn_norm.reshape(1, GDN_DV), c0, s0)


def _out_proj_kernel(x_ref, oa_ref, ob_ref, w_ref, g1_ref, sc_ref, sh_ref, n2_ref, x1_ref, h2_ref,
                     *, n_sub):
    bt, lt, d = x_ref.shape
    if bt == 1:
        ls = lt // n_sub
        subs = [(slice(0, 1), slice(i * ls, (i + 1) * ls)) for i in range(n_sub)]
    else:
        bs = bt // n_sub
        subs = [(slice(i * bs, (i + 1) * bs), slice(0, lt)) for i in range(n_sub)]
    mixes = []
    for bsl, lsl in subs:
        oa = oa_ref[bsl, lsl, :]
        nb, nl, _ = oa.shape
        oa = oa.reshape(nb * nl, GLA_WIDTH)
        ob = ob_ref[bsl, lsl, :].reshape(nb * nl, GDN_WIDTH)
        mix = _dot(oa, w_ref[0:GLA_WIDTH, :]) + _dot(ob, w_ref[GLA_WIDTH:GLA_WIDTH + GDN_WIDTH, :])
        mixes.append(mix.reshape(nb, nl, d))
    for (bsl, lsl), mix in zip(subs, mixes):
        x1 = x_ref[bsl, lsl, :] + g1_ref[bsl] * mix
        x1_ref[bsl, lsl, :] = x1
        y = x1 * lax.rsqrt(jnp.mean(x1 * x1, axis=-1, keepdims=True) + EPS) * n2_ref[...]
        h2_ref[bsl, lsl, :] = (y * (1.0 + sc_ref[bsl]) + sh_ref[bsl]).astype(h2_ref.dtype)


def _out_proj(x, o_a, o_b, mod3, mod_row0, w_out16, norm2, bt, lt):
    b, l, d = x.shape
    mrow = mod_row0 // bt
    return pl.pallas_call(
        functools.partial(_out_proj_kernel, n_sub=2),
        grid=(b // bt, l // lt),
        in_specs=[
            pl.BlockSpec((bt, lt, d), lambda i, t: (i, t, 0)),
            pl.BlockSpec((bt, lt, GLA_WIDTH), lambda i, t: (i, t, 0)),
            pl.BlockSpec((bt, lt, GDN_WIDTH), lambda i, t: (i, t, 0)),
            pl.BlockSpec((GLA_WIDTH + GDN_WIDTH, d), lambda i, t: (0, 0)),
            pl.BlockSpec((bt, 1, d), lambda i, t: (mrow + i, 0, 2)),
            pl.BlockSpec((bt, 1, d), lambda i, t: (mrow + i, 0, 4)),
            pl.BlockSpec((bt, 1, d), lambda i, t: (mrow + i, 0, 3)),
            pl.BlockSpec((1, 1, d), lambda i, t: (0, 0, 0)),
        ],
        out_specs=[
            pl.BlockSpec((bt, lt, d), lambda i, t: (i, t, 0)),
            pl.BlockSpec((bt, lt, d), lambda i, t: (i, t, 0)),
        ],
        out_shape=[
            jax.ShapeDtypeStruct((b, l, d), F32),
            jax.ShapeDtypeStruct((b, l, d), BF16),
        ],
        compiler_params=_params("arbitrary", "arbitrary"),
        name="out_proj",
    )(x, o_a, o_b, w_out16, mod3, mod3, mod3, norm2.reshape(1, 1, d))


FFN_PAD = SUBLANES


def _ffn_kernel(h_ref, halo_ref, wg_ref, wv_ref, wd_ref, cw_ref, cb_ref, st_ref, x1_hbm, g2_ref,
                fn_ref, y_ref, so_ref, gbuf, x1_buf, x1_sem, *, has_halo, n_sub):
    ib = pl.program_id(0)
    t = pl.program_id(1)
    f = pl.program_id(2)
    bt, lt, d = h_ref.shape
    tf = wg_ref.shape[1]
    x1_copy = pltpu.make_async_copy(
        x1_hbm.at[pl.ds(ib * bt, bt), pl.ds(t * lt, lt), :], x1_buf, x1_sem)
    if bt == 1:
        ls = lt // n_sub
        subs = [(slice(0, 1), slice(i * ls, (i + 1) * ls)) for i in range(n_sub)]
    else:
        bs = bt // n_sub
        subs = [(slice(i * bs, (i + 1) * bs), slice(0, lt)) for i in range(n_sub)]

    @pl.when(f == 0)
    def _():
        x1_copy.start()
        y_ref[...] = jnp.zeros(y_ref.shape, F32)

    prev = st_ref[...]
    if has_halo:
        halo_gate = _dot(halo_ref[0], wg_ref[...])
        prev = jnp.where(t == 0, prev, halo_gate[SUBLANES - (FFN_CONV - 1):SUBLANES, :][None])
    gbuf[:, FFN_PAD - (FFN_CONV - 1):FFN_PAD, :] = prev
    gates, vals = [], []
    for bsl, lsl in subs:
        hb = h_ref[bsl, lsl, :]
        nb, nl, _ = hb.shape
        hb = hb.reshape(nb * nl, d)
        gate = _dot(hb, wg_ref[...]).reshape(nb, nl, tf)
        gbuf[bsl, FFN_PAD + lsl.start:FFN_PAD + lsl.stop, :] = gate
        gates.append(gate)
        vals.append(_dot(hb, wv_ref[...]))
    so_ref[...] = gbuf[:, FFN_PAD + lt - (FFN_CONV - 1):FFN_PAD + lt, :].reshape(so_ref.shape)
    acts = []
    for (bsl, lsl), gate, val in zip(subs, gates, vals):
        nb, nl, _ = gate.shape
        conv = cw_ref[FFN_CONV - 1:FFN_CONV, :] * gate
        for tap in range(FFN_CONV - 1):
            back = FFN_CONV - 1 - tap
            conv = conv + cw_ref[tap:tap + 1, :] * gbuf[bsl, FFN_PAD + lsl.start - back:FFN_PAD + lsl.stop - back, :]
        acts.append((_silu(conv + cb_ref[...]).reshape(nb * nl, tf) * val).astype(BF16))
    parts = [_dot(act, wd_ref[...]) for act in acts]
    for (bsl, lsl), gate, part in zip(subs, gates, parts):
        nb, nl, _ = gate.shape
        y_ref[bsl, lsl, :] += part.reshape(nb, nl, d)

    @pl.when(f == pl.num_programs(2) - 1)
    def _():
        x1_copy.wait()
        x2 = x1_buf[...] + g2_ref[...] * y_ref[...]
        y = x2 * lax.rsqrt(jnp.mean(x2 * x2, axis=-1, keepdims=True) + EPS) * fn_ref[...]
        y_ref[...] = y


def _ffn(h2, x1, st0, mod3, mod_row0, w_up16, w_down16, conv_w, conv_b, final_norm, bt, lt):
    b, l, d = h2.shape
    tf = 512
    nf = D_FF // tf
    mrow = mod_row0 // bt
    has_halo = lt < l
    halo_blocks = lt // SUBLANES
    kern = functools.partial(_ffn_kernel, has_halo=has_halo, n_sub=4)
    return pl.pallas_call(
        kern,
        grid=(b // bt, l // lt, nf),
        in_specs=[
            pl.BlockSpec((bt, lt, d), lambda i, t, f: (i, t, 0)),
            pl.BlockSpec((1, SUBLANES, d), lambda i, t, f: (i, jnp.maximum(t * halo_blocks - 1, 0), 0)),
            pl.BlockSpec((d, tf), lambda i, t, f: (0, f)),
            pl.BlockSpec((d, tf), lambda i, t, f: (0, nf + f)),
            pl.BlockSpec((tf, d), lambda i, t, f: (f, 0)),
            pl.BlockSpec((FFN_CONV, tf), lambda i, t, f: (0, f)),
            pl.BlockSpec((1, tf), lambda i, t, f: (0, f)),
            pl.BlockSpec((bt, FFN_CONV - 1, tf), lambda i, t, f: (i, 0, f)),
            pl.BlockSpec(memory_space=pl.ANY),
            pl.BlockSpec((bt, 1, d), lambda i, t, f: (mrow + i, 0, 5)),
            pl.BlockSpec((1, 1, d), lambda i, t, f: (0, 0, 0)),
        ],
        out_specs=[
            pl.BlockSpec((bt, lt, d), lambda i, t, f: (i, t, 0)),
            pl.BlockSpec((bt, 1, FFN_CONV - 1, tf), lambda i, t, f: (i, t, 0, f)),
        ],
        out_shape=[
            jax.ShapeDtypeStruct((b, l, d), F32),
            jax.ShapeDtypeStruct((b, l // lt, FFN_CONV - 1, D_FF), F32),
        ],
        scratch_shapes=[
            pltpu.VMEM((bt, lt + FFN_PAD, tf), F32),
            pltpu.VMEM((bt, lt, d), F32),
            pltpu.SemaphoreType.DMA(()),
        ],
        compiler_params=_params("arbitrary", "arbitrary", "arbitrary",
                                vmem_limit_bytes=FFN_VMEM_LIMIT_BYTES),
        name="ffn",
    )(h2, h2, w_up16, w_up16, w_down16, conv_w, conv_b.reshape(1, D_FF), st0, x1, mod3,
      final_norm.reshape(1, 1, d))


def _trunk(x, mod3, mod_row0, states, weights, cfg):
    (w_gla16, w_gdn16, w_small, norm1, gla_wg, gla_bg, gla_norm, gdn_conv_w, gate_vec, gdn_norm, w_out16,
     norm2, w_up16, w_down16, ffn_conv_w, ffn_conv_b, final_norm) = weights
    s_gla, s_gdn, s_conv, s_ffn = states
    proj, small = _in_proj(x, mod3, mod_row0, norm1, w_gla16, w_gdn16, w_small, cfg["in_bt"], cfg["in_lt"])
    o_a, n_gla = _gla_mix(proj, small, s_gla, gla_wg, gla_bg, gla_norm, cfg["gla_bt"], cfg["gla_lt"])
    o_b, n_gdn, n_conv = _gdn_mix(proj, small, s_conv, s_gdn, gdn_conv_w, gate_vec, gdn_norm,
                                  cfg["gdn_bt"], cfg["gdn_lt"])
    x1, h2 = _out_proj(x, o_a, o_b, mod3, mod_row0, w_out16, norm2, cfg["out_bt"], cfg["out_lt"])
    y, n_ffn = _ffn(h2, x1, s_ffn, mod3, mod_row0, w_up16, w_down16, ffn_conv_w, ffn_conv_b,
                    final_norm, cfg["ffn_bt"], cfg["ffn_lt"])
    return y, n_gla[None], n_gdn[None], n_conv[None], n_ffn[:, -1][None]


def kernel(x_prompt, x_sample, c_prompt, c_sample, state_gla, state_gdn, state_gdn_conv, state_ffn_conv, w_ada, b_ada, norm1, w_in, gla_wg, gla_bg, gla_norm, gdn_conv_w, gdn_a_log, gdn_dt_bias, gdn_norm, w_out, norm2, w_up, ffn_conv_w, ffn_conv_b, w_down, final_norm):
    bp = x_prompt.shape[0]
    bs = x_sample.shape[0]

    w_in16 = w_in[0].astype(BF16)
    w_gla16 = w_in16
    w_gdn16 = w_in16[:, IN_OFF_GDN:IN_OFF_BETA]
    w_small = jnp.concatenate(
        [w_in16[:, IN_OFF_LR:IN_OFF_GDN], w_in16[:, IN_OFF_BETA:],
         jnp.zeros((D_MODEL, SMALL_COLS - GLA_GATE_RANK - 2 * GDN_HEADS), BF16)], axis=1)
    w_out16 = w_out[0].astype(BF16)
    w_up16 = w_up[0].astype(BF16)
    w_down16 = w_down[0].astype(BF16)
    gate_vec = jnp.zeros((2, SMALL_COLS), F32)
    gate_vec = gate_vec.at[0, ALPHA_OFF:ALPHA_OFF + GDN_HEADS].set(gdn_a_log[0])
    gate_vec = gate_vec.at[1, ALPHA_OFF:ALPHA_OFF + GDN_HEADS].set(gdn_dt_bias[0])

    c_all = jnp.concatenate([c_sample, c_prompt], axis=0)
    mod = _ada_mod(c_all, w_ada[0], b_ada[0])
    mod3 = mod.reshape(bs + bp, 1, N_MOD * D_MODEL)

    weights = (w_gla16, w_gdn16, w_small, norm1[0], gla_wg[0], gla_bg[0], gla_norm[0], gdn_conv_w[0], gate_vec,
               gdn_norm[0], w_out16, norm2[0], w_up16, w_down16, ffn_conv_w[0], ffn_conv_b[0],
               final_norm)

    fresh = (jnp.zeros((bp, GLA_HEADS, GLA_DK, GLA_DV), F32),
             jnp.zeros((bp, GDN_HEADS, GDN_DK, GDN_DV), F32),
             jnp.zeros((bp, GDN_CONV - 1, GDN_CONV_DIM), F32),
             jnp.zeros((bp, FFN_CONV - 1, D_FF), F32))
    cfg_p = dict(in_bt=1, in_lt=1024, gla_bt=1, gla_lt=512, gdn_bt=1, gdn_lt=256,
                 out_bt=1, out_lt=512, ffn_bt=1, ffn_lt=1024)
    y_p, p_gla, p_gdn, p_conv, p_ffn = _trunk(x_prompt, mod3, bs, fresh, weights, cfg_p)

    carried = (state_gla[0], state_gdn[0], state_gdn_conv[0], state_ffn_conv[0])
    ls = x_sample.shape[1]
    cfg_s = dict(in_bt=bs, in_lt=ls, gla_bt=4, gla_lt=ls, gdn_bt=4, gdn_lt=ls,
                 out_bt=bs // 2, out_lt=ls, ffn_bt=bs, ffn_lt=ls)
    y_s, s_gla, s_gdn, s_conv, s_ffn = _trunk(x_sample, mod3, 0, carried, weights, cfg_s)
    return (y_p, y_s, p_gla, p_gdn, p_conv, p_ffn, s_gla, s_gdn, s_conv, s_ffn)
```

```python
import functools

import jax
import jax.numpy as jnp
from jax import lax
from jax.experimental import pallas as pl
from jax.experimental.pallas import tpu as pltpu

F32 = jnp.float32
BF16 = jnp.bfloat16

D_MODEL = 2048
CHUNK = 64
GLA_HEADS = 4
GLA_DK = 128
GLA_DV = 256
GLA_WIDTH = GLA_HEADS * GLA_DV
GLA_GATE_RANK = 16
GLA_GATE_NORM = 16.0
GLA_MILD_LOG_DECAY = 60.0
GDN_HEADS = 8
GDN_DK = 128
GDN_DV = 128
GDN_WIDTH = GDN_HEADS * GDN_DV
GDN_CONV = 4
GDN_CONV_DIM = 3 * GDN_WIDTH
D_FF = 5632
FFN_CONV = 3
N_MOD = 6
EPS = 1e-6

GLA_COLS = 2 * GLA_HEADS * GLA_DK + 2 * GLA_WIDTH
GDN_COLS = 4 * GDN_WIDTH
MAIN_COLS = GLA_COLS + GDN_COLS
SMALL_COLS = 128
LR_OFF, BETA_OFF, ALPHA_OFF = 0, GLA_GATE_RANK, GLA_GATE_RANK + GDN_HEADS
IN_OFF_LR = GLA_COLS
IN_OFF_GDN = IN_OFF_LR + GLA_GATE_RANK
IN_OFF_BETA = IN_OFF_GDN + GDN_COLS

SUBLANES = 8
VMEM_LIMIT_BYTES = 56 * 1024 * 1024
FFN_VMEM_LIMIT_BYTES = 60 * 1024 * 1024

NT_DIMS = (((1,), (1,)), ((), ()))
TN_DIMS = (((0,), (0,)), ((), ()))


def _dot(a, b):
    return jnp.dot(a, b, preferred_element_type=F32)


def _dot_nt(a, b):
    return lax.dot_general(a, b, NT_DIMS, preferred_element_type=F32)


def _dot_tn(a, b):
    return lax.dot_general(a, b, TN_DIMS, preferred_element_type=F32)


def _dot_f32(a, b):
    return jnp.dot(a, b, preferred_element_type=F32, precision=lax.Precision.HIGHEST)


def _dot_many(a_list, b_list):
    return [_dot(a.astype(BF16), b.astype(BF16)) for a, b in zip(a_list, b_list)]


def _silu(x):
    return x * jax.nn.sigmoid(x)


def _softplus(x):
    return jnp.maximum(x, 0.0) + jnp.log1p(jnp.exp(-jnp.abs(x)))


def _log_sigmoid(x):
    return jnp.minimum(x, 0.0) - jnp.log1p(jnp.exp(-jnp.abs(x)))


def _params(*semantics, vmem_limit_bytes=VMEM_LIMIT_BYTES):
    return pltpu.CompilerParams(dimension_semantics=semantics, vmem_limit_bytes=vmem_limit_bytes)


def _rider_specs(rider, grid):
    steps = grid[0] * grid[1]
    rows, cols = rider.shape
    assert rows % steps == 0, (rows, steps)
    spec = pl.BlockSpec((rows // steps, cols), lambda i, t: (i * grid[1] + t, 0))
    return spec, spec, jax.ShapeDtypeStruct(rider.shape, BF16)


def _tri_masks():
    row = lax.broadcasted_iota(jnp.int32, (CHUNK, CHUNK), 0)
    col = lax.broadcasted_iota(jnp.int32, (CHUNK, CHUNK), 1)
    return row >= col, row > col


def _ada_kernel(c_ref, w_ref, b_ref, o_ref):
    s = _silu(c_ref[...]).astype(BF16)
    o_ref[...] = _dot(s, w_ref[...].astype(BF16)) + b_ref[...]


def _ada_mod(c_all, w_ada, b_ada):
    rows = c_all.shape[0]
    n = w_ada.shape[1]
    tn = 1024
    return pl.pallas_call(
        _ada_kernel,
        grid=(n // tn,),
        in_specs=[
            pl.BlockSpec((rows, D_MODEL), lambda j: (0, 0)),
            pl.BlockSpec((D_MODEL, tn), lambda j: (0, j)),
            pl.BlockSpec((1, tn), lambda j: (0, j)),
        ],
        out_specs=pl.BlockSpec((rows, tn), lambda j: (0, j)),
        out_shape=jax.ShapeDtypeStruct((rows, n), F32),
        compiler_params=_params("arbitrary"),
        name="ada_mod",
    )(c_all, w_ada, b_ada.reshape(1, n))


def _in_proj_kernel(x_ref, sc_ref, sh_ref, n1_ref, wa_ref, wb_ref, ws_ref, o_ref, os_ref, h_scr,
                    *, n_a):
    bt, lt, d = x_ref.shape
    j = pl.program_id(2)

    @pl.when(j == 0)
    def _():
        x = x_ref[...]
        y = x * lax.rsqrt(jnp.mean(x * x, axis=-1, keepdims=True) + EPS) * n1_ref[...]
        h = y * (1.0 + sc_ref[...]) + sh_ref[...]
        hb = h.reshape(bt * lt, d).astype(BF16)
        h_scr[...] = hb
        os_ref[...] = _dot(hb, ws_ref[...]).reshape(os_ref.shape)

    @pl.when(j < n_a)
    def _():
        o_ref[...] = _dot(h_scr[...], wa_ref[...]).reshape(o_ref.shape)

    @pl.when(j >= n_a)
    def _():
        o_ref[...] = _dot(h_scr[...], wb_ref[...]).reshape(o_ref.shape)


def _in_proj(x, mod3, mod_row0, norm1, w_gla16, w_gdn16, w_small, bt, lt):
    b, l, d = x.shape
    tn = 1024
    n_a = GLA_COLS // tn
    mrow = mod_row0 // bt
    grid = (b // bt, l // lt, MAIN_COLS // tn)
    return pl.pallas_call(
        functools.partial(_in_proj_kernel, n_a=n_a),
        grid=grid,
        in_specs=[
            pl.BlockSpec((bt, lt, d), lambda i, t, j: (i, t, 0)),
            pl.BlockSpec((bt, 1, d), lambda i, t, j: (mrow + i, 0, 1)),
            pl.BlockSpec((bt, 1, d), lambda i, t, j: (mrow + i, 0, 0)),
            pl.BlockSpec((1, 1, d), lambda i, t, j: (0, 0, 0)),
            pl.BlockSpec((d, tn), lambda i, t, j: (0, jnp.minimum(j, n_a - 1))),
            pl.BlockSpec((d, tn), lambda i, t, j: (0, jnp.maximum(j - n_a, 0))),
            pl.BlockSpec((d, SMALL_COLS), lambda i, t, j: (0, 0)),
        ],
        out_specs=[
            pl.BlockSpec((bt, lt, tn), lambda i, t, j: (i, t, j)),
            pl.BlockSpec((bt, lt, SMALL_COLS), lambda i, t, j: (i, t, 0)),
        ],
        out_shape=[
            jax.ShapeDtypeStruct((b, l, MAIN_COLS), F32),
            jax.ShapeDtypeStruct((b, l, SMALL_COLS), F32),
        ],
        scratch_shapes=[pltpu.VMEM((bt * lt, d), BF16)],
        compiler_params=_params("arbitrary", "arbitrary", "arbitrary"),
        name="in_proj",
    )(x, mod3, mod3, norm1.reshape(1, 1, d), w_gla16, w_gdn16, w_small)


def _gla_levels():
    row = lax.broadcasted_iota(jnp.int32, (CHUNK, CHUNK), 0)
    col = lax.broadcasted_iota(jnp.int32, (CHUNK, CHUNK), 1)
    ops, masks = [], []
    half = CHUNK // 2
    while half >= 1:
        shift = half.bit_length()
        parent_row = jnp.left_shift(jnp.right_shift(row, shift), shift)
        parent_col = jnp.left_shift(jnp.right_shift(col, shift), shift)
        m_row = parent_row + (half - 1)
        right_row = row > m_row
        in_span = (right_row & (col > m_row) & (col <= row)) | (
            jnp.logical_not(right_row) & (col > row) & (col <= m_row))
        ops.append(jnp.where(in_span, 1.0, 0.0))
        masks.append((parent_row == parent_col) & right_row & (col <= m_row))
        half //= 2
    return ops, masks, row == col


def _gla_kernel(q_ref, k_ref, v_ref, r_ref, sm_ref, wg_ref, bg_ref, gn_ref, s0_ref, *rest, has_rider):
    if has_rider:
        src_ref, o_ref, s_ref, dst_ref, st_scr, attn_scr = rest
        dst_ref[...] = src_ref[...].astype(dst_ref.dtype)
    else:
        o_ref, s_ref, st_scr, attn_scr = rest
    t = pl.program_id(1)
    bt, lt = q_ref.shape[0], q_ref.shape[1]
    n_chunks = lt // CHUNK
    causal, _ = _tri_masks()
    tri = causal.astype(F32)
    scale = GLA_DK ** -0.5

    @pl.when(t == 0)
    def _():
        for b in range(bt):
            for h in range(GLA_HEADS):
                st_scr[b, h] = s0_ref[b, h].T

    wg = wg_ref[...].astype(BF16)
    heads = range(GLA_HEADS)
    kcs = [slice(h * GLA_DK, (h + 1) * GLA_DK) for h in heads]
    vcs = [slice(h * GLA_DV, (h + 1) * GLA_DV) for h in heads]
    items = [(b, pl.ds(c * CHUNK, CHUNK)) for c in range(n_chunks) for b in range(bt)]

    def scaled_q(b, rows):
        return [q_ref[b, rows, kc] * scale for kc in kcs]

    log_a, big_g = [], []
    for b, rows in items:
        a_lr = sm_ref[b, rows, LR_OFF:LR_OFF + GLA_GATE_RANK].astype(BF16)
        la = _log_sigmoid(_dot(a_lr, wg) + bg_ref[...]) / GLA_GATE_NORM
        log_a.append(la)
        big_g.append(_dot_f32(tri, la))
    total = big_g[0][CHUNK - 1:CHUNK, :]
    for g in big_g[1:]:
        total = jnp.minimum(total, g[CHUNK - 1:CHUNK, :])
    mild = jnp.min(total) > -GLA_MILD_LOG_DECAY

    @pl.when(mild)
    def _():
        for n, (b, rows) in enumerate(items):
            g_mid = big_g[n][CHUNK // 2 - 1:CHUNK // 2, :]
            e_q = jnp.exp(big_g[n] - g_mid)
            e_k = jnp.exp(g_mid - big_g[n])
            q = scaled_q(b, rows)
            scores = [_dot_nt((qq * e_q[:, kc]).astype(BF16),
                              (k_ref[b, rows, kc] * e_k[:, kc]).astype(BF16))
                      for qq, kc in zip(q, kcs)]
            for h in heads:
                attn_scr[n, h] = jnp.where(causal, scores[h], 0.0)

    @pl.when(jnp.logical_not(mild))
    def _():
        level_ops, level_masks, diag = _gla_levels()
        level_op = jnp.concatenate(level_ops, axis=0)
        for n, (b, rows) in enumerate(items):
            sums = _dot_f32(level_op, log_a[n])
            q = scaled_q(b, rows)
            k = [k_ref[b, rows, kc] for kc in kcs]
            scores = [jnp.where(diag, _dot_nt(qq.astype(BF16), kk.astype(BF16)), 0.0)
                      for qq, kk in zip(q, k)]
            for i, mask in enumerate(level_masks):
                e = jnp.exp(sums[i * CHUNK:(i + 1) * CHUNK])
                part = [_dot_nt((qq * e[:, kc]).astype(BF16), (kk * e[:, kc]).astype(BF16))
                        for qq, kk, kc in zip(q, k, kcs)]
                scores = [jnp.where(mask, p, a) for p, a in zip(part, scores)]
            for h in heads:
                attn_scr[n, h] = scores[h]

    pending = []
    for n, (b, rows) in enumerate(items):
        g_last = big_g[n][CHUNK - 1:CHUNK, :]
        e_g = jnp.exp(big_g[n])
        e_kl = jnp.exp(g_last - big_g[n])
        e_l = jnp.exp(g_last)
        q = scaled_q(b, rows)
        v = [v_ref[b, rows, vc].astype(BF16) for vc in vcs]
        v_k = [_dot_tn(vv, (k_ref[b, rows, kc] * e_kl[:, kc]).astype(BF16)) for vv, kc in zip(v, kcs)]
        a_v = [_dot(attn_scr[n, h].astype(BF16), v[h]) for h in heads]
        q_g = [(qq * e_g[:, kc]).astype(BF16) for qq, kc in zip(q, kcs)]
        pending.append((q_g, a_v, v_k, [e_l[:, kc] for kc in kcs]))
    for c in range(n_chunks):
        group = range(c * bt, (c + 1) * bt)
        st = [[st_scr[items[n][0], h] for h in heads] for n in group]
        q_s = [[_dot_nt(qq, ss.astype(BF16)) for qq, ss in zip(pending[n][0], st_n)]
               for n, st_n in zip(group, st)]
        for n, st_n, q_s_n in zip(group, st, q_s):
            b, rows = items[n]
            _, a_v, v_k, e_l = pending[n]
            for h in heads:
                st_scr[b, h] = st_n[h] * e_l[h] + v_k[h]
                o = a_v[h] + q_s_n[h]
                o = o * lax.rsqrt(jnp.mean(o * o, axis=-1, keepdims=True) + EPS) * gn_ref[...]
                o = o * _silu(r_ref[b, rows, vcs[h]])
                o_ref[b, rows, vcs[h]] = o.astype(o_ref.dtype)

    @pl.when(t == pl.num_programs(1) - 1)
    def _():
        for b in range(bt):
            for h in range(GLA_HEADS):
                s_ref[b, h] = st_scr[b, h].T


def _gla_mix(proj, small, s0, gla_wg, gla_bg, gla_norm, bt, lt, rider=None):
    b, l, _ = proj.shape
    kw = GLA_HEADS * GLA_DK
    grid = (b // bt, l // lt)
    riders = [] if rider is None else [_rider_specs(rider, grid)]
    return pl.pallas_call(
        functools.partial(_gla_kernel, has_rider=rider is not None),
        grid=grid,
        in_specs=[
            pl.BlockSpec((bt, lt, kw), lambda i, t: (i, t, 0)),
            pl.BlockSpec((bt, lt, kw), lambda i, t: (i, t, 1)),
            pl.BlockSpec((bt, lt, GLA_WIDTH), lambda i, t: (i, t, 1)),
            pl.BlockSpec((bt, lt, GLA_WIDTH), lambda i, t: (i, t, 2)),
            pl.BlockSpec((bt, lt, SMALL_COLS), lambda i, t: (i, t, 0)),
            pl.BlockSpec((GLA_GATE_RANK, kw), lambda i, t: (0, 0)),
            pl.BlockSpec((1, kw), lambda i, t: (0, 0)),
            pl.BlockSpec((1, GLA_DV), lambda i, t: (0, 0)),
            pl.BlockSpec((bt, GLA_HEADS, GLA_DK, GLA_DV), lambda i, t: (i, 0, 0, 0)),
        ] + [r[0] for r in riders],
        out_specs=[
            pl.BlockSpec((bt, lt, GLA_WIDTH), lambda i, t: (i, t, 0)),
            pl.BlockSpec((bt, GLA_HEADS, GLA_DK, GLA_DV), lambda i, t: (i, 0, 0, 0)),
        ] + [r[1] for r in riders],
        out_shape=[
            jax.ShapeDtypeStruct((b, l, GLA_WIDTH), BF16),
            jax.ShapeDtypeStruct((b, GLA_HEADS, GLA_DK, GLA_DV), F32),
        ] + [r[2] for r in riders],
        scratch_shapes=[pltpu.VMEM((bt, GLA_HEADS, GLA_DV, GLA_DK), F32),
                        pltpu.VMEM((bt * (lt // CHUNK), GLA_HEADS, CHUNK, CHUNK), F32)],
        compiler_params=_params("arbitrary", "arbitrary"),
        name="gla_mix",
    )(proj, proj, proj, proj, small, gla_wg, gla_bg.reshape(1, kw), gla_norm.reshape(1, GLA_DV), s0,
      *([] if rider is None else [rider]))


CONV_PAD = SUBLANES
INV_BASE_LOG2 = 3
GDN_CHUNK_GROUP = 4


def _inverse_masks():
    row = lax.broadcasted_iota(jnp.int32, (CHUNK, CHUNK), 0)
    col = lax.broadcasted_iota(jnp.int32, (CHUNK, CHUNK), 1)

    def same_block(log2):
        return jnp.right_shift(row, log2) == jnp.right_shift(col, log2)

    base = same_block(INV_BASE_LOG2)
    merges = []
    log2 = INV_BASE_LOG2
    while (1 << log2) < CHUNK:
        merges.append(same_block(log2 + 1) & jnp.logical_not(same_block(log2)))
        log2 += 1
    return base, merges


def _unit_lower_inverse_minus_eye(lower, base, merges):
    neg = [jnp.where(base, -l, 0.0) for l in lower]
    x = neg
    p = neg
    for _ in range(INV_BASE_LOG2 - 1):
        p = _dot_many(p, p)
        xp = _dot_many(x, p)
        x = [a + b + c for a, b, c in zip(x, p, xp)]
    for m in merges:
        c = [jnp.where(m, l, 0.0) for l in lower]
        w = [a + b for a, b in zip(c, _dot_many(x, c))]
        wx = _dot_many(w, x)
        x = [a - (b + d) for a, b, d in zip(x, w, wx)]
    return x


def _gdn_kernel(x_ref, g_ref, sm_ref, cw_ref, ga_ref, gn_ref, c0_ref, s0_ref, *rest, has_rider):
    if has_rider:
        src_ref, o_ref, s_ref, c_ref, dst_ref, xbuf = rest
        dst_ref[...] = src_ref[...].astype(dst_ref.dtype)
    else:
        o_ref, s_ref, c_ref, xbuf = rest
    t = pl.program_id(1)
    bt, lt = x_ref.shape[0], x_ref.shape[1]
    causal, strict = _tri_masks()
    tri = causal.astype(F32)
    inv_base, inv_merges = _inverse_masks()
    scale = GDN_DK ** -0.5

    @pl.when(t == 0)
    def _():
        xbuf[:, 0:CONV_PAD, :] = jnp.zeros((bt, CONV_PAD, GDN_CONV_DIM), F32)
        xbuf[:, CONV_PAD - (GDN_CONV - 1):CONV_PAD, :] = c0_ref[...]
        s_ref[...] = s0_ref[...]

    @pl.when(t > 0)
    def _():
        xbuf[:, 0:CONV_PAD, :] = xbuf[:, lt:lt + CONV_PAD, :]

    xbuf[:, CONV_PAD:CONV_PAD + lt, :] = x_ref[...]
    c_ref[...] = xbuf[:, lt + CONV_PAD - (GDN_CONV - 1):lt + CONV_PAD, :]

    def conv_silu(b, r0, cols):
        acc = cw_ref[GDN_CONV - 1:GDN_CONV, cols] * xbuf[b, r0 + CONV_PAD:r0 + CONV_PAD + CHUNK, cols]
        for tap in range(GDN_CONV - 1):
            start = r0 + CONV_PAD - (GDN_CONV - 1 - tap)
            acc = acc + cw_ref[tap:tap + 1, cols] * xbuf[b, start:start + CHUNK, cols]
        return _silu(acc)

    def l2norm(x):
        return x * lax.rsqrt(jnp.sum(x * x, axis=-1, keepdims=True) + EPS)

    heads = range(GDN_HEADS)
    def prep_gates(b, c):
        sm = sm_ref[b, pl.ds(c * CHUNK, CHUNK), :]
        gate = -jnp.exp(ga_ref[0:1, :]) * _softplus(sm + ga_ref[1:2, :])
        big_g = _dot_f32(tri, gate)
        return jax.nn.sigmoid(sm), big_g, big_g.T

    def prep_head(b, c, gates, h):
        beta, big_g, big_gt = gates
        r0 = c * CHUNK
        g_col = big_g[:, ALPHA_OFF + h:ALPHA_OFF + h + 1]
        g_row = big_gt[ALPHA_OFF + h:ALPHA_OFF + h + 1, :]
        b_col = beta[:, BETA_OFF + h:BETA_OFF + h + 1]
        g_last = g_col[CHUNK - 1:CHUNK, :]
        e_g = jnp.exp(g_col)
        q = l2norm(conv_silu(b, r0, slice(h * GDN_DK, (h + 1) * GDN_DK))) * scale
        k = l2norm(conv_silu(b, r0, slice(GDN_WIDTH + h * GDN_DK, GDN_WIDTH + (h + 1) * GDN_DK)))
        v = conv_silu(b, r0, slice(2 * GDN_WIDTH + h * GDN_DV, 2 * GDN_WIDTH + (h + 1) * GDN_DV))
        kb = k * b_col
        return dict(
            decay=jnp.where(causal, jnp.exp(g_col - g_row), 0.0),
            rhs=jnp.concatenate([v * b_col, kb * e_g], axis=-1),
            q16=q.astype(BF16), k16=k.astype(BF16), kb16=kb.astype(BF16),
            q_g=(q * e_g).astype(BF16),
            k_dec=(k * jnp.exp(g_last - g_col)).astype(BF16),
            e_last=jnp.exp(g_last))

    def key_products(p):
        kk_t = [_dot_nt(ph["kb16"], ph["k16"]) for ph in p]
        qk_t = [_dot_nt(ph["q16"], ph["k16"]) for ph in p]
        lower = [jnp.where(strict, m * ph["decay"], 0.0) for m, ph in zip(kk_t, p)]
        attn = [jnp.where(causal, m * ph["decay"], 0.0).astype(BF16) for m, ph in zip(qk_t, p)]
        return lower, attn

    n_chunks = lt // CHUNK
    items = [(b, c) for c in range(n_chunks) for b in range(bt)]
    pending = []
    for i0 in range(0, len(items), GDN_CHUNK_GROUP):
        group = items[i0:i0 + GDN_CHUNK_GROUP]
        gates = [prep_gates(b, c) for b, c in group]
        p = [prep_head(b, c, g, h) for (b, c), g in zip(group, gates) for h in heads]
        lower, attn = key_products(p)
        xinv = _unit_lower_inverse_minus_eye(lower, inv_base, inv_merges)
        rhs = [ph["rhs"] for ph in p]
        sol = [r + xr for r, xr in zip(rhs, _dot_many(xinv, rhs))]
        for n in range(len(group)):
            sl = slice(n * GDN_HEADS, (n + 1) * GDN_HEADS)
            pending.append((attn[sl], [so[:, :GDN_DV] for so in sol[sl]],
                            [so[:, GDN_DV:].astype(BF16) for so in sol[sl]],
                            [ph["q_g"] for ph in p[sl]], [ph["k_dec"] for ph in p[sl]],
                            [ph["e_last"] for ph in p[sl]]))
    for c in range(n_chunks):
        group = range(c * bt, (c + 1) * bt)
        attn, sol_v, sol_k, q_g, k_dec, e_last = (
            [x for n in group for x in pending[n][field]] for field in range(6))
        where = [(items[n][0], h) for n in group for h in heads]
        s = [s_ref[b, h] for b, h in where]
        s16 = [ss.astype(BF16) for ss in s]
        k_s = [_dot(a, ss) for a, ss in zip(sol_k, s16)]
        q_s = [_dot(a, ss) for a, ss in zip(q_g, s16)]
        u16 = [(sv - ks).astype(BF16) for sv, ks in zip(sol_v, k_s)]
        a_u = [_dot(a, uu) for a, uu in zip(attn, u16)]
        k_u = [_dot_tn(kd, uu) for kd, uu in zip(k_dec, u16)]
        rows = pl.ds(c * CHUNK, CHUNK)
        for i, (b, h) in enumerate(where):
            hc = slice(h * GDN_DV, (h + 1) * GDN_DV)
            s_ref[b, h] = e_last[i] * s[i] + k_u[i]
            o = q_s[i] + a_u[i]
            o = o * lax.rsqrt(jnp.mean(o * o, axis=-1, keepdims=True) + EPS) * gn_ref[...]
            o = o * _silu(g_ref[b, rows, hc])
            o_ref[b, rows, hc] = o.astype(o_ref.dtype)


def _gdn_mix(proj, small, c0, s0, conv_w, gate_vec, gdn_norm, bt, lt, rider=None):
    b, l, _ = proj.shape
    grid = (b // bt, l // lt)
    riders = [] if rider is None else [_rider_specs(rider, grid)]
    return pl.pallas_call(
        functools.partial(_gdn_kernel, has_rider=rider is not None),
        grid=grid,
        in_specs=[
            pl.BlockSpec((bt, lt, GDN_CONV_DIM), lambda i, t: (i, t, 1)),
            pl.BlockSpec((bt, lt, GDN_WIDTH), lambda i, t: (i, t, 6)),
            pl.BlockSpec((bt, lt, SMALL_COLS), lambda i, t: (i, t, 0)),
            pl.BlockSpec((GDN_CONV, GDN_CONV_DIM), lambda i, t: (0, 0)),
            pl.BlockSpec((2, SMALL_COLS), lambda i, t: (0, 0)),
            pl.BlockSpec((1, GDN_DV), lambda i, t: (0, 0)),
            pl.BlockSpec((bt, GDN_CONV - 1, GDN_CONV_DIM), lambda i, t: (i, 0, 0)),
            pl.BlockSpec((bt, GDN_HEADS, GDN_DK, GDN_DV), lambda i, t: (i, 0, 0, 0)),
        ] + [r[0] for r in riders],
        out_specs=[
            pl.BlockSpec((bt, lt, GDN_WIDTH), lambda i, t: (i, t, 0)),
            pl.BlockSpec((bt, GDN_HEADS, GDN_DK, GDN_DV), lambda i, t: (i, 0, 0, 0)),
            pl.BlockSpec((bt, GDN_CONV - 1, GDN_CONV_DIM), lambda i, t: (i, 0, 0)),
        ] + [r[1] for r in riders],
        out_shape=[
            jax.ShapeDtypeStruct((b, l, GDN_WIDTH), BF16),
            jax.ShapeDtypeStruct((b, GDN_HEADS, GDN_DK, GDN_DV), F32),
            jax.ShapeDtypeStruct((b, GDN_CONV - 1, GDN_CONV_DIM), F32),
        ] + [r[2] for r in riders],
        scratch_shapes=[pltpu.VMEM((bt, lt + CONV_PAD, GDN_CONV_DIM), F32)],
        compiler_params=_params("arbitrary", "arbitrary"),
        name="gdn_mix",
    )(proj, proj, small, conv_w, gate_vec, gdn_norm.reshape(1, GDN_DV), c0, s0,
      *([] if rider is None else [rider]))


def _out_proj_kernel(x_ref, oa_ref, ob_ref, w_ref, g1_ref, sc_ref, sh_ref, n2_ref, x1_ref, h2_ref,
                     *, n_sub):
    bt, lt, d = x_ref.shape
    if bt == 1:
        ls = lt // n_sub
        subs = [(slice(0, 1), slice(i * ls, (i + 1) * ls)) for i in range(n_sub)]
    else:
        bs = bt // n_sub
        subs = [(slice(i * bs, (i + 1) * bs), slice(0, lt)) for i in range(n_sub)]
    mixes = []
    for bsl, lsl in subs:
        oa = oa_ref[bsl, lsl, :]
        nb, nl, _ = oa.shape
        oa = oa.reshape(nb * nl, GLA_WIDTH)
        ob = ob_ref[bsl, lsl, :].reshape(nb * nl, GDN_WIDTH)
        mix = _dot(oa, w_ref[0:GLA_WIDTH, :]) + _dot(ob, w_ref[GLA_WIDTH:GLA_WIDTH + GDN_WIDTH, :])
        mixes.append(mix.reshape(nb, nl, d))
    for (bsl, lsl), mix in zip(subs, mixes):
        x1 = x_ref[bsl, lsl, :] + g1_ref[bsl] * mix
        x1_ref[bsl, lsl, :] = x1
        y = x1 * lax.rsqrt(jnp.mean(x1 * x1, axis=-1, keepdims=True) + EPS) * n2_ref[...]
        h2_ref[bsl, lsl, :] = (y * (1.0 + sc_ref[bsl]) + sh_ref[bsl]).astype(h2_ref.dtype)


def _out_proj(x, o_a, o_b, mod3, mod_row0, w_out16, norm2, bt, lt):
    b, l, d = x.shape
    mrow = mod_row0 // bt
    return pl.pallas_call(
        functools.partial(_out_proj_kernel, n_sub=2),
        grid=(b // bt, l // lt),
        in_specs=[
            pl.BlockSpec((bt, lt, d), lambda i, t: (i, t, 0)),
            pl.BlockSpec((bt, lt, GLA_WIDTH), lambda i, t: (i, t, 0)),
            pl.BlockSpec((bt, lt, GDN_WIDTH), lambda i, t: (i, t, 0)),
            pl.BlockSpec((GLA_WIDTH + GDN_WIDTH, d), lambda i, t: (0, 0)),
            pl.BlockSpec((bt, 1, d), lambda i, t: (mrow + i, 0, 2)),
            pl.BlockSpec((bt, 1, d), lambda i, t: (mrow + i, 0, 4)),
            pl.BlockSpec((bt, 1, d), lambda i, t: (mrow + i, 0, 3)),
            pl.BlockSpec((1, 1, d), lambda i, t: (0, 0, 0)),
        ],
        out_specs=[
            pl.BlockSpec((bt, lt, d), lambda i, t: (i, t, 0)),
            pl.BlockSpec((bt, lt, d), lambda i, t: (i, t, 0)),
        ],
        out_shape=[
            jax.ShapeDtypeStruct((b, l, d), F32),
            jax.ShapeDtypeStruct((b, l, d), BF16),
        ],
        compiler_params=_params("arbitrary", "arbitrary"),
        name="out_proj",
    )(x, o_a, o_b, w_out16, mod3, mod3, mod3, norm2.reshape(1, 1, d))


FFN_PAD = SUBLANES


def _ffn_kernel(h_ref, halo_ref, wg_ref, wv_ref, wd_ref, cw_ref, cb_ref, st_ref, x1_hbm, g2_ref,
                fn_ref, y_ref, so_ref, gbuf, x1_buf, x1_sem, *, has_halo, n_sub):
    ib = pl.program_id(0)
    t = pl.program_id(1)
    f = pl.program_id(2)
    bt, lt, d = h_ref.shape
    tf = wg_ref.shape[1]
    x1_copy = pltpu.make_async_copy(
        x1_hbm.at[pl.ds(ib * bt, bt), pl.ds(t * lt, lt), :], x1_buf, x1_sem)
    if bt == 1:
        ls = lt // n_sub
        subs = [(slice(0, 1), slice(i * ls, (i + 1) * ls)) for i in range(n_sub)]
    else:
        bs = bt // n_sub
        subs = [(slice(i * bs, (i + 1) * bs), slice(0, lt)) for i in range(n_sub)]

    @pl.when(f == 0)
    def _():
        x1_copy.start()
        y_ref[...] = jnp.zeros(y_ref.shape, F32)

    prev = st_ref[...]
    if has_halo:
        halo_gate = _dot(halo_ref[0], wg_ref[...])
        prev = jnp.where(t == 0, prev, halo_gate[SUBLANES - (FFN_CONV - 1):SUBLANES, :][None])
    gbuf[:, FFN_PAD - (FFN_CONV - 1):FFN_PAD, :] = prev
    gates, vals = [], []
    for bsl, lsl in subs:
        hb = h_ref[bsl, lsl, :]
        nb, nl, _ = hb.shape
        hb = hb.reshape(nb * nl, d)
        gate = _dot(hb, wg_ref[...]).reshape(nb, nl, tf)
        gbuf[bsl, FFN_PAD + lsl.start:FFN_PAD + lsl.stop, :] = gate
        gates.append(gate)
        vals.append(_dot(hb, wv_ref[...]))
    so_ref[...] = gbuf[:, FFN_PAD + lt - (FFN_CONV - 1):FFN_PAD + lt, :].reshape(so_ref.shape)
    acts = []
    for (bsl, lsl), gate, val in zip(subs, gates, vals):
        nb, nl, _ = gate.shape
        conv = cw_ref[FFN_CONV - 1:FFN_CONV, :] * gate
        for tap in range(FFN_CONV - 1):
            back = FFN_CONV - 1 - tap
            conv = conv + cw_ref[tap:tap + 1, :] * gbuf[bsl, FFN_PAD + lsl.start - back:FFN_PAD + lsl.stop - back, :]
        acts.append((_silu(conv + cb_ref[...]).reshape(nb * nl, tf) * val).astype(BF16))
    parts = [_dot(act, wd_ref[...]) for act in acts]
    for (bsl, lsl), gate, part in zip(subs, gates, parts):
        nb, nl, _ = gate.shape
        y_ref[bsl, lsl, :] += part.reshape(nb, nl, d)

    @pl.when(f == pl.num_programs(2) - 1)
    def _():
        x1_copy.wait()
        x2 = x1_buf[...] + g2_ref[...] * y_ref[...]
        y = x2 * lax.rsqrt(jnp.mean(x2 * x2, axis=-1, keepdims=True) + EPS) * fn_ref[...]
        y_ref[...] = y


def _ffn(h2, x1, st0, mod3, mod_row0, w_up16, w_down16, conv_w, conv_b, final_norm, bt, lt):
    b, l, d = h2.shape
    tf = 512
    nf = D_FF // tf
    mrow = mod_row0 // bt
    has_halo = lt < l
    halo_blocks = lt // SUBLANES
    kern = functools.partial(_ffn_kernel, has_halo=has_halo, n_sub=4)
    return pl.pallas_call(
        kern,
        grid=(b // bt, l // lt, nf),
        in_specs=[
            pl.BlockSpec((bt, lt, d), lambda i, t, f: (i, t, 0)),
            pl.BlockSpec((1, SUBLANES, d), lambda i, t, f: (i, jnp.maximum(t * halo_blocks - 1, 0), 0)),
            pl.BlockSpec((d, tf), lambda i, t, f: (0, f)),
            pl.BlockSpec((d, tf), lambda i, t, f: (0, nf + f)),
            pl.BlockSpec((tf, d), lambda i, t, f: (f, 0)),
            pl.BlockSpec((FFN_CONV, tf), lambda i, t, f: (0, f)),
            pl.BlockSpec((1, tf), lambda i, t, f: (0, f)),
            pl.BlockSpec((bt, FFN_CONV - 1, tf), lambda i, t, f: (i, 0, f)),
            pl.BlockSpec(memory_space=pl.ANY),
            pl.BlockSpec((bt, 1, d), lambda i, t, f: (mrow + i, 0, 5)),
            pl.BlockSpec((1, 1, d), lambda i, t, f: (0, 0, 0)),
        ],
        out_specs=[
            pl.BlockSpec((bt, lt, d), lambda i, t, f: (i, t, 0)),
            pl.BlockSpec((bt, 1, FFN_CONV - 1, tf), lambda i, t, f: (i, t, 0, f)),
        ],
        out_shape=[
            jax.ShapeDtypeStruct((b, l, d), F32),
            jax.ShapeDtypeStruct((b, l // lt, FFN_CONV - 1, D_FF), F32),
        ],
        scratch_shapes=[
            pltpu.VMEM((bt, lt + FFN_PAD, tf), F32),
            pltpu.VMEM((bt, lt, d), F32),
            pltpu.SemaphoreType.DMA(()),
        ],
        compiler_params=_params("arbitrary", "arbitrary", "arbitrary",
                                vmem_limit_bytes=FFN_VMEM_LIMIT_BYTES),
        name="ffn",
    )(h2, h2, w_up16, w_up16, w_down16, conv_w, conv_b.reshape(1, D_FF), st0, x1, mod3,
      final_norm.reshape(1, 1, d))


def _trunk(x, mod3, mod_row0, states, weights, cfg):
    (w_gla16, w_gdn16, w_small, norm1, gla_wg, gla_bg, gla_norm, gdn_conv_w, gate_vec, gdn_norm, w_out16,
     norm2, w_up, w_down, ffn_conv_w, ffn_conv_b, final_norm) = weights
    s_gla, s_gdn, s_conv, s_ffn = states
    convert = w_up.dtype != BF16
    proj, small = _in_proj(x, mod3, mod_row0, norm1, w_gla16, w_gdn16, w_small, cfg["in_bt"], cfg["in_lt"])
    o_a, n_gla, *w_down16 = _gla_mix(proj, small, s_gla, gla_wg, gla_bg, gla_norm, cfg["gla_bt"],
                                     cfg["gla_lt"], rider=w_down if convert else None)
    o_b, n_gdn, n_conv, *w_up16 = _gdn_mix(proj, small, s_conv, s_gdn, gdn_conv_w, gate_vec, gdn_norm,
                                           cfg["gdn_bt"], cfg["gdn_lt"], rider=w_up if convert else None)
    w_up16, w_down16 = (w_up16[0], w_down16[0]) if convert else (w_up, w_down)
    x1, h2 = _out_proj(x, o_a, o_b, mod3, mod_row0, w_out16, norm2, cfg["out_bt"], cfg["out_lt"])
    y, n_ffn = _ffn(h2, x1, s_ffn, mod3, mod_row0, w_up16, w_down16, ffn_conv_w, ffn_conv_b,
                    final_norm, cfg["ffn_bt"], cfg["ffn_lt"])
    return (y, n_gla[None], n_gdn[None], n_conv[None], n_ffn[:, -1][None]), (w_up16, w_down16)


def kernel(x_prompt, x_sample, c_prompt, c_sample, state_gla, state_gdn, state_gdn_conv, state_ffn_conv, w_ada, b_ada, norm1, w_in, gla_wg, gla_bg, gla_norm, gdn_conv_w, gdn_a_log, gdn_dt_bias, gdn_norm, w_out, norm2, w_up, ffn_conv_w, ffn_conv_b, w_down, final_norm):
    bp = x_prompt.shape[0]
    bs = x_sample.shape[0]

    w_in16 = w_in[0].astype(BF16)
    w_gla16 = w_in16
    w_gdn16 = w_in16[:, IN_OFF_GDN:IN_OFF_BETA]
    w_small = jnp.concatenate(
        [w_in16[:, IN_OFF_LR:IN_OFF_GDN], w_in16[:, IN_OFF_BETA:],
         jnp.zeros((D_MODEL, SMALL_COLS - GLA_GATE_RANK - 2 * GDN_HEADS), BF16)], axis=1)
    w_out16 = w_out[0].astype(BF16)
    gate_vec = jnp.zeros((2, SMALL_COLS), F32)
    gate_vec = gate_vec.at[0, ALPHA_OFF:ALPHA_OFF + GDN_HEADS].set(gdn_a_log[0])
    gate_vec = gate_vec.at[1, ALPHA_OFF:ALPHA_OFF + GDN_HEADS].set(gdn_dt_bias[0])

    c_all = jnp.concatenate([c_sample, c_prompt], axis=0)
    mod = _ada_mod(c_all, w_ada[0], b_ada[0])
    mod3 = mod.reshape(bs + bp, 1, N_MOD * D_MODEL)

    weights = (w_gla16, w_gdn16, w_small, norm1[0], gla_wg[0], gla_bg[0], gla_norm[0], gdn_conv_w[0], gate_vec,
               gdn_norm[0], w_out16, norm2[0], w_up[0], w_down[0], ffn_conv_w[0], ffn_conv_b[0],
               final_norm)

    fresh = (jnp.zeros((bp, GLA_HEADS, GLA_DK, GLA_DV), F32),
             jnp.zeros((bp, GDN_HEADS, GDN_DK, GDN_DV), F32),
             jnp.zeros((bp, GDN_CONV - 1, GDN_CONV_DIM), F32),
             jnp.zeros((bp, FFN_CONV - 1, D_FF), F32))
    cfg_p = dict(in_bt=1, in_lt=1024, gla_bt=1, gla_lt=512, gdn_bt=1, gdn_lt=256,
                 out_bt=1, out_lt=512, ffn_bt=1, ffn_lt=1024)
    (y_p, p_gla, p_gdn, p_conv, p_ffn), ffn_w16 = _trunk(x_prompt, mod3, bs, fresh, weights, cfg_p)

    carried = (state_gla[0], state_gdn[0], state_gdn_conv[0], state_ffn_conv[0])
    ls = x_sample.shape[1]
    cfg_s = dict(in_bt=bs, in_lt=ls, gla_bt=4, gla_lt=ls, gdn_bt=4, gdn_lt=ls,
                 out_bt=bs // 2, out_lt=ls, ffn_bt=bs, ffn_lt=ls)
    weights_s = weights[:12] + ffn_w16 + weights[14:]
    (y_s, s_gla, s_gdn, s_conv, s_ffn), _ = _trunk(x_sample, mod3, 0, carried, weights_s, cfg_s)
    return (y_p, y_s, p_gla, p_gdn, p_conv, p_ffn, s_gla, s_gdn, s_conv, s_ffn)
```

```python
import functools

import jax
import jax.numpy as jnp
from jax import lax
from jax.experimental import pallas as pl
from jax.experimental.pallas import tpu as pltpu

F32 = jnp.float32
BF16 = jnp.bfloat16

D_MODEL = 2048
CHUNK = 64
GLA_HEADS = 4
GLA_DK = 128
GLA_DV = 256
GLA_WIDTH = GLA_HEADS * GLA_DV
GLA_GATE_RANK = 16
GLA_GATE_NORM = 16.0
GLA_MILD_LOG_DECAY = 60.0
GDN_HEADS = 8
GDN_DK = 128
GDN_DV = 128
GDN_WIDTH = GDN_HEADS * GDN_DV
GDN_CONV = 4
GDN_CONV_DIM = 3 * GDN_WIDTH
D_FF = 5632
FFN_CONV = 3
N_MOD = 6
EPS = 1e-6

GLA_COLS = 2 * GLA_HEADS * GLA_DK + 2 * GLA_WIDTH
GDN_COLS = 4 * GDN_WIDTH
MAIN_COLS = GLA_COLS + GDN_COLS
SMALL_COLS = 128
LR_OFF, BETA_OFF, ALPHA_OFF = 0, GLA_GATE_RANK, GLA_GATE_RANK + GDN_HEADS
IN_OFF_LR = GLA_COLS
IN_OFF_GDN = IN_OFF_LR + GLA_GATE_RANK
IN_OFF_BETA = IN_OFF_GDN + GDN_COLS

SUBLANES = 8
VMEM_LIMIT_BYTES = 56 * 1024 * 1024
BIG_VMEM_LIMIT_BYTES = 60 * 1024 * 1024

NT_DIMS = (((1,), (1,)), ((), ()))
TN_DIMS = (((0,), (0,)), ((), ()))


def _dot(a, b):
    return jnp.dot(a, b, preferred_element_type=F32)


def _dot_nt(a, b):
    return lax.dot_general(a, b, NT_DIMS, preferred_element_type=F32)


def _dot_tn(a, b):
    return lax.dot_general(a, b, TN_DIMS, preferred_element_type=F32)


def _dot_f32(a, b):
    return jnp.dot(a, b, preferred_element_type=F32, precision=lax.Precision.HIGHEST)


def _dot_many(a_list, b_list):
    return [_dot(a.astype(BF16), b.astype(BF16)) for a, b in zip(a_list, b_list)]


def _silu(x):
    return x * jax.nn.sigmoid(x)


def _softplus(x):
    return jnp.maximum(x, 0.0) + jnp.log1p(jnp.exp(-jnp.abs(x)))


def _log_sigmoid(x):
    return jnp.minimum(x, 0.0) - jnp.log1p(jnp.exp(-jnp.abs(x)))


def _params(*semantics, vmem_limit_bytes=VMEM_LIMIT_BYTES):
    return pltpu.CompilerParams(dimension_semantics=semantics, vmem_limit_bytes=vmem_limit_bytes)


def _rider_specs(rider, grid):
    steps = grid[0] * grid[1]
    rows, cols = rider.shape
    assert rows % steps == 0, (rows, steps)
    spec = pl.BlockSpec((rows // steps, cols), lambda i, t: (i * grid[1] + t, 0))
    return spec, spec, jax.ShapeDtypeStruct(rider.shape, BF16)


def _tri_masks():
    row = lax.broadcasted_iota(jnp.int32, (CHUNK, CHUNK), 0)
    col = lax.broadcasted_iota(jnp.int32, (CHUNK, CHUNK), 1)
    return row >= col, row > col


def _ada_kernel(c_ref, w_ref, b_ref, o_ref):
    s = _silu(c_ref[...]).astype(BF16)
    o_ref[...] = _dot(s, w_ref[...].astype(BF16)) + b_ref[...]


def _ada_mod(c_all, w_ada, b_ada):
    rows = c_all.shape[0]
    n = w_ada.shape[1]
    tn = 1024
    return pl.pallas_call(
        _ada_kernel,
        grid=(n // tn,),
        in_specs=[
            pl.BlockSpec((rows, D_MODEL), lambda j: (0, 0)),
            pl.BlockSpec((D_MODEL, tn), lambda j: (0, j)),
            pl.BlockSpec((1, tn), lambda j: (0, j)),
        ],
        out_specs=pl.BlockSpec((rows, tn), lambda j: (0, j)),
        out_shape=jax.ShapeDtypeStruct((rows, n), F32),
        compiler_params=_params("arbitrary"),
        name="ada_mod",
    )(c_all, w_ada, b_ada.reshape(1, n))


def _in_proj_kernel(x_ref, sc_ref, sh_ref, n1_ref, wa_ref, wb_ref, ws_ref, o_ref, os_ref, h_scr,
                    *, n_a):
    bt, lt, d = x_ref.shape
    j = pl.program_id(2)

    @pl.when(j == 0)
    def _():
        x = x_ref[...]
        y = x * lax.rsqrt(jnp.mean(x * x, axis=-1, keepdims=True) + EPS) * n1_ref[...]
        h = y * (1.0 + sc_ref[...]) + sh_ref[...]
        hb = h.reshape(bt * lt, d).astype(BF16)
        h_scr[...] = hb
        os_ref[...] = _dot(hb, ws_ref[...]).reshape(os_ref.shape)

    @pl.when(j < n_a)
    def _():
        o_ref[...] = _dot(h_scr[...], wa_ref[...].astype(BF16)).reshape(o_ref.shape)

    @pl.when(j >= n_a)
    def _():
        o_ref[...] = _dot(h_scr[...], wb_ref[...]).reshape(o_ref.shape)


def _in_proj(x, mod3, mod_row0, norm1, w_gla, w_gdn16, w_small, bt, lt):
    b, l, d = x.shape
    tn = 1024
    n_a = GLA_COLS // tn
    mrow = mod_row0 // bt
    grid = (b // bt, l // lt, MAIN_COLS // tn)
    return pl.pallas_call(
        functools.partial(_in_proj_kernel, n_a=n_a),
        grid=grid,
        in_specs=[
            pl.BlockSpec((bt, lt, d), lambda i, t, j: (i, t, 0)),
            pl.BlockSpec((bt, 1, d), lambda i, t, j: (mrow + i, 0, 1)),
            pl.BlockSpec((bt, 1, d), lambda i, t, j: (mrow + i, 0, 0)),
            pl.BlockSpec((1, 1, d), lambda i, t, j: (0, 0, 0)),
            pl.BlockSpec((d, tn), lambda i, t, j: (0, jnp.minimum(j, n_a - 1))),
            pl.BlockSpec((d, tn), lambda i, t, j: (0, jnp.maximum(j - n_a, 0))),
            pl.BlockSpec((d, SMALL_COLS), lambda i, t, j: (0, 0)),
        ],
        out_specs=[
            pl.BlockSpec((bt, lt, tn), lambda i, t, j: (i, t, j)),
            pl.BlockSpec((bt, lt, SMALL_COLS), lambda i, t, j: (i, t, 0)),
        ],
        out_shape=[
            jax.ShapeDtypeStruct((b, l, MAIN_COLS), F32),
            jax.ShapeDtypeStruct((b, l, SMALL_COLS), F32),
        ],
        scratch_shapes=[pltpu.VMEM((bt * lt, d), BF16)],
        compiler_params=_params("arbitrary", "arbitrary", "arbitrary",
                                vmem_limit_bytes=BIG_VMEM_LIMIT_BYTES),
        name="in_proj",
    )(x, mod3, mod3, norm1.reshape(1, 1, d), w_gla, w_gdn16, w_small)


def _gla_levels():
    row = lax.broadcasted_iota(jnp.int32, (CHUNK, CHUNK), 0)
    col = lax.broadcasted_iota(jnp.int32, (CHUNK, CHUNK), 1)
    ops, masks = [], []
    half = CHUNK // 2
    while half >= 1:
        shift = half.bit_length()
        parent_row = jnp.left_shift(jnp.right_shift(row, shift), shift)
        parent_col = jnp.left_shift(jnp.right_shift(col, shift), shift)
        m_row = parent_row + (half - 1)
        right_row = row > m_row
        in_span = (right_row & (col > m_row) & (col <= row)) | (
            jnp.logical_not(right_row) & (col > row) & (col <= m_row))
        ops.append(jnp.where(in_span, 1.0, 0.0))
        masks.append((parent_row == parent_col) & right_row & (col <= m_row))
        half //= 2
    return ops, masks, row == col


def _gla_kernel(q_ref, k_ref, v_ref, r_ref, sm_ref, wg_ref, bg_ref, gn_ref, s0_ref, *rest, n_riders):
    src_refs, (o_ref, s_ref), rest = rest[:n_riders], rest[n_riders:n_riders + 2], rest[n_riders + 2:]
    dst_refs, (st_scr, attn_scr) = rest[:n_riders], rest[n_riders:]
    for src_ref, dst_ref in zip(src_refs, dst_refs):
        dst_ref[...] = src_ref[...].astype(dst_ref.dtype)
    t = pl.program_id(1)
    bt, lt = q_ref.shape[0], q_ref.shape[1]
    n_chunks = lt // CHUNK
    causal, _ = _tri_masks()
    tri = causal.astype(F32)
    scale = GLA_DK ** -0.5

    @pl.when(t == 0)
    def _():
        for b in range(bt):
            for h in range(GLA_HEADS):
                st_scr[b, h] = s0_ref[b, h].T

    wg = wg_ref[...].astype(BF16)
    heads = range(GLA_HEADS)
    kcs = [slice(h * GLA_DK, (h + 1) * GLA_DK) for h in heads]
    vcs = [slice(h * GLA_DV, (h + 1) * GLA_DV) for h in heads]
    items = [(b, pl.ds(c * CHUNK, CHUNK)) for c in range(n_chunks) for b in range(bt)]

    def scaled_q(b, rows):
        return [q_ref[b, rows, kc] * scale for kc in kcs]

    log_a, big_g = [], []
    for b, rows in items:
        a_lr = sm_ref[b, rows, LR_OFF:LR_OFF + GLA_GATE_RANK].astype(BF16)
        la = _log_sigmoid(_dot(a_lr, wg) + bg_ref[...]) / GLA_GATE_NORM
        log_a.append(la)
        big_g.append(_dot_f32(tri, la))
    total = big_g[0][CHUNK - 1:CHUNK, :]
    for g in big_g[1:]:
        total = jnp.minimum(total, g[CHUNK - 1:CHUNK, :])
    mild = jnp.min(total) > -GLA_MILD_LOG_DECAY

    @pl.when(mild)
    def _():
        for n, (b, rows) in enumerate(items):
            g_mid = big_g[n][CHUNK // 2 - 1:CHUNK // 2, :]
            e_q = jnp.exp(big_g[n] - g_mid)
            e_k = jnp.exp(g_mid - big_g[n])
            q = scaled_q(b, rows)
            scores = [_dot_nt((qq * e_q[:, kc]).astype(BF16),
                              (k_ref[b, rows, kc] * e_k[:, kc]).astype(BF16))
                      for qq, kc in zip(q, kcs)]
            for h in heads:
                attn_scr[n, h] = jnp.where(causal, scores[h], 0.0)

    @pl.when(jnp.logical_not(mild))
    def _():
        level_ops, level_masks, diag = _gla_levels()
        level_op = jnp.concatenate(level_ops, axis=0)
        for n, (b, rows) in enumerate(items):
            sums = _dot_f32(level_op, log_a[n])
            q = scaled_q(b, rows)
            k = [k_ref[b, rows, kc] for kc in kcs]
            scores = [jnp.where(diag, _dot_nt(qq.astype(BF16), kk.astype(BF16)), 0.0)
                      for qq, kk in zip(q, k)]
            for i, mask in enumerate(level_masks):
                e = jnp.exp(sums[i * CHUNK:(i + 1) * CHUNK])
                part = [_dot_nt((qq * e[:, kc]).astype(BF16), (kk * e[:, kc]).astype(BF16))
                        for qq, kk, kc in zip(q, k, kcs)]
                scores = [jnp.where(mask, p, a) for p, a in zip(part, scores)]
            for h in heads:
                attn_scr[n, h] = scores[h]

    pending = []
    for n, (b, rows) in enumerate(items):
        g_last = big_g[n][CHUNK - 1:CHUNK, :]
        e_g = jnp.exp(big_g[n])
        e_kl = jnp.exp(g_last - big_g[n])
        e_l = jnp.exp(g_last)
        q = scaled_q(b, rows)
        v = [v_ref[b, rows, vc].astype(BF16) for vc in vcs]
        v_k = [_dot_tn(vv, (k_ref[b, rows, kc] * e_kl[:, kc]).astype(BF16)) for vv, kc in zip(v, kcs)]
        a_v = [_dot(attn_scr[n, h].astype(BF16), v[h]) for h in heads]
        q_g = [(qq * e_g[:, kc]).astype(BF16) for qq, kc in zip(q, kcs)]
        pending.append((q_g, a_v, v_k, [e_l[:, kc] for kc in kcs]))
    for c in range(n_chunks):
        group = range(c * bt, (c + 1) * bt)
        st = [[st_scr[items[n][0], h] for h in heads] for n in group]
        q_s = [[_dot_nt(qq, ss.astype(BF16)) for qq, ss in zip(pending[n][0], st_n)]
               for n, st_n in zip(group, st)]
        for n, st_n, q_s_n in zip(group, st, q_s):
            b, rows = items[n]
            _, a_v, v_k, e_l = pending[n]
            for h in heads:
                st_scr[b, h] = st_n[h] * e_l[h] + v_k[h]
                o = a_v[h] + q_s_n[h]
                o = o * lax.rsqrt(jnp.mean(o * o, axis=-1, keepdims=True) + EPS) * gn_ref[...]
                o = o * _silu(r_ref[b, rows, vcs[h]])
                o_ref[b, rows, vcs[h]] = o.astype(o_ref.dtype)

    @pl.when(t == pl.num_programs(1) - 1)
    def _():
        for b in range(bt):
            for h in range(GLA_HEADS):
                s_ref[b, h] = st_scr[b, h].T


def _gla_mix(proj, small, s0, gla_wg, gla_bg, gla_norm, bt, lt, riders=()):
    b, l, _ = proj.shape
    kw = GLA_HEADS * GLA_DK
    grid = (b // bt, l // lt)
    rider_args = list(riders)
    riders = [_rider_specs(r, grid) for r in rider_args]
    return pl.pallas_call(
        functools.partial(_gla_kernel, n_riders=len(riders)),
        grid=grid,
        in_specs=[
            pl.BlockSpec((bt, lt, kw), lambda i, t: (i, t, 0)),
            pl.BlockSpec((bt, lt, kw), lambda i, t: (i, t, 1)),
            pl.BlockSpec((bt, lt, GLA_WIDTH), lambda i, t: (i, t, 1)),
            pl.BlockSpec((bt, lt, GLA_WIDTH), lambda i, t: (i, t, 2)),
            pl.BlockSpec((bt, lt, SMALL_COLS), lambda i, t: (i, t, 0)),
            pl.BlockSpec((GLA_GATE_RANK, kw), lambda i, t: (0, 0)),
            pl.BlockSpec((1, kw), lambda i, t: (0, 0)),
            pl.BlockSpec((1, GLA_DV), lambda i, t: (0, 0)),
            pl.BlockSpec((bt, GLA_HEADS, GLA_DK, GLA_DV), lambda i, t: (i, 0, 0, 0)),
        ] + [r[0] for r in riders],
        out_specs=[
            pl.BlockSpec((bt, lt, GLA_WIDTH), lambda i, t: (i, t, 0)),
            pl.BlockSpec((bt, GLA_HEADS, GLA_DK, GLA_DV), lambda i, t: (i, 0, 0, 0)),
        ] + [r[1] for r in riders],
        out_shape=[
            jax.ShapeDtypeStruct((b, l, GLA_WIDTH), BF16),
            jax.ShapeDtypeStruct((b, GLA_HEADS, GLA_DK, GLA_DV), F32),
        ] + [r[2] for r in riders],
        scratch_shapes=[pltpu.VMEM((bt, GLA_HEADS, GLA_DV, GLA_DK), F32),
                        pltpu.VMEM((bt * (lt // CHUNK), GLA_HEADS, CHUNK, CHUNK), F32)],
        compiler_params=_params("arbitrary", "arbitrary"),
        name="gla_mix",
    )(proj, proj, proj, proj, small, gla_wg, gla_bg.reshape(1, kw), gla_norm.reshape(1, GLA_DV), s0,
      *rider_args)


CONV_PAD = SUBLANES
INV_BASE_LOG2 = 3
GDN_CHUNK_GROUP = 4


def _inverse_masks():
    row = lax.broadcasted_iota(jnp.int32, (CHUNK, CHUNK), 0)
    col = lax.broadcasted_iota(jnp.int32, (CHUNK, CHUNK), 1)

    def same_block(log2):
        return jnp.right_shift(row, log2) == jnp.right_shift(col, log2)

    base = same_block(INV_BASE_LOG2)
    merges = []
    log2 = INV_BASE_LOG2
    while (1 << log2) < CHUNK:
        merges.append(same_block(log2 + 1) & jnp.logical_not(same_block(log2)))
        log2 += 1
    return base, merges


def _unit_lower_inverse_minus_eye(lower, base, merges):
    neg = [jnp.where(base, -l, 0.0) for l in lower]
    x = neg
    p = neg
    for _ in range(INV_BASE_LOG2 - 1):
        p = _dot_many(p, p)
        xp = _dot_many(x, p)
        x = [a + b + c for a, b, c in zip(x, p, xp)]
    for m in merges:
        c = [jnp.where(m, l, 0.0) for l in lower]
        w = [a + b for a, b in zip(c, _dot_many(x, c))]
        wx = _dot_many(w, x)
        x = [a - (b + d) for a, b, d in zip(x, w, wx)]
    return x


def _gdn_kernel(x_ref, g_ref, sm_ref, cw_ref, ga_ref, gn_ref, c0_ref, s0_ref, *rest, n_riders):
    src_refs, (o_ref, s_ref, c_ref), rest = rest[:n_riders], rest[n_riders:n_riders + 3], rest[n_riders + 3:]
    dst_refs, (xbuf,) = rest[:n_riders], rest[n_riders:]
    for src_ref, dst_ref in zip(src_refs, dst_refs):
        dst_ref[...] = src_ref[...].astype(dst_ref.dtype)
    t = pl.program_id(1)
    bt, lt = x_ref.shape[0], x_ref.shape[1]
    causal, strict = _tri_masks()
    tri = causal.astype(F32)
    inv_base, inv_merges = _inverse_masks()
    scale = GDN_DK ** -0.5

    @pl.when(t == 0)
    def _():
        xbuf[:, 0:CONV_PAD, :] = jnp.zeros((bt, CONV_PAD, GDN_CONV_DIM), F32)
        xbuf[:, CONV_PAD - (GDN_CONV - 1):CONV_PAD, :] = c0_ref[...]
        s_ref[...] = s0_ref[...]

    @pl.when(t > 0)
    def _():
        xbuf[:, 0:CONV_PAD, :] = xbuf[:, lt:lt + CONV_PAD, :]

    xbuf[:, CONV_PAD:CONV_PAD + lt, :] = x_ref[...]
    c_ref[...] = xbuf[:, lt + CONV_PAD - (GDN_CONV - 1):lt + CONV_PAD, :]

    def conv_silu(b, r0, cols):
        acc = cw_ref[GDN_CONV - 1:GDN_CONV, cols] * xbuf[b, r0 + CONV_PAD:r0 + CONV_PAD + CHUNK, cols]
        for tap in range(GDN_CONV - 1):
            start = r0 + CONV_PAD - (GDN_CONV - 1 - tap)
            acc = acc + cw_ref[tap:tap + 1, cols] * xbuf[b, start:start + CHUNK, cols]
        return _silu(acc)

    def l2norm(x):
        return x * lax.rsqrt(jnp.sum(x * x, axis=-1, keepdims=True) + EPS)

    heads = range(GDN_HEADS)
    def prep_gates(b, c):
        sm = sm_ref[b, pl.ds(c * CHUNK, CHUNK), :]
        gate = -jnp.exp(ga_ref[0:1, :]) * _softplus(sm + ga_ref[1:2, :])
        big_g = _dot_f32(tri, gate)
        return jax.nn.sigmoid(sm), big_g, big_g.T

    def prep_head(b, c, gates, h):
        beta, big_g, big_gt = gates
        r0 = c * CHUNK
        g_col = big_g[:, ALPHA_OFF + h:ALPHA_OFF + h + 1]
        g_row = big_gt[ALPHA_OFF + h:ALPHA_OFF + h + 1, :]
        b_col = beta[:, BETA_OFF + h:BETA_OFF + h + 1]
        g_last = g_col[CHUNK - 1:CHUNK, :]
        e_g = jnp.exp(g_col)
        q = l2norm(conv_silu(b, r0, slice(h * GDN_DK, (h + 1) * GDN_DK))) * scale
        k = l2norm(conv_silu(b, r0, slice(GDN_WIDTH + h * GDN_DK, GDN_WIDTH + (h + 1) * GDN_DK)))
        v = conv_silu(b, r0, slice(2 * GDN_WIDTH + h * GDN_DV, 2 * GDN_WIDTH + (h + 1) * GDN_DV))
        kb = k * b_col
        return dict(
            decay=jnp.where(causal, jnp.exp(g_col - g_row), 0.0),
            rhs=jnp.concatenate([v * b_col, kb * e_g], axis=-1),
            q16=q.astype(BF16), k16=k.astype(BF16), kb16=kb.astype(BF16),
            q_g=(q * e_g).astype(BF16),
            k_dec=(k * jnp.exp(g_last - g_col)).astype(BF16),
            e_last=jnp.exp(g_last))

    def key_products(p):
        kk_t = [_dot_nt(ph["kb16"], ph["k16"]) for ph in p]
        qk_t = [_dot_nt(ph["q16"], ph["k16"]) for ph in p]
        lower = [jnp.where(strict, m * ph["decay"], 0.0) for m, ph in zip(kk_t, p)]
        attn = [jnp.where(causal, m * ph["decay"], 0.0).astype(BF16) for m, ph in zip(qk_t, p)]
        return lower, attn

    n_chunks = lt // CHUNK
    items = [(b, c) for c in range(n_chunks) for b in range(bt)]
    pending = []
    for i0 in range(0, len(items), GDN_CHUNK_GROUP):
        group = items[i0:i0 + GDN_CHUNK_GROUP]
        gates = [prep_gates(b, c) for b, c in group]
        p = [prep_head(b, c, g, h) for (b, c), g in zip(group, gates) for h in heads]
        lower, attn = key_products(p)
        xinv = _unit_lower_inverse_minus_eye(lower, inv_base, inv_merges)
        rhs = [ph["rhs"] for ph in p]
        sol = [r + xr for r, xr in zip(rhs, _dot_many(xinv, rhs))]
        for n in range(len(group)):
            sl = slice(n * GDN_HEADS, (n + 1) * GDN_HEADS)
            pending.append((attn[sl], [so[:, :GDN_DV] for so in sol[sl]],
                            [so[:, GDN_DV:].astype(BF16) for so in sol[sl]],
                            [ph["q_g"] for ph in p[sl]], [ph["k_dec"] for ph in p[sl]],
                            [ph["e_last"] for ph in p[sl]]))
    for c in range(n_chunks):
        group = range(c * bt, (c + 1) * bt)
        attn, sol_v, sol_k, q_g, k_dec, e_last = (
            [x for n in group for x in pending[n][field]] for field in range(6))
        where = [(items[n][0], h) for n in group for h in heads]
        s = [s_ref[b, h] for b, h in where]
        s16 = [ss.astype(BF16) for ss in s]
        k_s = [_dot(a, ss) for a, ss in zip(sol_k, s16)]
        q_s = [_dot(a, ss) for a, ss in zip(q_g, s16)]
        u16 = [(sv - ks).astype(BF16) for sv, ks in zip(sol_v, k_s)]
        a_u = [_dot(a, uu) for a, uu in zip(attn, u16)]
        k_u = [_dot_tn(kd, uu) for kd, uu in zip(k_dec, u16)]
        rows = pl.ds(c * CHUNK, CHUNK)
        for i, (b, h) in enumerate(where):
            hc = slice(h * GDN_DV, (h + 1) * GDN_DV)
            s_ref[b, h] = e_last[i] * s[i] + k_u[i]
            o = q_s[i] + a_u[i]
            o = o * lax.rsqrt(jnp.mean(o * o, axis=-1, keepdims=True) + EPS) * gn_ref[...]
            o = o * _silu(g_ref[b, rows, hc])
            o_ref[b, rows, hc] = o.astype(o_ref.dtype)


def _gdn_mix(proj, small, c0, s0, conv_w, gate_vec, gdn_norm, bt, lt, riders=()):
    b, l, _ = proj.shape
    grid = (b // bt, l // lt)
    rider_args = list(riders)
    riders = [_rider_specs(r, grid) for r in rider_args]
    return pl.pallas_call(
        functools.partial(_gdn_kernel, n_riders=len(riders)),
        grid=grid,
        in_specs=[
            pl.BlockSpec((bt, lt, GDN_CONV_DIM), lambda i, t: (i, t, 1)),
            pl.BlockSpec((bt, lt, GDN_WIDTH), lambda i, t: (i, t, 6)),
            pl.BlockSpec((bt, lt, SMALL_COLS), lambda i, t: (i, t, 0)),
            pl.BlockSpec((GDN_CONV, GDN_CONV_DIM), lambda i, t: (0, 0)),
            pl.BlockSpec((2, SMALL_COLS), lambda i, t: (0, 0)),
            pl.BlockSpec((1, GDN_DV), lambda i, t: (0, 0)),
            pl.BlockSpec((bt, GDN_CONV - 1, GDN_CONV_DIM), lambda i, t: (i, 0, 0)),
            pl.BlockSpec((bt, GDN_HEADS, GDN_DK, GDN_DV), lambda i, t: (i, 0, 0, 0)),
        ] + [r[0] for r in riders],
        out_specs=[
            pl.BlockSpec((bt, lt, GDN_WIDTH), lambda i, t: (i, t, 0)),
            pl.BlockSpec((bt, GDN_HEADS, GDN_DK, GDN_DV), lambda i, t: (i, 0, 0, 0)),
            pl.BlockSpec((bt, GDN_CONV - 1, GDN_CONV_DIM), lambda i, t: (i, 0, 0)),
        ] + [r[1] for r in riders],
        out_shape=[
            jax.ShapeDtypeStruct((b, l, GDN_WIDTH), BF16),
            jax.ShapeDtypeStruct((b, GDN_HEADS, GDN_DK, GDN_DV), F32),
            jax.ShapeDtypeStruct((b, GDN_CONV - 1, GDN_CONV_DIM), F32),
        ] + [r[2] for r in riders],
        scratch_shapes=[pltpu.VMEM((bt, lt + CONV_PAD, GDN_CONV_DIM), F32)],
        compiler_params=_params("arbitrary", "arbitrary"),
        name="gdn_mix",
    )(proj, proj, small, conv_w, gate_vec, gdn_norm.reshape(1, GDN_DV), c0, s0, *rider_args)


def _out_proj_kernel(x_ref, oa_ref, ob_ref, w_ref, g1_ref, sc_ref, sh_ref, n2_ref, x1_ref, h2_ref,
                     *, n_sub):
    bt, lt, d = x_ref.shape
    if bt == 1:
        ls = lt // n_sub
        subs = [(slice(0, 1), slice(i * ls, (i + 1) * ls)) for i in range(n_sub)]
    else:
        bs = bt // n_sub
        subs = [(slice(i * bs, (i + 1) * bs), slice(0, lt)) for i in range(n_sub)]
    mixes = []
    for bsl, lsl in subs:
        oa = oa_ref[bsl, lsl, :]
        nb, nl, _ = oa.shape
        oa = oa.reshape(nb * nl, GLA_WIDTH)
        ob = ob_ref[bsl, lsl, :].reshape(nb * nl, GDN_WIDTH)
        mix = _dot(oa, w_ref[0:GLA_WIDTH, :]) + _dot(ob, w_ref[GLA_WIDTH:GLA_WIDTH + GDN_WIDTH, :])
        mixes.append(mix.reshape(nb, nl, d))
    for (bsl, lsl), mix in zip(subs, mixes):
        x1 = x_ref[bsl, lsl, :] + g1_ref[bsl] * mix
        x1_ref[bsl, lsl, :] = x1
        y = x1 * lax.rsqrt(jnp.mean(x1 * x1, axis=-1, keepdims=True) + EPS) * n2_ref[...]
        h2_ref[bsl, lsl, :] = (y * (1.0 + sc_ref[bsl]) + sh_ref[bsl]).astype(h2_ref.dtype)


def _out_proj(x, o_a, o_b, mod3, mod_row0, w_out16, norm2, bt, lt):
    b, l, d = x.shape
    mrow = mod_row0 // bt
    return pl.pallas_call(
        functools.partial(_out_proj_kernel, n_sub=2),
        grid=(b // bt, l // lt),
        in_specs=[
            pl.BlockSpec((bt, lt, d), lambda i, t: (i, t, 0)),
            pl.BlockSpec((bt, lt, GLA_WIDTH), lambda i, t: (i, t, 0)),
            pl.BlockSpec((bt, lt, GDN_WIDTH), lambda i, t: (i, t, 0)),
            pl.BlockSpec((GLA_WIDTH + GDN_WIDTH, d), lambda i, t: (0, 0)),
            pl.BlockSpec((bt, 1, d), lambda i, t: (mrow + i, 0, 2)),
            pl.BlockSpec((bt, 1, d), lambda i, t: (mrow + i, 0, 4)),
            pl.BlockSpec((bt, 1, d), lambda i, t: (mrow + i, 0, 3)),
            pl.BlockSpec((1, 1, d), lambda i, t: (0, 0, 0)),
        ],
        out_specs=[
            pl.BlockSpec((bt, lt, d), lambda i, t: (i, t, 0)),
            pl.BlockSpec((bt, lt, d), lambda i, t: (i, t, 0)),
        ],
        out_shape=[
            jax.ShapeDtypeStruct((b, l, d), F32),
            jax.ShapeDtypeStruct((b, l, d), BF16),
        ],
        compiler_params=_params("arbitrary", "arbitrary"),
        name="out_proj",
    )(x, o_a, o_b, w_out16, mod3, mod3, mod3, norm2.reshape(1, 1, d))


FFN_PAD = SUBLANES


def _ffn_kernel(h_ref, halo_ref, wg_ref, wv_ref, wd_ref, cw_ref, cb_ref, st_ref, x1_hbm, g2_ref,
                fn_ref, y_ref, so_ref, gbuf, x1_buf, x1_sem, *, has_halo, n_sub):
    ib = pl.program_id(0)
    t = pl.program_id(1)
    f = pl.program_id(2)
    bt, lt, d = h_ref.shape
    tf = wg_ref.shape[1]
    x1_copy = pltpu.make_async_copy(
        x1_hbm.at[pl.ds(ib * bt, bt), pl.ds(t * lt, lt), :], x1_buf, x1_sem)
    if bt == 1:
        ls = lt // n_sub
        subs = [(slice(0, 1), slice(i * ls, (i + 1) * ls)) for i in range(n_sub)]
    else:
        bs = bt // n_sub
        subs = [(slice(i * bs, (i + 1) * bs), slice(0, lt)) for i in range(n_sub)]

    @pl.when(f == 0)
    def _():
        x1_copy.start()
        y_ref[...] = jnp.zeros(y_ref.shape, F32)

    prev = st_ref[...]
    if has_halo:
        halo_gate = _dot(halo_ref[0], wg_ref[...])
        prev = jnp.where(t == 0, prev, halo_gate[SUBLANES - (FFN_CONV - 1):SUBLANES, :][None])
    gbuf[:, FFN_PAD - (FFN_CONV - 1):FFN_PAD, :] = prev
    gates, vals = [], []
    for bsl, lsl in subs:
        hb = h_ref[bsl, lsl, :]
        nb, nl, _ = hb.shape
        hb = hb.reshape(nb * nl, d)
        gate = _dot(hb, wg_ref[...]).reshape(nb, nl, tf)
        gbuf[bsl, FFN_PAD + lsl.start:FFN_PAD + lsl.stop, :] = gate
        gates.append(gate)
        vals.append(_dot(hb, wv_ref[...]))
    so_ref[...] = gbuf[:, FFN_PAD + lt - (FFN_CONV - 1):FFN_PAD + lt, :].reshape(so_ref.shape)
    acts = []
    for (bsl, lsl), gate, val in zip(subs, gates, vals):
        nb, nl, _ = gate.shape
        conv = cw_ref[FFN_CONV - 1:FFN_CONV, :] * gate
        for tap in range(FFN_CONV - 1):
            back = FFN_CONV - 1 - tap
            conv = conv + cw_ref[tap:tap + 1, :] * gbuf[bsl, FFN_PAD + lsl.start - back:FFN_PAD + lsl.stop - back, :]
        acts.append((_silu(conv + cb_ref[...]).reshape(nb * nl, tf) * val).astype(BF16))
    parts = [_dot(act, wd_ref[...]) for act in acts]
    for (bsl, lsl), gate, part in zip(subs, gates, parts):
        nb, nl, _ = gate.shape
        y_ref[bsl, lsl, :] += part.reshape(nb, nl, d)

    @pl.when(f == pl.num_programs(2) - 1)
    def _():
        x1_copy.wait()
        x2 = x1_buf[...] + g2_ref[...] * y_ref[...]
        y = x2 * lax.rsqrt(jnp.mean(x2 * x2, axis=-1, keepdims=True) + EPS) * fn_ref[...]
        y_ref[...] = y


def _ffn(h2, x1, st0, mod3, mod_row0, w_up16, w_down16, conv_w, conv_b, final_norm, bt, lt):
    b, l, d = h2.shape
    tf = 512
    nf = D_FF // tf
    mrow = mod_row0 // bt
    has_halo = lt < l
    halo_blocks = lt // SUBLANES
    kern = functools.partial(_ffn_kernel, has_halo=has_halo, n_sub=4)
    return pl.pallas_call(
        kern,
        grid=(b // bt, l // lt, nf),
        in_specs=[
            pl.BlockSpec((bt, lt, d), lambda i, t, f: (i, t, 0)),
            pl.BlockSpec((1, SUBLANES, d), lambda i, t, f: (i, jnp.maximum(t * halo_blocks - 1, 0), 0)),
            pl.BlockSpec((d, tf), lambda i, t, f: (0, f)),
            pl.BlockSpec((d, tf), lambda i, t, f: (0, nf + f)),
            pl.BlockSpec((tf, d), lambda i, t, f: (f, 0)),
            pl.BlockSpec((FFN_CONV, tf), lambda i, t, f: (0, f)),
            pl.BlockSpec((1, tf), lambda i, t, f: (0, f)),
            pl.BlockSpec((bt, FFN_CONV - 1, tf), lambda i, t, f: (i, 0, f)),
            pl.BlockSpec(memory_space=pl.ANY),
            pl.BlockSpec((bt, 1, d), lambda i, t, f: (mrow + i, 0, 5)),
            pl.BlockSpec((1, 1, d), lambda i, t, f: (0, 0, 0)),
        ],
        out_specs=[
            pl.BlockSpec((bt, lt, d), lambda i, t, f: (i, t, 0)),
            pl.BlockSpec((bt, 1, FFN_CONV - 1, tf), lambda i, t, f: (i, t, 0, f)),
        ],
        out_shape=[
            jax.ShapeDtypeStruct((b, l, d), F32),
            jax.ShapeDtypeStruct((b, l // lt, FFN_CONV - 1, D_FF), F32),
        ],
        scratch_shapes=[
            pltpu.VMEM((bt, lt + FFN_PAD, tf), F32),
            pltpu.VMEM((bt, lt, d), F32),
            pltpu.SemaphoreType.DMA(()),
        ],
        compiler_params=_params("arbitrary", "arbitrary", "arbitrary",
                                vmem_limit_bytes=BIG_VMEM_LIMIT_BYTES),
        name="ffn",
    )(h2, h2, w_up16, w_up16, w_down16, conv_w, conv_b.reshape(1, D_FF), st0, x1, mod3,
      final_norm.reshape(1, 1, d))


def _trunk(x, mod3, mod_row0, states, weights, cfg):
    (w_gla, w_gdn16, w_small, norm1, gla_wg, gla_bg, gla_norm, gdn_conv_w, gate_vec, gdn_norm, w_out,
     norm2, w_up, w_down, ffn_conv_w, ffn_conv_b, final_norm) = weights
    s_gla, s_gdn, s_conv, s_ffn = states
    convert = w_up.dtype != BF16
    proj, small = _in_proj(x, mod3, mod_row0, norm1, w_gla, w_gdn16, w_small, cfg["in_bt"], cfg["in_lt"])
    o_a, n_gla, *gla_riders = _gla_mix(proj, small, s_gla, gla_wg, gla_bg, gla_norm, cfg["gla_bt"],
                                       cfg["gla_lt"], riders=[w_down] if convert else [])
    o_b, n_gdn, n_conv, *gdn_riders = _gdn_mix(proj, small, s_conv, s_gdn, gdn_conv_w, gate_vec, gdn_norm,
                                               cfg["gdn_bt"], cfg["gdn_lt"],
                                               riders=[w_up, w_out] if convert else [])
    w_out16, w_up16, w_down16 = (gdn_riders[1], gdn_riders[0], gla_riders[0]) if convert else (
        w_out, w_up, w_down)
    x1, h2 = _out_proj(x, o_a, o_b, mod3, mod_row0, w_out16, norm2, cfg["out_bt"], cfg["out_lt"])
    y, n_ffn = _ffn(h2, x1, s_ffn, mod3, mod_row0, w_up16, w_down16, ffn_conv_w, ffn_conv_b,
                    final_norm, cfg["ffn_bt"], cfg["ffn_lt"])
    return (y, n_gla[None], n_gdn[None], n_conv[None], n_ffn[:, -1][None]), (w_out16, w_up16, w_down16)


def kernel(x_prompt, x_sample, c_prompt, c_sample, state_gla, state_gdn, state_gdn_conv, state_ffn_conv, w_ada, b_ada, norm1, w_in, gla_wg, gla_bg, gla_norm, gdn_conv_w, gdn_a_log, gdn_dt_bias, gdn_norm, w_out, norm2, w_up, ffn_conv_w, ffn_conv_b, w_down, final_norm):
    bp = x_prompt.shape[0]
    bs = x_sample.shape[0]

    w_in0 = w_in[0]
    w_gdn16 = w_in0[:, IN_OFF_GDN:IN_OFF_BETA].astype(BF16)
    w_small = jnp.concatenate(
        [w_in0[:, IN_OFF_LR:IN_OFF_GDN], w_in0[:, IN_OFF_BETA:],
         jnp.zeros((D_MODEL, SMALL_COLS - GLA_GATE_RANK - 2 * GDN_HEADS), F32)], axis=1).astype(BF16)
    gate_vec = jnp.zeros((2, SMALL_COLS), F32)
    gate_vec = gate_vec.at[0, ALPHA_OFF:ALPHA_OFF + GDN_HEADS].set(gdn_a_log[0])
    gate_vec = gate_vec.at[1, ALPHA_OFF:ALPHA_OFF + GDN_HEADS].set(gdn_dt_bias[0])

    c_all = jnp.concatenate([c_sample, c_prompt], axis=0)
    mod = _ada_mod(c_all, w_ada[0], b_ada[0])
    mod3 = mod.reshape(bs + bp, 1, N_MOD * D_MODEL)

    weights = (w_in0, w_gdn16, w_small, norm1[0], gla_wg[0], gla_bg[0], gla_norm[0], gdn_conv_w[0], gate_vec,
               gdn_norm[0], w_out[0], norm2[0], w_up[0], w_down[0], ffn_conv_w[0], ffn_conv_b[0],
               final_norm)

    fresh = (jnp.zeros((bp, GLA_HEADS, GLA_DK, GLA_DV), F32),
             jnp.zeros((bp, GDN_HEADS, GDN_DK, GDN_DV), F32),
             jnp.zeros((bp, GDN_CONV - 1, GDN_CONV_DIM), F32),
             jnp.zeros((bp, FFN_CONV - 1, D_FF), F32))
    cfg_p = dict(in_bt=1, in_lt=1024, gla_bt=1, gla_lt=512, gdn_bt=1, gdn_lt=256,
                 out_bt=1, out_lt=512, ffn_bt=1, ffn_lt=1024)
    (y_p, p_gla, p_gdn, p_conv, p_ffn), (w_out16, w_up16, w_down16) = _trunk(
        x_prompt, mod3, bs, fresh, weights, cfg_p)

    carried = (state_gla[0], state_gdn[0], state_gdn_conv[0], state_ffn_conv[0])
    ls = x_sample.shape[1]
    cfg_s = dict(in_bt=bs, in_lt=ls, gla_bt=4, gla_lt=ls, gdn_bt=4, gdn_lt=ls,
                 out_bt=bs // 2, out_lt=ls, ffn_bt=bs, ffn_lt=ls)
    weights_s = weights[:10] + (w_out16, weights[11], w_up16, w_down16) + weights[14:]
    (y_s, s_gla, s_gdn, s_conv, s_ffn), _ = _trunk(x_sample, mod3, 0, carried, weights_s, cfg_s)
    return (y_p, y_s, p_gla, p_gdn, p_conv, p_ffn, s_gla, s_gdn, s_conv, s_ffn)
```

```python
import functools

import jax
import jax.numpy as jnp
from jax import lax
from jax.experimental import pallas as pl
from jax.experimental.pallas import tpu as pltpu

F32 = jnp.float32
BF16 = jnp.bfloat16

D_MODEL = 2048
CHUNK = 64
GLA_HEADS = 4
GLA_DK = 128
GLA_DV = 256
GLA_WIDTH = GLA_HEADS * GLA_DV
GLA_GATE_RANK = 16
GLA_GATE_NORM = 16.0
GLA_MILD_LOG_DECAY = 60.0
GDN_HEADS = 8
GDN_DK = 128
GDN_DV = 128
GDN_WIDTH = GDN_HEADS * GDN_DV
GDN_CONV = 4
GDN_CONV_DIM = 3 * GDN_WIDTH
D_FF = 5632
FFN_CONV = 3
N_MOD = 6
EPS = 1e-6

GLA_COLS = 2 * GLA_HEADS * GLA_DK + 2 * GLA_WIDTH
GDN_COLS = 4 * GDN_WIDTH
MAIN_COLS = GLA_COLS + GDN_COLS
SMALL_COLS = 128
LR_OFF, BETA_OFF, ALPHA_OFF = 0, GLA_GATE_RANK, GLA_GATE_RANK + GDN_HEADS
IN_OFF_LR = GLA_COLS
IN_OFF_GDN = IN_OFF_LR + GLA_GATE_RANK
IN_OFF_BETA = IN_OFF_GDN + GDN_COLS

SUBLANES = 8
VMEM_LIMIT_BYTES = 56 * 1024 * 1024
FFN_VMEM_LIMIT_BYTES = 60 * 1024 * 1024

NT_DIMS = (((1,), (1,)), ((), ()))
TN_DIMS = (((0,), (0,)), ((), ()))


def _dot(a, b):
    return jnp.dot(a, b, preferred_element_type=F32)


def _dot_nt(a, b):
    return lax.dot_general(a, b, NT_DIMS, preferred_element_type=F32)


def _dot_tn(a, b):
    return lax.dot_general(a, b, TN_DIMS, preferred_element_type=F32)


def _dot_f32(a, b):
    return jnp.dot(a, b, preferred_element_type=F32, precision=lax.Precision.HIGHEST)


def _dot_many(a_list, b_list):
    return [_dot(a.astype(BF16), b.astype(BF16)) for a, b in zip(a_list, b_list)]


def _silu(x):
    return x * jax.nn.sigmoid(x)


def _softplus(x):
    return jnp.maximum(x, 0.0) + jnp.log1p(jnp.exp(-jnp.abs(x)))


def _log_sigmoid(x):
    return jnp.minimum(x, 0.0) - jnp.log1p(jnp.exp(-jnp.abs(x)))


def _params(*semantics, vmem_limit_bytes=VMEM_LIMIT_BYTES):
    return pltpu.CompilerParams(dimension_semantics=semantics, vmem_limit_bytes=vmem_limit_bytes)


def _rider_specs(rider, grid):
    steps = grid[0] * grid[1]
    rows, cols = rider.shape
    assert rows % steps == 0, (rows, steps)
    spec = pl.BlockSpec((rows // steps, cols), lambda i, t: (i * grid[1] + t, 0))
    return spec, spec, jax.ShapeDtypeStruct(rider.shape, BF16)


def _tri_masks():
    row = lax.broadcasted_iota(jnp.int32, (CHUNK, CHUNK), 0)
    col = lax.broadcasted_iota(jnp.int32, (CHUNK, CHUNK), 1)
    return row >= col, row > col


def _ada_kernel(c_ref, w_ref, b_ref, o_ref):
    s = _silu(c_ref[...]).astype(BF16)
    o_ref[...] = _dot(s, w_ref[...].astype(BF16)) + b_ref[...]


def _ada_mod(c_all, w_ada, b_ada):
    rows = c_all.shape[0]
    n = w_ada.shape[1]
    tn = 1024
    return pl.pallas_call(
        _ada_kernel,
        grid=(n // tn,),
        in_specs=[
            pl.BlockSpec((rows, D_MODEL), lambda j: (0, 0)),
            pl.BlockSpec((D_MODEL, tn), lambda j: (0, j)),
            pl.BlockSpec((1, tn), lambda j: (0, j)),
        ],
        out_specs=pl.BlockSpec((rows, tn), lambda j: (0, j)),
        out_shape=jax.ShapeDtypeStruct((rows, n), F32),
        compiler_params=_params("arbitrary"),
        name="ada_mod",
    )(c_all, w_ada, b_ada.reshape(1, n))


def _in_proj_kernel(x_ref, sc_ref, sh_ref, n1_ref, w_ref, ws_ref, o_ref, os_ref, h_scr):
    bt, lt, d = x_ref.shape

    @pl.when(pl.program_id(2) == 0)
    def _():
        x = x_ref[...]
        y = x * lax.rsqrt(jnp.mean(x * x, axis=-1, keepdims=True) + EPS) * n1_ref[...]
        h = y * (1.0 + sc_ref[...]) + sh_ref[...]
        hb = h.reshape(bt * lt, d).astype(BF16)
        h_scr[...] = hb
        os_ref[...] = _dot_nt(hb, ws_ref[...]).reshape(os_ref.shape)

    o_ref[...] = _dot_nt(h_scr[...], w_ref[...].astype(BF16)).reshape(o_ref.shape)


def _in_proj(x, mod3, mod_row0, norm1, w_in_t, w_small_t, bt, lt):
    b, l, d = x.shape
    tn = 1024
    n_a = GLA_COLS // tn
    mrow = mod_row0 // bt
    grid = (b // bt, l // lt, MAIN_COLS // tn)

    def w_rows(i, t, j):
        start = jnp.where(j < n_a, j * tn, IN_OFF_GDN + (j - n_a) * tn)
        return pl.multiple_of(start, SUBLANES), 0

    return pl.pallas_call(
        _in_proj_kernel,
        grid=grid,
        in_specs=[
            pl.BlockSpec((bt, lt, d), lambda i, t, j: (i, t, 0)),
            pl.BlockSpec((bt, 1, d), lambda i, t, j: (mrow + i, 0, 1)),
            pl.BlockSpec((bt, 1, d), lambda i, t, j: (mrow + i, 0, 0)),
            pl.BlockSpec((1, 1, d), lambda i, t, j: (0, 0, 0)),
            pl.BlockSpec((pl.Element(tn), pl.Element(d)), w_rows),
            pl.BlockSpec((SMALL_COLS, d), lambda i, t, j: (0, 0)),
        ],
        out_specs=[
            pl.BlockSpec((bt, lt, tn), lambda i, t, j: (i, t, j)),
            pl.BlockSpec((bt, lt, SMALL_COLS), lambda i, t, j: (i, t, 0)),
        ],
        out_shape=[
            jax.ShapeDtypeStruct((b, l, MAIN_COLS), F32),
            jax.ShapeDtypeStruct((b, l, SMALL_COLS), F32),
        ],
        scratch_shapes=[pltpu.VMEM((bt * lt, d), BF16)],
        compiler_params=_params("arbitrary", "arbitrary", "arbitrary"),
        name="in_proj",
    )(x, mod3, mod3, norm1.reshape(1, 1, d), w_in_t, w_small_t)


def _gla_levels():
    row = lax.broadcasted_iota(jnp.int32, (CHUNK, CHUNK), 0)
    col = lax.broadcasted_iota(jnp.int32, (CHUNK, CHUNK), 1)
    ops, masks = [], []
    half = CHUNK // 2
    while half >= 1:
        shift = half.bit_length()
        parent_row = jnp.left_shift(jnp.right_shift(row, shift), shift)
        parent_col = jnp.left_shift(jnp.right_shift(col, shift), shift)
        m_row = parent_row + (half - 1)
        right_row = row > m_row
        in_span = (right_row & (col > m_row) & (col <= row)) | (
            jnp.logical_not(right_row) & (col > row) & (col <= m_row))
        ops.append(jnp.where(in_span, 1.0, 0.0))
        masks.append((parent_row == parent_col) & right_row & (col <= m_row))
        half //= 2
    return ops, masks, row == col


def _gla_kernel(q_ref, k_ref, v_ref, r_ref, sm_ref, wg_ref, bg_ref, gn_ref, s0_ref, *rest, n_riders):
    src_refs, (o_ref, s_ref), rest = rest[:n_riders], rest[n_riders:n_riders + 2], rest[n_riders + 2:]
    dst_refs, (st_scr, attn_scr) = rest[:n_riders], rest[n_riders:]
    for src_ref, dst_ref in zip(src_refs, dst_refs):
        dst_ref[...] = src_ref[...].astype(dst_ref.dtype)
    t = pl.program_id(1)
    bt, lt = q_ref.shape[0], q_ref.shape[1]
    n_chunks = lt // CHUNK
    causal, _ = _tri_masks()
    tri = causal.astype(F32)
    scale = GLA_DK ** -0.5

    @pl.when(t == 0)
    def _():
        for b in range(bt):
            for h in range(GLA_HEADS):
                st_scr[b, h] = s0_ref[b, h].T

    wg = wg_ref[...].astype(BF16)
    heads = range(GLA_HEADS)
    kcs = [slice(h * GLA_DK, (h + 1) * GLA_DK) for h in heads]
    vcs = [slice(h * GLA_DV, (h + 1) * GLA_DV) for h in heads]
    items = [(b, pl.ds(c * CHUNK, CHUNK)) for c in range(n_chunks) for b in range(bt)]

    def scaled_q(b, rows):
        return [q_ref[b, rows, kc] * scale for kc in kcs]

    log_a, big_g = [], []
    for b, rows in items:
        a_lr = sm_ref[b, rows, LR_OFF:LR_OFF + GLA_GATE_RANK].astype(BF16)
        la = _log_sigmoid(_dot(a_lr, wg) + bg_ref[...]) / GLA_GATE_NORM
        log_a.append(la)
        big_g.append(_dot_f32(tri, la))
    total = big_g[0][CHUNK - 1:CHUNK, :]
    for g in big_g[1:]:
        total = jnp.minimum(total, g[CHUNK - 1:CHUNK, :])
    mild = jnp.min(total) > -GLA_MILD_LOG_DECAY

    @pl.when(mild)
    def _():
        for n, (b, rows) in enumerate(items):
            g_mid = big_g[n][CHUNK // 2 - 1:CHUNK // 2, :]
            e_q = jnp.exp(big_g[n] - g_mid)
            e_k = jnp.exp(g_mid - big_g[n])
            q = scaled_q(b, rows)
            scores = [_dot_nt((qq * e_q[:, kc]).astype(BF16),
                              (k_ref[b, rows, kc] * e_k[:, kc]).astype(BF16))
                      for qq, kc in zip(q, kcs)]
            for h in heads:
                attn_scr[n, h] = jnp.where(causal, scores[h], 0.0)

    @pl.when(jnp.logical_not(mild))
    def _():
        level_ops, level_masks, diag = _gla_levels()
        level_op = jnp.concatenate(level_ops, axis=0)
        for n, (b, rows) in enumerate(items):
            sums = _dot_f32(level_op, log_a[n])
            q = scaled_q(b, rows)
            k = [k_ref[b, rows, kc] for kc in kcs]
            scores = [jnp.where(diag, _dot_nt(qq.astype(BF16), kk.astype(BF16)), 0.0)
                      for qq, kk in zip(q, k)]
            for i, mask in enumerate(level_masks):
                e = jnp.exp(sums[i * CHUNK:(i + 1) * CHUNK])
                part = [_dot_nt((qq * e[:, kc]).astype(BF16), (kk * e[:, kc]).astype(BF16))
                        for qq, kk, kc in zip(q, k, kcs)]
                scores = [jnp.where(mask, p, a) for p, a in zip(part, scores)]
            for h in heads:
                attn_scr[n, h] = scores[h]

    pending = []
    for n, (b, rows) in enumerate(items):
        g_last = big_g[n][CHUNK - 1:CHUNK, :]
        e_g = jnp.exp(big_g[n])
        e_kl = jnp.exp(g_last - big_g[n])
        e_l = jnp.exp(g_last)
        q = scaled_q(b, rows)
        v = [v_ref[b, rows, vc].astype(BF16) for vc in vcs]
        v_k = [_dot_tn(vv, (k_ref[b, rows, kc] * e_kl[:, kc]).astype(BF16)) for vv, kc in zip(v, kcs)]
        a_v = [_dot(attn_scr[n, h].astype(BF16), v[h]) for h in heads]
        q_g = [(qq * e_g[:, kc]).astype(BF16) for qq, kc in zip(q, kcs)]
        pending.append((q_g, a_v, v_k, [e_l[:, kc] for kc in kcs]))
    for c in range(n_chunks):
        group = range(c * bt, (c + 1) * bt)
        st = [[st_scr[items[n][0], h] for h in heads] for n in group]
        q_s = [[_dot_nt(qq, ss.astype(BF16)) for qq, ss in zip(pending[n][0], st_n)]
               for n, st_n in zip(group, st)]
        for n, st_n, q_s_n in zip(group, st, q_s):
            b, rows = items[n]
            _, a_v, v_k, e_l = pending[n]
            for h in heads:
                st_scr[b, h] = st_n[h] * e_l[h] + v_k[h]
                o = a_v[h] + q_s_n[h]
                o = o * lax.rsqrt(jnp.mean(o * o, axis=-1, keepdims=True) + EPS) * gn_ref[...]
                o = o * _silu(r_ref[b, rows, vcs[h]])
                o_ref[b, rows, vcs[h]] = o.astype(o_ref.dtype)

    @pl.when(t == pl.num_programs(1) - 1)
    def _():
        for b in range(bt):
            for h in range(GLA_HEADS):
                s_ref[b, h] = st_scr[b, h].T


def _gla_mix(proj, small, s0, gla_wg, gla_bg, gla_norm, bt, lt, riders=()):
    b, l, _ = proj.shape
    kw = GLA_HEADS * GLA_DK
    grid = (b // bt, l // lt)
    rider_args = list(riders)
    riders = [_rider_specs(r, grid) for r in rider_args]
    return pl.pallas_call(
        functools.partial(_gla_kernel, n_riders=len(riders)),
        grid=grid,
        in_specs=[
            pl.BlockSpec((bt, lt, kw), lambda i, t: (i, t, 0)),
            pl.BlockSpec((bt, lt, kw), lambda i, t: (i, t, 1)),
            pl.BlockSpec((bt, lt, GLA_WIDTH), lambda i, t: (i, t, 1)),
            pl.BlockSpec((bt, lt, GLA_WIDTH), lambda i, t: (i, t, 2)),
            pl.BlockSpec((bt, lt, SMALL_COLS), lambda i, t: (i, t, 0)),
            pl.BlockSpec((GLA_GATE_RANK, kw), lambda i, t: (0, 0)),
            pl.BlockSpec((1, kw), lambda i, t: (0, 0)),
            pl.BlockSpec((1, GLA_DV), lambda i, t: (0, 0)),
            pl.BlockSpec((bt, GLA_HEADS, GLA_DK, GLA_DV), lambda i, t: (i, 0, 0, 0)),
        ] + [r[0] for r in riders],
        out_specs=[
            pl.BlockSpec((bt, lt, GLA_WIDTH), lambda i, t: (i, t, 0)),
            pl.BlockSpec((bt, GLA_HEADS, GLA_DK, GLA_DV), lambda i, t: (i, 0, 0, 0)),
        ] + [r[1] for r in riders],
        out_shape=[
            jax.ShapeDtypeStruct((b, l, GLA_WIDTH), BF16),
            jax.ShapeDtypeStruct((b, GLA_HEADS, GLA_DK, GLA_DV), F32),
        ] + [r[2] for r in riders],
        scratch_shapes=[pltpu.VMEM((bt, GLA_HEADS, GLA_DV, GLA_DK), F32),
                        pltpu.VMEM((bt * (lt // CHUNK), GLA_HEADS, CHUNK, CHUNK), F32)],
        compiler_params=_params("arbitrary", "arbitrary"),
        name="gla_mix",
    )(proj, proj, proj, proj, small, gla_wg, gla_bg.reshape(1, kw), gla_norm.reshape(1, GLA_DV), s0,
      *rider_args)


CONV_PAD = SUBLANES
INV_BASE_LOG2 = 3
GDN_CHUNK_GROUP = 4


def _inverse_masks():
    row = lax.broadcasted_iota(jnp.int32, (CHUNK, CHUNK), 0)
    col = lax.broadcasted_iota(jnp.int32, (CHUNK, CHUNK), 1)

    def same_block(log2):
        return jnp.right_shift(row, log2) == jnp.right_shift(col, log2)

    base = same_block(INV_BASE_LOG2)
    merges = []
    log2 = INV_BASE_LOG2
    while (1 << log2) < CHUNK:
        merges.append(same_block(log2 + 1) & jnp.logical_not(same_block(log2)))
        log2 += 1
    return base, merges


def _unit_lower_inverse_minus_eye(lower, base, merges):
    neg = [jnp.where(base, -l, 0.0) for l in lower]
    x = neg
    p = neg
    for _ in range(INV_BASE_LOG2 - 1):
        p = _dot_many(p, p)
        xp = _dot_many(x, p)
        x = [a + b + c for a, b, c in zip(x, p, xp)]
    for m in merges:
        c = [jnp.where(m, l, 0.0) for l in lower]
        w = [a + b for a, b in zip(c, _dot_many(x, c))]
        wx = _dot_many(w, x)
        x = [a - (b + d) for a, b, d in zip(x, w, wx)]
    return x


def _gdn_kernel(x_ref, g_ref, sm_ref, cw_ref, ga_ref, gn_ref, c0_ref, s0_ref, *rest, n_riders):
    src_refs, (o_ref, s_ref, c_ref), rest = rest[:n_riders], rest[n_riders:n_riders + 3], rest[n_riders + 3:]
    dst_refs, (xbuf,) = rest[:n_riders], rest[n_riders:]
    for src_ref, dst_ref in zip(src_refs, dst_refs):
        dst_ref[...] = src_ref[...].astype(dst_ref.dtype)
    t = pl.program_id(1)
    bt, lt = x_ref.shape[0], x_ref.shape[1]
    causal, strict = _tri_masks()
    tri = causal.astype(F32)
    inv_base, inv_merges = _inverse_masks()
    scale = GDN_DK ** -0.5

    @pl.when(t == 0)
    def _():
        xbuf[:, 0:CONV_PAD, :] = jnp.zeros((bt, CONV_PAD, GDN_CONV_DIM), F32)
        xbuf[:, CONV_PAD - (GDN_CONV - 1):CONV_PAD, :] = c0_ref[...]
        s_ref[...] = s0_ref[...]

    @pl.when(t > 0)
    def _():
        xbuf[:, 0:CONV_PAD, :] = xbuf[:, lt:lt + CONV_PAD, :]

    xbuf[:, CONV_PAD:CONV_PAD + lt, :] = x_ref[...]
    c_ref[...] = xbuf[:, lt + CONV_PAD - (GDN_CONV - 1):lt + CONV_PAD, :]

    def conv_silu(b, r0, cols):
        acc = cw_ref[GDN_CONV - 1:GDN_CONV, cols] * xbuf[b, r0 + CONV_PAD:r0 + CONV_PAD + CHUNK, cols]
        for tap in range(GDN_CONV - 1):
            start = r0 + CONV_PAD - (GDN_CONV - 1 - tap)
            acc = acc + cw_ref[tap:tap + 1, cols] * xbuf[b, start:start + CHUNK, cols]
        return _silu(acc)

    def l2norm(x):
        return x * lax.rsqrt(jnp.sum(x * x, axis=-1, keepdims=True) + EPS)

    heads = range(GDN_HEADS)
    def prep_gates(b, c):
        sm = sm_ref[b, pl.ds(c * CHUNK, CHUNK), :]
        gate = -jnp.exp(ga_ref[0:1, :]) * _softplus(sm + ga_ref[1:2, :])
        big_g = _dot_f32(tri, gate)
        return jax.nn.sigmoid(sm), big_g, big_g.T

    def prep_head(b, c, gates, h):
        beta, big_g, big_gt = gates
        r0 = c * CHUNK
        g_col = big_g[:, ALPHA_OFF + h:ALPHA_OFF + h + 1]
        g_row = big_gt[ALPHA_OFF + h:ALPHA_OFF + h + 1, :]
        b_col = beta[:, BETA_OFF + h:BETA_OFF + h + 1]
        g_last = g_col[CHUNK - 1:CHUNK, :]
        e_g = jnp.exp(g_col)
        q = l2norm(conv_silu(b, r0, slice(h * GDN_DK, (h + 1) * GDN_DK))) * scale
        k = l2norm(conv_silu(b, r0, slice(GDN_WIDTH + h * GDN_DK, GDN_WIDTH + (h + 1) * GDN_DK)))
        v = conv_silu(b, r0, slice(2 * GDN_WIDTH + h * GDN_DV, 2 * GDN_WIDTH + (h + 1) * GDN_DV))
        kb = k * b_col
        return dict(
            decay=jnp.where(causal, jnp.exp(g_col - g_row), 0.0),
            rhs=jnp.concatenate([v * b_col, kb * e_g], axis=-1),
            q16=q.astype(BF16), k16=k.astype(BF16), kb16=kb.astype(BF16),
            q_g=(q * e_g).astype(BF16),
            k_dec=(k * jnp.exp(g_last - g_col)).astype(BF16),
            e_last=jnp.exp(g_last))

    def key_products(p):
        kk_t = [_dot_nt(ph["kb16"], ph["k16"]) for ph in p]
        qk_t = [_dot_nt(ph["q16"], ph["k16"]) for ph in p]
        lower = [jnp.where(strict, m * ph["decay"], 0.0) for m, ph in zip(kk_t, p)]
        attn = [jnp.where(causal, m * ph["decay"], 0.0).astype(BF16) for m, ph in zip(qk_t, p)]
        return lower, attn

    n_chunks = lt // CHUNK
    items = [(b, c) for c in range(n_chunks) for b in range(bt)]
    pending = []
    for i0 in range(0, len(items), GDN_CHUNK_GROUP):
        group = items[i0:i0 + GDN_CHUNK_GROUP]
        gates = [prep_gates(b, c) for b, c in group]
        p = [prep_head(b, c, g, h) for (b, c), g in zip(group, gates) for h in heads]
        lower, attn = key_products(p)
        xinv = _unit_lower_inverse_minus_eye(lower, inv_base, inv_merges)
        rhs = [ph["rhs"] for ph in p]
        sol = [r + xr for r, xr in zip(rhs, _dot_many(xinv, rhs))]
        for n in range(len(group)):
            sl = slice(n * GDN_HEADS, (n + 1) * GDN_HEADS)
            pending.append((attn[sl], [so[:, :GDN_DV] for so in sol[sl]],
                            [so[:, GDN_DV:].astype(BF16) for so in sol[sl]],
                            [ph["q_g"] for ph in p[sl]], [ph["k_dec"] for ph in p[sl]],
                            [ph["e_last"] for ph in p[sl]]))
    for c in range(n_chunks):
        group = range(c * bt, (c + 1) * bt)
        attn, sol_v, sol_k, q_g, k_dec, e_last = (
            [x for n in group for x in pending[n][field]] for field in range(6))
        where = [(items[n][0], h) for n in group for h in heads]
        s = [s_ref[b, h] for b, h in where]
        s16 = [ss.astype(BF16) for ss in s]
        k_s = [_dot(a, ss) for a, ss in zip(sol_k, s16)]
        q_s = [_dot(a, ss) for a, ss in zip(q_g, s16)]
        u16 = [(sv - ks).astype(BF16) for sv, ks in zip(sol_v, k_s)]
        a_u = [_dot(a, uu) for a, uu in zip(attn, u16)]
        k_u = [_dot_tn(kd, uu) for kd, uu in zip(k_dec, u16)]
        rows = pl.ds(c * CHUNK, CHUNK)
        for i, (b, h) in enumerate(where):
            hc = slice(h * GDN_DV, (h + 1) * GDN_DV)
            s_ref[b, h] = e_last[i] * s[i] + k_u[i]
            o = q_s[i] + a_u[i]
            o = o * lax.rsqrt(jnp.mean(o * o, axis=-1, keepdims=True) + EPS) * gn_ref[...]
            o = o * _silu(g_ref[b, rows, hc])
            o_ref[b, rows, hc] = o.astype(o_ref.dtype)


def _gdn_mix(proj, small, c0, s0, conv_w, gate_vec, gdn_norm, bt, lt, riders=()):
    b, l, _ = proj.shape
    grid = (b // bt, l // lt)
    rider_args = list(riders)
    riders = [_rider_specs(r, grid) for r in rider_args]
    return pl.pallas_call(
        functools.partial(_gdn_kernel, n_riders=len(riders)),
        grid=grid,
        in_specs=[
            pl.BlockSpec((bt, lt, GDN_CONV_DIM), lambda i, t: (i, t, 1)),
            pl.BlockSpec((bt, lt, GDN_WIDTH), lambda i, t: (i, t, 6)),
            pl.BlockSpec((bt, lt, SMALL_COLS), lambda i, t: (i, t, 0)),
            pl.BlockSpec((GDN_CONV, GDN_CONV_DIM), lambda i, t: (0, 0)),
            pl.BlockSpec((2, SMALL_COLS), lambda i, t: (0, 0)),
            pl.BlockSpec((1, GDN_DV), lambda i, t: (0, 0)),
            pl.BlockSpec((bt, GDN_CONV - 1, GDN_CONV_DIM), lambda i, t: (i, 0, 0)),
            pl.BlockSpec((bt, GDN_HEADS, GDN_DK, GDN_DV), lambda i, t: (i, 0, 0, 0)),
        ] + [r[0] for r in riders],
        out_specs=[
            pl.BlockSpec((bt, lt, GDN_WIDTH), lambda i, t: (i, t, 0)),
            pl.BlockSpec((bt, GDN_HEADS, GDN_DK, GDN_DV), lambda i, t: (i, 0, 0, 0)),
            pl.BlockSpec((bt, GDN_CONV - 1, GDN_CONV_DIM), lambda i, t: (i, 0, 0)),
        ] + [r[1] for r in riders],
        out_shape=[
            jax.ShapeDtypeStruct((b, l, GDN_WIDTH), BF16),
            jax.ShapeDtypeStruct((b, GDN_HEADS, GDN_DK, GDN_DV), F32),
            jax.ShapeDtypeStruct((b, GDN_CONV - 1, GDN_CONV_DIM), F32),
        ] + [r[2] for r in riders],
        scratch_shapes=[pltpu.VMEM((bt, lt + CONV_PAD, GDN_CONV_DIM), F32)],
        compiler_params=_params("arbitrary", "arbitrary"),
        name="gdn_mix",
    )(proj, proj, small, conv_w, gate_vec, gdn_norm.reshape(1, GDN_DV), c0, s0, *rider_args)


def _out_proj_kernel(x_ref, oa_ref, ob_ref, w_ref, g1_ref, sc_ref, sh_ref, n2_ref, x1_ref, h2_ref,
                     *, n_sub):
    bt, lt, d = x_ref.shape
    if bt == 1:
        ls = lt // n_sub
        subs = [(slice(0, 1), slice(i * ls, (i + 1) * ls)) for i in range(n_sub)]
    else:
        bs = bt // n_sub
        subs = [(slice(i * bs, (i + 1) * bs), slice(0, lt)) for i in range(n_sub)]
    mixes = []
    for bsl, lsl in subs:
        oa = oa_ref[bsl, lsl, :]
        nb, nl, _ = oa.shape
        oa = oa.reshape(nb * nl, GLA_WIDTH)
        ob = ob_ref[bsl, lsl, :].reshape(nb * nl, GDN_WIDTH)
        mix = _dot(oa, w_ref[0:GLA_WIDTH, :]) + _dot(ob, w_ref[GLA_WIDTH:GLA_WIDTH + GDN_WIDTH, :])
        mixes.append(mix.reshape(nb, nl, d))
    for (bsl, lsl), mix in zip(subs, mixes):
        x1 = x_ref[bsl, lsl, :] + g1_ref[bsl] * mix
        x1_ref[bsl, lsl, :] = x1
        y = x1 * lax.rsqrt(jnp.mean(x1 * x1, axis=-1, keepdims=True) + EPS) * n2_ref[...]
        h2_ref[bsl, lsl, :] = (y * (1.0 + sc_ref[bsl]) + sh_ref[bsl]).astype(h2_ref.dtype)


def _out_proj(x, o_a, o_b, mod3, mod_row0, w_out16, norm2, bt, lt):
    b, l, d = x.shape
    mrow = mod_row0 // bt
    return pl.pallas_call(
        functools.partial(_out_proj_kernel, n_sub=2),
        grid=(b // bt, l // lt),
        in_specs=[
            pl.BlockSpec((bt, lt, d), lambda i, t: (i, t, 0)),
            pl.BlockSpec((bt, lt, GLA_WIDTH), lambda i, t: (i, t, 0)),
            pl.BlockSpec((bt, lt, GDN_WIDTH), lambda i, t: (i, t, 0)),
            pl.BlockSpec((GLA_WIDTH + GDN_WIDTH, d), lambda i, t: (0, 0)),
            pl.BlockSpec((bt, 1, d), lambda i, t: (mrow + i, 0, 2)),
            pl.BlockSpec((bt, 1, d), lambda i, t: (mrow + i, 0, 4)),
            pl.BlockSpec((bt, 1, d), lambda i, t: (mrow + i, 0, 3)),
            pl.BlockSpec((1, 1, d), lambda i, t: (0, 0, 0)),
        ],
        out_specs=[
            pl.BlockSpec((bt, lt, d), lambda i, t: (i, t, 0)),
            pl.BlockSpec((bt, lt, d), lambda i, t: (i, t, 0)),
        ],
        out_shape=[
            jax.ShapeDtypeStruct((b, l, d), F32),
            jax.ShapeDtypeStruct((b, l, d), BF16),
        ],
        compiler_params=_params("arbitrary", "arbitrary"),
        name="out_proj",
    )(x, o_a, o_b, w_out16, mod3, mod3, mod3, norm2.reshape(1, 1, d))


FFN_PAD = SUBLANES


def _ffn_kernel(h_ref, halo_ref, wg_ref, wv_ref, wd_ref, cw_ref, cb_ref, st_ref, x1_hbm, g2_ref,
                fn_ref, y_ref, so_ref, gbuf, x1_buf, x1_sem, *, has_halo, n_sub):
    ib = pl.program_id(0)
    t = pl.program_id(1)
    f = pl.program_id(2)
    bt, lt, d = h_ref.shape
    tf = wg_ref.shape[1]
    x1_copy = pltpu.make_async_copy(
        x1_hbm.at[pl.ds(ib * bt, bt), pl.ds(t * lt, lt), :], x1_buf, x1_sem)
    if bt == 1:
        ls = lt // n_sub
        subs = [(slice(0, 1), slice(i * ls, (i + 1) * ls)) for i in range(n_sub)]
    else:
        bs = bt // n_sub
        subs = [(slice(i * bs, (i + 1) * bs), slice(0, lt)) for i in range(n_sub)]

    @pl.when(f == 0)
    def _():
        x1_copy.start()
        y_ref[...] = jnp.zeros(y_ref.shape, F32)

    prev = st_ref[...]
    if has_halo:
        halo_gate = _dot(halo_ref[0], wg_ref[...])
        prev = jnp.where(t == 0, prev, halo_gate[SUBLANES - (FFN_CONV - 1):SUBLANES, :][None])
    gbuf[:, FFN_PAD - (FFN_CONV - 1):FFN_PAD, :] = prev
    gates, vals = [], []
    for bsl, lsl in subs:
        hb = h_ref[bsl, lsl, :]
        nb, nl, _ = hb.shape
        hb = hb.reshape(nb * nl, d)
        gate = _dot(hb, wg_ref[...]).reshape(nb, nl, tf)
        gbuf[bsl, FFN_PAD + lsl.start:FFN_PAD + lsl.stop, :] = gate
        gates.append(gate)
        vals.append(_dot(hb, wv_ref[...]))
    so_ref[...] = gbuf[:, FFN_PAD + lt - (FFN_CONV - 1):FFN_PAD + lt, :].reshape(so_ref.shape)
    acts = []
    for (bsl, lsl), gate, val in zip(subs, gates, vals):
        nb, nl, _ = gate.shape
        conv = cw_ref[FFN_CONV - 1:FFN_CONV, :] * gate
        for tap in range(FFN_CONV - 1):
            back = FFN_CONV - 1 - tap
            conv = conv + cw_ref[tap:tap + 1, :] * gbuf[bsl, FFN_PAD + lsl.start - back:FFN_PAD + lsl.stop - back, :]
        acts.append((_silu(conv + cb_ref[...]).reshape(nb * nl, tf) * val).astype(BF16))
    parts = [_dot(act, wd_ref[...]) for act in acts]
    for (bsl, lsl), gate, part in zip(subs, gates, parts):
        nb, nl, _ = gate.shape
        y_ref[bsl, lsl, :] += part.reshape(nb, nl, d)

    @pl.when(f == pl.num_programs(2) - 1)
    def _():
        x1_copy.wait()
        x2 = x1_buf[...] + g2_ref[...] * y_ref[...]
        y = x2 * lax.rsqrt(jnp.mean(x2 * x2, axis=-1, keepdims=True) + EPS) * fn_ref[...]
        y_ref[...] = y


def _ffn(h2, x1, st0, mod3, mod_row0, w_up16, w_down16, conv_w, conv_b, final_norm, bt, lt):
    b, l, d = h2.shape
    tf = 512
    nf = D_FF // tf
    mrow = mod_row0 // bt
    has_halo = lt < l
    halo_blocks = lt // SUBLANES
    kern = functools.partial(_ffn_kernel, has_halo=has_halo, n_sub=4)
    return pl.pallas_call(
        kern,
        grid=(b // bt, l // lt, nf),
        in_specs=[
            pl.BlockSpec((bt, lt, d), lambda i, t, f: (i, t, 0)),
            pl.BlockSpec((1, SUBLANES, d), lambda i, t, f: (i, jnp.maximum(t * halo_blocks - 1, 0), 0)),
            pl.BlockSpec((d, tf), lambda i, t, f: (0, f)),
            pl.BlockSpec((d, tf), lambda i, t, f: (0, nf + f)),
            pl.BlockSpec((tf, d), lambda i, t, f: (f, 0)),
            pl.BlockSpec((FFN_CONV, tf), lambda i, t, f: (0, f)),
            pl.BlockSpec((1, tf), lambda i, t, f: (0, f)),
            pl.BlockSpec((bt, FFN_CONV - 1, tf), lambda i, t, f: (i, 0, f)),
            pl.BlockSpec(memory_space=pl.ANY),
            pl.BlockSpec((bt, 1, d), lambda i, t, f: (mrow + i, 0, 5)),
            pl.BlockSpec((1, 1, d), lambda i, t, f: (0, 0, 0)),
        ],
        out_specs=[
            pl.BlockSpec((bt, lt, d), lambda i, t, f: (i, t, 0)),
            pl.BlockSpec((bt, 1, FFN_CONV - 1, tf), lambda i, t, f: (i, t, 0, f)),
        ],
        out_shape=[
            jax.ShapeDtypeStruct((b, l, d), F32),
            jax.ShapeDtypeStruct((b, l // lt, FFN_CONV - 1, D_FF), F32),
        ],
        scratch_shapes=[
            pltpu.VMEM((bt, lt + FFN_PAD, tf), F32),
            pltpu.VMEM((bt, lt, d), F32),
            pltpu.SemaphoreType.DMA(()),
        ],
        compiler_params=_params("arbitrary", "arbitrary", "arbitrary",
                                vmem_limit_bytes=FFN_VMEM_LIMIT_BYTES),
        name="ffn",
    )(h2, h2, w_up16, w_up16, w_down16, conv_w, conv_b.reshape(1, D_FF), st0, x1, mod3,
      final_norm.reshape(1, 1, d))


def _trunk(x, mod3, mod_row0, states, weights, cfg):
    (w_in_t, w_small_t, norm1, gla_wg, gla_bg, gla_norm, gdn_conv_w, gate_vec, gdn_norm, w_out,
     norm2, w_up, w_down, ffn_conv_w, ffn_conv_b, final_norm) = weights
    s_gla, s_gdn, s_conv, s_ffn = states
    convert = w_up.dtype != BF16
    proj, small = _in_proj(x, mod3, mod_row0, norm1, w_in_t, w_small_t, cfg["in_bt"], cfg["in_lt"])
    o_a, n_gla, *gla_riders = _gla_mix(proj, small, s_gla, gla_wg, gla_bg, gla_norm, cfg["gla_bt"],
                                       cfg["gla_lt"], riders=[w_down] if convert else [])
    o_b, n_gdn, n_conv, *gdn_riders = _gdn_mix(proj, small, s_conv, s_gdn, gdn_conv_w, gate_vec, gdn_norm,
                                               cfg["gdn_bt"], cfg["gdn_lt"],
                                               riders=[w_up, w_out] if convert else [])
    w_out16, w_up16, w_down16 = (gdn_riders[1], gdn_riders[0], gla_riders[0]) if convert else (
        w_out, w_up, w_down)
    x1, h2 = _out_proj(x, o_a, o_b, mod3, mod_row0, w_out16, norm2, cfg["out_bt"], cfg["out_lt"])
    y, n_ffn = _ffn(h2, x1, s_ffn, mod3, mod_row0, w_up16, w_down16, ffn_conv_w, ffn_conv_b,
                    final_norm, cfg["ffn_bt"], cfg["ffn_lt"])
    return (y, n_gla[None], n_gdn[None], n_conv[None], n_ffn[:, -1][None]), (w_out16, w_up16, w_down16)


def kernel(x_prompt, x_sample, c_prompt, c_sample, state_gla, state_gdn, state_gdn_conv, state_ffn_conv, w_ada, b_ada, norm1, w_in, gla_wg, gla_bg, gla_norm, gdn_conv_w, gdn_a_log, gdn_dt_bias, gdn_norm, w_out, norm2, w_up, ffn_conv_w, ffn_conv_b, w_down, final_norm):
    bp = x_prompt.shape[0]
    bs = x_sample.shape[0]

    w_in_t = jnp.swapaxes(w_in[0], 0, 1)
    w_small_t = jnp.concatenate(
        [w_in_t[IN_OFF_LR:IN_OFF_GDN], w_in_t[IN_OFF_BETA:],
         jnp.zeros((SMALL_COLS - GLA_GATE_RANK - 2 * GDN_HEADS, D_MODEL), F32)], axis=0).astype(BF16)
    gate_vec = jnp.zeros((2, SMALL_COLS), F32)
    gate_vec = gate_vec.at[0, ALPHA_OFF:ALPHA_OFF + GDN_HEADS].set(gdn_a_log[0])
    gate_vec = gate_vec.at[1, ALPHA_OFF:ALPHA_OFF + GDN_HEADS].set(gdn_dt_bias[0])

    c_all = jnp.concatenate([c_sample, c_prompt], axis=0)
    mod = _ada_mod(c_all, w_ada[0], b_ada[0])
    mod3 = mod.reshape(bs + bp, 1, N_MOD * D_MODEL)

    weights = (w_in_t, w_small_t, norm1[0], gla_wg[0], gla_bg[0], gla_norm[0], gdn_conv_w[0], gate_vec,
               gdn_norm[0], w_out[0], norm2[0], w_up[0], w_down[0], ffn_conv_w[0], ffn_conv_b[0],
               final_norm)

    fresh = (jnp.zeros((bp, GLA_HEADS, GLA_DK, GLA_DV), F32),
             jnp.zeros((bp, GDN_HEADS, GDN_DK, GDN_DV), F32),
             jnp.zeros((bp, GDN_CONV - 1, GDN_CONV_DIM), F32),
             jnp.zeros((bp, FFN_CONV - 1, D_FF), F32))
    cfg_p = dict(in_bt=1, in_lt=1024, gla_bt=1, gla_lt=512, gdn_bt=1, gdn_lt=256,
                 out_bt=1, out_lt=512, ffn_bt=1, ffn_lt=1024)
    (y_p, p_gla, p_gdn, p_conv, p_ffn), (w_out16, w_up16, w_down16) = _trunk(
        x_prompt, mod3, bs, fresh, weights, cfg_p)

    carried = (state_gla[0], state_gdn[0], state_gdn_conv[0], state_ffn_conv[0])
    ls = x_sample.shape[1]
    cfg_s = dict(in_bt=bs, in_lt=ls, gla_bt=4, gla_lt=ls, gdn_bt=4, gdn_lt=ls,
                 out_bt=bs // 2, out_lt=ls, ffn_bt=bs, ffn_lt=ls)
    weights_s = weights[:9] + (w_out16, weights[10], w_up16, w_down16) + weights[13:]
    (y_s, s_gla, s_gdn, s_conv, s_ffn), _ = _trunk(x_sample, mod3, 0, carried, weights_s, cfg_s)
    return (y_p, y_s, p_gla, p_gdn, p_conv, p_ffn, s_gla, s_gdn, s_conv, s_ffn)
```

```python
import functools

import jax
import jax.numpy as jnp
from jax import lax
from jax.experimental import pallas as pl
from jax.experimental.pallas import tpu as pltpu

F32 = jnp.float32
BF16 = jnp.bfloat16

D_MODEL = 2048
CHUNK = 64
GLA_HEADS = 4
GLA_DK = 128
GLA_DV = 256
GLA_WIDTH = GLA_HEADS * GLA_DV
GLA_GATE_RANK = 16
GLA_GATE_NORM = 16.0
GLA_MILD_LOG_DECAY = 60.0
GDN_HEADS = 8
GDN_DK = 128
GDN_DV = 128
GDN_WIDTH = GDN_HEADS * GDN_DV
GDN_CONV = 4
GDN_CONV_DIM = 3 * GDN_WIDTH
D_FF = 5632
FFN_CONV = 3
N_MOD = 6
EPS = 1e-6

GLA_COLS = 2 * GLA_HEADS * GLA_DK + 2 * GLA_WIDTH
GDN_COLS = 4 * GDN_WIDTH
MAIN_COLS = GLA_COLS + GDN_COLS
SMALL_COLS = 128
LR_OFF, BETA_OFF, ALPHA_OFF = 0, GLA_GATE_RANK, GLA_GATE_RANK + GDN_HEADS
IN_OFF_LR = GLA_COLS
IN_OFF_GDN = IN_OFF_LR + GLA_GATE_RANK
IN_OFF_BETA = IN_OFF_GDN + GDN_COLS

SUBLANES = 8
VMEM_LIMIT_BYTES = 56 * 1024 * 1024
FFN_VMEM_LIMIT_BYTES = 60 * 1024 * 1024

NT_DIMS = (((1,), (1,)), ((), ()))
TN_DIMS = (((0,), (0,)), ((), ()))


def _dot(a, b):
    return jnp.dot(a, b, preferred_element_type=F32)


def _dot_nt(a, b):
    return lax.dot_general(a, b, NT_DIMS, preferred_element_type=F32)


def _dot_tn(a, b):
    return lax.dot_general(a, b, TN_DIMS, preferred_element_type=F32)


def _dot_f32(a, b):
    return jnp.dot(a, b, preferred_element_type=F32, precision=lax.Precision.HIGHEST)


def _dot_many(a_list, b_list):
    return [_dot(a.astype(BF16), b.astype(BF16)) for a, b in zip(a_list, b_list)]


def _silu(x):
    return x * jax.nn.sigmoid(x)


def _softplus(x):
    return jnp.maximum(x, 0.0) + jnp.log1p(jnp.exp(-jnp.abs(x)))


def _log_sigmoid(x):
    return jnp.minimum(x, 0.0) - jnp.log1p(jnp.exp(-jnp.abs(x)))


def _params(*semantics, vmem_limit_bytes=VMEM_LIMIT_BYTES):
    return pltpu.CompilerParams(dimension_semantics=semantics, vmem_limit_bytes=vmem_limit_bytes)


def _rider_specs(rider, grid):
    steps = grid[0] * grid[1]
    rows, cols = rider.shape
    assert rows % steps == 0, (rows, steps)
    spec = pl.BlockSpec((rows // steps, cols), lambda i, t: (i * grid[1] + t, 0))
    return spec, spec, jax.ShapeDtypeStruct(rider.shape, BF16)


def _tri_masks():
    row = lax.broadcasted_iota(jnp.int32, (CHUNK, CHUNK), 0)
    col = lax.broadcasted_iota(jnp.int32, (CHUNK, CHUNK), 1)
    return row >= col, row > col


def _ada_kernel(c_ref, w_ref, b_ref, o_ref):
    s = _silu(c_ref[...]).astype(BF16)
    o_ref[...] = _dot(s, w_ref[...].astype(BF16)) + b_ref[...]


def _ada_mod(c_all, w_ada, b_ada):
    rows = c_all.shape[0]
    n = w_ada.shape[1]
    tn = 1024
    return pl.pallas_call(
        _ada_kernel,
        grid=(n // tn,),
        in_specs=[
            pl.BlockSpec((rows, D_MODEL), lambda j: (0, 0)),
            pl.BlockSpec((D_MODEL, tn), lambda j: (0, j)),
            pl.BlockSpec((1, tn), lambda j: (0, j)),
        ],
        out_specs=pl.BlockSpec((rows, tn), lambda j: (0, j)),
        out_shape=jax.ShapeDtypeStruct((rows, n), F32),
        compiler_params=_params("arbitrary"),
        name="ada_mod",
    )(c_all, w_ada, b_ada.reshape(1, n))


def _in_proj_kernel(x_ref, sc_ref, sh_ref, n1_ref, w_ref, ws_ref, o_ref, os_ref, h_scr):
    bt, lt, d = x_ref.shape

    @pl.when(pl.program_id(2) == 0)
    def _():
        x = x_ref[...]
        y = x * lax.rsqrt(jnp.mean(x * x, axis=-1, keepdims=True) + EPS) * n1_ref[...]
        h = y * (1.0 + sc_ref[...]) + sh_ref[...]
        hb = h.reshape(bt * lt, d).astype(BF16)
        h_scr[...] = hb
        os_ref[...] = _dot_nt(hb, ws_ref[...].astype(BF16)).reshape(os_ref.shape)

    o_ref[...] = _dot_nt(h_scr[...], w_ref[...].astype(BF16)).reshape(o_ref.shape)


def _in_proj(x, mod3, mod_row0, norm1, w_in_t, w_small_t, bt, lt):
    b, l, d = x.shape
    tn = 1024
    n_a = GLA_COLS // tn
    mrow = mod_row0 // bt
    grid = (b // bt, l // lt, MAIN_COLS // tn)

    def w_rows(i, t, j):
        start = jnp.where(j < n_a, j * tn, IN_OFF_GDN + (j - n_a) * tn)
        return pl.multiple_of(start, SUBLANES), 0

    return pl.pallas_call(
        _in_proj_kernel,
        grid=grid,
        in_specs=[
            pl.BlockSpec((bt, lt, d), lambda i, t, j: (i, t, 0)),
            pl.BlockSpec((bt, 1, d), lambda i, t, j: (mrow + i, 0, 1)),
            pl.BlockSpec((bt, 1, d), lambda i, t, j: (mrow + i, 0, 0)),
            pl.BlockSpec((1, 1, d), lambda i, t, j: (0, 0, 0)),
            pl.BlockSpec((pl.Element(tn), pl.Element(d)), w_rows),
            pl.BlockSpec((SMALL_COLS, d), lambda i, t, j: (0, 0)),
        ],
        out_specs=[
            pl.BlockSpec((bt, lt, tn), lambda i, t, j: (i, t, j)),
            pl.BlockSpec((bt, lt, SMALL_COLS), lambda i, t, j: (i, t, 0)),
        ],
        out_shape=[
            jax.ShapeDtypeStruct((b, l, MAIN_COLS), F32),
            jax.ShapeDtypeStruct((b, l, SMALL_COLS), F32),
        ],
        scratch_shapes=[pltpu.VMEM((bt * lt, d), BF16)],
        compiler_params=_params("arbitrary", "arbitrary", "arbitrary"),
        name="in_proj",
    )(x, mod3, mod3, norm1.reshape(1, 1, d), w_in_t, w_small_t)


def _gla_levels():
    row = lax.broadcasted_iota(jnp.int32, (CHUNK, CHUNK), 0)
    col = lax.broadcasted_iota(jnp.int32, (CHUNK, CHUNK), 1)
    ops, masks = [], []
    half = CHUNK // 2
    while half >= 1:
        shift = half.bit_length()
        parent_row = jnp.left_shift(jnp.right_shift(row, shift), shift)
        parent_col = jnp.left_shift(jnp.right_shift(col, shift), shift)
        m_row = parent_row + (half - 1)
        right_row = row > m_row
        in_span = (right_row & (col > m_row) & (col <= row)) | (
            jnp.logical_not(right_row) & (col > row) & (col <= m_row))
        ops.append(jnp.where(in_span, 1.0, 0.0))
        masks.append((parent_row == parent_col) & right_row & (col <= m_row))
        half //= 2
    return ops, masks, row == col


def _gla_kernel(q_ref, k_ref, v_ref, r_ref, sm_ref, wg_ref, bg_ref, gn_ref, s0_ref, *rest, n_riders):
    src_refs, (o_ref, s_ref), rest = rest[:n_riders], rest[n_riders:n_riders + 2], rest[n_riders + 2:]
    dst_refs, (st_scr, attn_scr) = rest[:n_riders], rest[n_riders:]
    for src_ref, dst_ref in zip(src_refs, dst_refs):
        dst_ref[...] = src_ref[...].astype(dst_ref.dtype)
    t = pl.program_id(1)
    bt, lt = q_ref.shape[0], q_ref.shape[1]
    n_chunks = lt // CHUNK
    causal, _ = _tri_masks()
    tri = causal.astype(F32)
    scale = GLA_DK ** -0.5

    @pl.when(t == 0)
    def _():
        for b in range(bt):
            for h in range(GLA_HEADS):
                st_scr[b, h] = s0_ref[b, h].T

    wg = wg_ref[...].astype(BF16)
    heads = range(GLA_HEADS)
    kcs = [slice(h * GLA_DK, (h + 1) * GLA_DK) for h in heads]
    vcs = [slice(h * GLA_DV, (h + 1) * GLA_DV) for h in heads]
    items = [(b, pl.ds(c * CHUNK, CHUNK)) for c in range(n_chunks) for b in range(bt)]

    def scaled_q(b, rows):
        return [q_ref[b, rows, kc] * scale for kc in kcs]

    log_a, big_g = [], []
    for b, rows in items:
        a_lr = sm_ref[b, rows, LR_OFF:LR_OFF + GLA_GATE_RANK].astype(BF16)
        la = _log_sigmoid(_dot(a_lr, wg) + bg_ref[...]) / GLA_GATE_NORM
        log_a.append(la)
        big_g.append(_dot_f32(tri, la))
    total = big_g[0][CHUNK - 1:CHUNK, :]
    for g in big_g[1:]:
        total = jnp.minimum(total, g[CHUNK - 1:CHUNK, :])
    mild = jnp.min(total) > -GLA_MILD_LOG_DECAY

    @pl.when(mild)
    def _():
        for n, (b, rows) in enumerate(items):
            g_mid = big_g[n][CHUNK // 2 - 1:CHUNK // 2, :]
            e_q = jnp.exp(big_g[n] - g_mid)
            e_k = jnp.exp(g_mid - big_g[n])
            q = scaled_q(b, rows)
            scores = [_dot_nt((qq * e_q[:, kc]).astype(BF16),
                              (k_ref[b, rows, kc] * e_k[:, kc]).astype(BF16))
                      for qq, kc in zip(q, kcs)]
            for h in heads:
                attn_scr[n, h] = jnp.where(causal, scores[h], 0.0)

    @pl.when(jnp.logical_not(mild))
    def _():
        level_ops, level_masks, diag = _gla_levels()
        level_op = jnp.concatenate(level_ops, axis=0)
        for n, (b, rows) in enumerate(items):
            sums = _dot_f32(level_op, log_a[n])
            q = scaled_q(b, rows)
            k = [k_ref[b, rows, kc] for kc in kcs]
            scores = [jnp.where(diag, _dot_nt(qq.astype(BF16), kk.astype(BF16)), 0.0)
                      for qq, kk in zip(q, k)]
            for i, mask in enumerate(level_masks):
                e = jnp.exp(sums[i * CHUNK:(i + 1) * CHUNK])
                part = [_dot_nt((qq * e[:, kc]).astype(BF16), (kk * e[:, kc]).astype(BF16))
                        for qq, kk, kc in zip(q, k, kcs)]
                scores = [jnp.where(mask, p, a) for p, a in zip(part, scores)]
            for h in heads:
                attn_scr[n, h] = scores[h]

    pending = []
    for n, (b, rows) in enumerate(items):
        g_last = big_g[n][CHUNK - 1:CHUNK, :]
        e_g = jnp.exp(big_g[n])
        e_kl = jnp.exp(g_last - big_g[n])
        e_l = jnp.exp(g_last)
        q = scaled_q(b, rows)
        v = [v_ref[b, rows, vc].astype(BF16) for vc in vcs]
        v_k = [_dot_tn(vv, (k_ref[b, rows, kc] * e_kl[:, kc]).astype(BF16)) for vv, kc in zip(v, kcs)]
        a_v = [_dot(attn_scr[n, h].astype(BF16), v[h]) for h in heads]
        q_g = [(qq * e_g[:, kc]).astype(BF16) for qq, kc in zip(q, kcs)]
        pending.append((q_g, a_v, v_k, [e_l[:, kc] for kc in kcs]))
    for c in range(n_chunks):
        group = range(c * bt, (c + 1) * bt)
        st = [[st_scr[items[n][0], h] for h in heads] for n in group]
        q_s = [[_dot_nt(qq, ss.astype(BF16)) for qq, ss in zip(pending[n][0], st_n)]
               for n, st_n in zip(group, st)]
        for n, st_n, q_s_n in zip(group, st, q_s):
            b, rows = items[n]
            _, a_v, v_k, e_l = pending[n]
            for h in heads:
                st_scr[b, h] = st_n[h] * e_l[h] + v_k[h]
                o = a_v[h] + q_s_n[h]
                o = o * lax.rsqrt(jnp.mean(o * o, axis=-1, keepdims=True) + EPS) * gn_ref[...]
                o = o * _silu(r_ref[b, rows, vcs[h]])
                o_ref[b, rows, vcs[h]] = o.astype(o_ref.dtype)

    @pl.when(t == pl.num_programs(1) - 1)
    def _():
        for b in range(bt):
            for h in range(GLA_HEADS):
                s_ref[b, h] = st_scr[b, h].T


def _gla_mix(proj, small, s0, gla_wg, gla_bg, gla_norm, bt, lt, riders=()):
    b, l, _ = proj.shape
    kw = GLA_HEADS * GLA_DK
    grid = (b // bt, l // lt)
    rider_args = list(riders)
    riders = [_rider_specs(r, grid) for r in rider_args]
    return pl.pallas_call(
        functools.partial(_gla_kernel, n_riders=len(riders)),
        grid=grid,
        in_specs=[
            pl.BlockSpec((bt, lt, kw), lambda i, t: (i, t, 0)),
            pl.BlockSpec((bt, lt, kw), lambda i, t: (i, t, 1)),
            pl.BlockSpec((bt, lt, GLA_WIDTH), lambda i, t: (i, t, 1)),
            pl.BlockSpec((bt, lt, GLA_WIDTH), lambda i, t: (i, t, 2)),
            pl.BlockSpec((bt, lt, SMALL_COLS), lambda i, t: (i, t, 0)),
            pl.BlockSpec((GLA_GATE_RANK, kw), lambda i, t: (0, 0)),
            pl.BlockSpec((1, kw), lambda i, t: (0, 0)),
            pl.BlockSpec((1, GLA_DV), lambda i, t: (0, 0)),
            pl.BlockSpec((bt, GLA_HEADS, GLA_DK, GLA_DV), lambda i, t: (i, 0, 0, 0)),
        ] + [r[0] for r in riders],
        out_specs=[
            pl.BlockSpec((bt, lt, GLA_WIDTH), lambda i, t: (i, t, 0)),
            pl.BlockSpec((bt, GLA_HEADS, GLA_DK, GLA_DV), lambda i, t: (i, 0, 0, 0)),
        ] + [r[1] for r in riders],
        out_shape=[
            jax.ShapeDtypeStruct((b, l, GLA_WIDTH), BF16),
            jax.ShapeDtypeStruct((b, GLA_HEADS, GLA_DK, GLA_DV), F32),
        ] + [r[2] for r in riders],
        scratch_shapes=[pltpu.VMEM((bt, GLA_HEADS, GLA_DV, GLA_DK), F32),
                        pltpu.VMEM((bt * (lt // CHUNK), GLA_HEADS, CHUNK, CHUNK), F32)],
        compiler_params=_params("arbitrary", "arbitrary"),
        name="gla_mix",
    )(proj, proj, proj, proj, small, gla_wg, gla_bg.reshape(1, kw), gla_norm.reshape(1, GLA_DV), s0,
      *rider_args)


CONV_PAD = SUBLANES
INV_BASE_LOG2 = 3
GDN_CHUNK_GROUP = 4


def _inverse_masks():
    row = lax.broadcasted_iota(jnp.int32, (CHUNK, CHUNK), 0)
    col = lax.broadcasted_iota(jnp.int32, (CHUNK, CHUNK), 1)

    def same_block(log2):
        return jnp.right_shift(row, log2) == jnp.right_shift(col, log2)

    base = same_block(INV_BASE_LOG2)
    merges = []
    log2 = INV_BASE_LOG2
    while (1 << log2) < CHUNK:
        merges.append(same_block(log2 + 1) & jnp.logical_not(same_block(log2)))
        log2 += 1
    return base, merges


def _unit_lower_inverse_minus_eye(lower, base, merges):
    neg = [jnp.where(base, -l, 0.0) for l in lower]
    x = neg
    p = neg
    for _ in range(INV_BASE_LOG2 - 1):
        p = _dot_many(p, p)
        xp = _dot_many(x, p)
        x = [a + b + c for a, b, c in zip(x, p, xp)]
    for m in merges:
        c = [jnp.where(m, l, 0.0) for l in lower]
        w = [a + b for a, b in zip(c, _dot_many(x, c))]
        wx = _dot_many(w, x)
        x = [a - (b + d) for a, b, d in zip(x, w, wx)]
    return x


def _gdn_kernel(x_ref, g_ref, sm_ref, cw_ref, ga_ref, gn_ref, c0_ref, s0_ref, *rest, n_riders):
    src_refs, (o_ref, s_ref, c_ref), rest = rest[:n_riders], rest[n_riders:n_riders + 3], rest[n_riders + 3:]
    dst_refs, (xbuf,) = rest[:n_riders], rest[n_riders:]
    for src_ref, dst_ref in zip(src_refs, dst_refs):
        dst_ref[...] = src_ref[...].astype(dst_ref.dtype)
    t = pl.program_id(1)
    bt, lt = x_ref.shape[0], x_ref.shape[1]
    causal, strict = _tri_masks()
    tri = causal.astype(F32)
    inv_base, inv_merges = _inverse_masks()
    scale = GDN_DK ** -0.5

    @pl.when(t == 0)
    def _():
        xbuf[:, 0:CONV_PAD, :] = jnp.zeros((bt, CONV_PAD, GDN_CONV_DIM), F32)
        xbuf[:, CONV_PAD - (GDN_CONV - 1):CONV_PAD, :] = c0_ref[...]
        s_ref[...] = s0_ref[...]

    @pl.when(t > 0)
    def _():
        xbuf[:, 0:CONV_PAD, :] = xbuf[:, lt:lt + CONV_PAD, :]

    xbuf[:, CONV_PAD:CONV_PAD + lt, :] = x_ref[...]
    c_ref[...] = xbuf[:, lt + CONV_PAD - (GDN_CONV - 1):lt + CONV_PAD, :]

    def conv_silu(b, r0, cols):
        acc = cw_ref[GDN_CONV - 1:GDN_CONV, cols] * xbuf[b, r0 + CONV_PAD:r0 + CONV_PAD + CHUNK, cols]
        for tap in range(GDN_CONV - 1):
            start = r0 + CONV_PAD - (GDN_CONV - 1 - tap)
            acc = acc + cw_ref[tap:tap + 1, cols] * xbuf[b, start:start + CHUNK, cols]
        return _silu(acc)

    def l2norm(x):
        return x * lax.rsqrt(jnp.sum(x * x, axis=-1, keepdims=True) + EPS)

    heads = range(GDN_HEADS)
    def prep_gates(b, c):
        sm = sm_ref[b, pl.ds(c * CHUNK, CHUNK), :]
        gate = -jnp.exp(ga_ref[0:1, :]) * _softplus(sm + ga_ref[1:2, :])
        big_g = _dot_f32(tri, gate)
        return jax.nn.sigmoid(sm), big_g, big_g.T

    def prep_head(b, c, gates, h):
        beta, big_g, big_gt = gates
        r0 = c * CHUNK
        g_col = big_g[:, ALPHA_OFF + h:ALPHA_OFF + h + 1]
        g_row = big_gt[ALPHA_OFF + h:ALPHA_OFF + h + 1, :]
        b_col = beta[:, BETA_OFF + h:BETA_OFF + h + 1]
        g_last = g_col[CHUNK - 1:CHUNK, :]
        e_g = jnp.exp(g_col)
        q = l2norm(conv_silu(b, r0, slice(h * GDN_DK, (h + 1) * GDN_DK))) * scale
        k = l2norm(conv_silu(b, r0, slice(GDN_WIDTH + h * GDN_DK, GDN_WIDTH + (h + 1) * GDN_DK)))
        v = conv_silu(b, r0, slice(2 * GDN_WIDTH + h * GDN_DV, 2 * GDN_WIDTH + (h + 1) * GDN_DV))
        kb = k * b_col
        return dict(
            decay=jnp.where(causal, jnp.exp(g_col - g_row), 0.0),
            rhs=jnp.concatenate([v * b_col, kb * e_g], axis=-1),
            q16=q.astype(BF16), k16=k.astype(BF16), kb16=kb.astype(BF16),
            q_g=(q * e_g).astype(BF16),
            k_dec=(k * jnp.exp(g_last - g_col)).astype(BF16),
            e_last=jnp.exp(g_last))

    def key_products(p):
        kk_t = [_dot_nt(ph["kb16"], ph["k16"]) for ph in p]
        qk_t = [_dot_nt(ph["q16"], ph["k16"]) for ph in p]
        lower = [jnp.where(strict, m * ph["decay"], 0.0) for m, ph in zip(kk_t, p)]
        attn = [jnp.where(causal, m * ph["decay"], 0.0).astype(BF16) for m, ph in zip(qk_t, p)]
        return lower, attn

    n_chunks = lt // CHUNK
    items = [(b, c) for c in range(n_chunks) for b in range(bt)]
    pending = []
    for i0 in range(0, len(items), GDN_CHUNK_GROUP):
        group = items[i0:i0 + GDN_CHUNK_GROUP]
        gates = [prep_gates(b, c) for b, c in group]
        p = [prep_head(b, c, g, h) for (b, c), g in zip(group, gates) for h in heads]
        lower, attn = key_products(p)
        xinv = _unit_lower_inverse_minus_eye(lower, inv_base, inv_merges)
        rhs = [ph["rhs"] for ph in p]
        sol = [r + xr for r, xr in zip(rhs, _dot_many(xinv, rhs))]
        for n in range(len(group)):
            sl = slice(n * GDN_HEADS, (n + 1) * GDN_HEADS)
            pending.append((attn[sl], [so[:, :GDN_DV] for so in sol[sl]],
                            [so[:, GDN_DV:].astype(BF16) for so in sol[sl]],
                            [ph["q_g"] for ph in p[sl]], [ph["k_dec"] for ph in p[sl]],
                            [ph["e_last"] for ph in p[sl]]))
    for c in range(n_chunks):
        group = range(c * bt, (c + 1) * bt)
        attn, sol_v, sol_k, q_g, k_dec, e_last = (
            [x for n in group for x in pending[n][field]] for field in range(6))
        where = [(items[n][0], h) for n in group for h in heads]
        s = [s_ref[b, h] for b, h in where]
        s16 = [ss.astype(BF16) for ss in s]
        k_s = [_dot(a, ss) for a, ss in zip(sol_k, s16)]
        q_s = [_dot(a, ss) for a, ss in zip(q_g, s16)]
        u16 = [(sv - ks).astype(BF16) for sv, ks in zip(sol_v, k_s)]
        a_u = [_dot(a, uu) for a, uu in zip(attn, u16)]
        k_u = [_dot_tn(kd, uu) for kd, uu in zip(k_dec, u16)]
        rows = pl.ds(c * CHUNK, CHUNK)
        for i, (b, h) in enumerate(where):
            hc = slice(h * GDN_DV, (h + 1) * GDN_DV)
            s_ref[b, h] = e_last[i] * s[i] + k_u[i]
            o = q_s[i] + a_u[i]
            o = o * lax.rsqrt(jnp.mean(o * o, axis=-1, keepdims=True) + EPS) * gn_ref[...]
            o = o * _silu(g_ref[b, rows, hc])
            o_ref[b, rows, hc] = o.astype(o_ref.dtype)


def _gdn_mix(proj, small, c0, s0, conv_w, gate_vec, gdn_norm, bt, lt, riders=()):
    b, l, _ = proj.shape
    grid = (b // bt, l // lt)
    rider_args = list(riders)
    riders = [_rider_specs(r, grid) for r in rider_args]
    return pl.pallas_call(
        functools.partial(_gdn_kernel, n_riders=len(riders)),
        grid=grid,
        in_specs=[
            pl.BlockSpec((bt, lt, GDN_CONV_DIM), lambda i, t: (i, t, 1)),
            pl.BlockSpec((bt, lt, GDN_WIDTH), lambda i, t: (i, t, 6)),
            pl.BlockSpec((bt, lt, SMALL_COLS), lambda i, t: (i, t, 0)),
            pl.BlockSpec((GDN_CONV, GDN_CONV_DIM), lambda i, t: (0, 0)),
            pl.BlockSpec((2, SMALL_COLS), lambda i, t: (0, 0)),
            pl.BlockSpec((1, GDN_DV), lambda i, t: (0, 0)),
            pl.BlockSpec((bt, GDN_CONV - 1, GDN_CONV_DIM), lambda i, t: (i, 0, 0)),
            pl.BlockSpec((bt, GDN_HEADS, GDN_DK, GDN_DV), lambda i, t: (i, 0, 0, 0)),
        ] + [r[0] for r in riders],
        out_specs=[
            pl.BlockSpec((bt, lt, GDN_WIDTH), lambda i, t: (i, t, 0)),
            pl.BlockSpec((bt, GDN_HEADS, GDN_DK, GDN_DV), lambda i, t: (i, 0, 0, 0)),
            pl.BlockSpec((bt, GDN_CONV - 1, GDN_CONV_DIM), lambda i, t: (i, 0, 0)),
        ] + [r[1] for r in riders],
        out_shape=[
            jax.ShapeDtypeStruct((b, l, GDN_WIDTH), BF16),
            jax.ShapeDtypeStruct((b, GDN_HEADS, GDN_DK, GDN_DV), F32),
            jax.ShapeDtypeStruct((b, GDN_CONV - 1, GDN_CONV_DIM), F32),
        ] + [r[2] for r in riders],
        scratch_shapes=[pltpu.VMEM((bt, lt + CONV_PAD, GDN_CONV_DIM), F32)],
        compiler_params=_params("arbitrary", "arbitrary"),
        name="gdn_mix",
    )(proj, proj, small, conv_w, gate_vec, gdn_norm.reshape(1, GDN_DV), c0, s0, *rider_args)


def _out_proj_kernel(x_ref, oa_ref, ob_ref, w_ref, g1_ref, sc_ref, sh_ref, n2_ref, x1_ref, h2_ref,
                     *, n_sub):
    bt, lt, d = x_ref.shape
    if bt == 1:
        ls = lt // n_sub
        subs = [(slice(0, 1), slice(i * ls, (i + 1) * ls)) for i in range(n_sub)]
    else:
        bs = bt // n_sub
        subs = [(slice(i * bs, (i + 1) * bs), slice(0, lt)) for i in range(n_sub)]
    mixes = []
    for bsl, lsl in subs:
        oa = oa_ref[bsl, lsl, :]
        nb, nl, _ = oa.shape
        oa = oa.reshape(nb * nl, GLA_WIDTH)
        ob = ob_ref[bsl, lsl, :].reshape(nb * nl, GDN_WIDTH)
        mix = _dot(oa, w_ref[0:GLA_WIDTH, :]) + _dot(ob, w_ref[GLA_WIDTH:GLA_WIDTH + GDN_WIDTH, :])
        mixes.append(mix.reshape(nb, nl, d))
    for (bsl, lsl), mix in zip(subs, mixes):
        x1 = x_ref[bsl, lsl, :] + g1_ref[bsl] * mix
        x1_ref[bsl, lsl, :] = x1
        y = x1 * lax.rsqrt(jnp.mean(x1 * x1, axis=-1, keepdims=True) + EPS) * n2_ref[...]
        h2_ref[bsl, lsl, :] = (y * (1.0 + sc_ref[bsl]) + sh_ref[bsl]).astype(h2_ref.dtype)


def _out_proj(x, o_a, o_b, mod3, mod_row0, w_out16, norm2, bt, lt):
    b, l, d = x.shape
    mrow = mod_row0 // bt
    return pl.pallas_call(
        functools.partial(_out_proj_kernel, n_sub=2),
        grid=(b // bt, l // lt),
        in_specs=[
            pl.BlockSpec((bt, lt, d), lambda i, t: (i, t, 0)),
            pl.BlockSpec((bt, lt, GLA_WIDTH), lambda i, t: (i, t, 0)),
            pl.BlockSpec((bt, lt, GDN_WIDTH), lambda i, t: (i, t, 0)),
            pl.BlockSpec((GLA_WIDTH + GDN_WIDTH, d), lambda i, t: (0, 0)),
            pl.BlockSpec((bt, 1, d), lambda i, t: (mrow + i, 0, 2)),
            pl.BlockSpec((bt, 1, d), lambda i, t: (mrow + i, 0, 4)),
            pl.BlockSpec((bt, 1, d), lambda i, t: (mrow + i, 0, 3)),
            pl.BlockSpec((1, 1, d), lambda i, t: (0, 0, 0)),
        ],
        out_specs=[
            pl.BlockSpec((bt, lt, d), lambda i, t: (i, t, 0)),
            pl.BlockSpec((bt, lt, d), lambda i, t: (i, t, 0)),
        ],
        out_shape=[
            jax.ShapeDtypeStruct((b, l, d), F32),
            jax.ShapeDtypeStruct((b, l, d), BF16),
        ],
        compiler_params=_params("arbitrary", "arbitrary"),
        name="out_proj",
    )(x, o_a, o_b, w_out16, mod3, mod3, mod3, norm2.reshape(1, 1, d))


FFN_PAD = SUBLANES


def _ffn_kernel(h_ref, halo_ref, wg_ref, wv_ref, wd_ref, cw_ref, cb_ref, st_ref, x1_hbm, g2_ref,
                fn_ref, y_ref, so_ref, gbuf, x1_buf, x1_sem, *, has_halo, n_sub):
    ib = pl.program_id(0)
    t = pl.program_id(1)
    f = pl.program_id(2)
    bt, lt, d = h_ref.shape
    tf = wg_ref.shape[1]
    x1_copy = pltpu.make_async_copy(
        x1_hbm.at[pl.ds(ib * bt, bt), pl.ds(t * lt, lt), :], x1_buf, x1_sem)
    if bt == 1:
        ls = lt // n_sub
        subs = [(slice(0, 1), slice(i * ls, (i + 1) * ls)) for i in range(n_sub)]
    else:
        bs = bt // n_sub
        subs = [(slice(i * bs, (i + 1) * bs), slice(0, lt)) for i in range(n_sub)]

    @pl.when(f == 0)
    def _():
        x1_copy.start()
        y_ref[...] = jnp.zeros(y_ref.shape, F32)

    prev = st_ref[...]
    if has_halo:
        halo_gate = _dot(halo_ref[0], wg_ref[...])
        prev = jnp.where(t == 0, prev, halo_gate[SUBLANES - (FFN_CONV - 1):SUBLANES, :][None])
    gbuf[:, FFN_PAD - (FFN_CONV - 1):FFN_PAD, :] = prev
    gates, vals = [], []
    for bsl, lsl in subs:
        hb = h_ref[bsl, lsl, :]
        nb, nl, _ = hb.shape
        hb = hb.reshape(nb * nl, d)
        gate = _dot(hb, wg_ref[...]).reshape(nb, nl, tf)
        gbuf[bsl, FFN_PAD + lsl.start:FFN_PAD + lsl.stop, :] = gate
        gates.append(gate)
        vals.append(_dot(hb, wv_ref[...]))
    so_ref[...] = gbuf[:, FFN_PAD + lt - (FFN_CONV - 1):FFN_PAD + lt, :].reshape(so_ref.shape)
    acts = []
    for (bsl, lsl), gate, val in zip(subs, gates, vals):
        nb, nl, _ = gate.shape
        conv = cw_ref[FFN_CONV - 1:FFN_CONV, :] * gate
        for tap in range(FFN_CONV - 1):
            back = FFN_CONV - 1 - tap
            conv = conv + cw_ref[tap:tap + 1, :] * gbuf[bsl, FFN_PAD + lsl.start - back:FFN_PAD + lsl.stop - back, :]
        acts.append((_silu(conv + cb_ref[...]).reshape(nb * nl, tf) * val).astype(BF16))
    parts = [_dot(act, wd_ref[...]) for act in acts]
    for (bsl, lsl), gate, part in zip(subs, gates, parts):
        nb, nl, _ = gate.shape
        y_ref[bsl, lsl, :] += part.reshape(nb, nl, d)

    @pl.when(f == pl.num_programs(2) - 1)
    def _():
        x1_copy.wait()
        x2 = x1_buf[...] + g2_ref[...] * y_ref[...]
        y = x2 * lax.rsqrt(jnp.mean(x2 * x2, axis=-1, keepdims=True) + EPS) * fn_ref[...]
        y_ref[...] = y


def _ffn(h2, x1, st0, mod3, mod_row0, w_up16, w_down16, conv_w, conv_b, final_norm, bt, lt):
    b, l, d = h2.shape
    tf = 512
    nf = D_FF // tf
    mrow = mod_row0 // bt
    has_halo = lt < l
    halo_blocks = lt // SUBLANES
    kern = functools.partial(_ffn_kernel, has_halo=has_halo, n_sub=4)
    return pl.pallas_call(
        kern,
        grid=(b // bt, l // lt, nf),
        in_specs=[
            pl.BlockSpec((bt, lt, d), lambda i, t, f: (i, t, 0)),
            pl.BlockSpec((1, SUBLANES, d), lambda i, t, f: (i, jnp.maximum(t * halo_blocks - 1, 0), 0)),
            pl.BlockSpec((d, tf), lambda i, t, f: (0, f)),
            pl.BlockSpec((d, tf), lambda i, t, f: (0, nf + f)),
            pl.BlockSpec((tf, d), lambda i, t, f: (f, 0)),
            pl.BlockSpec((FFN_CONV, tf), lambda i, t, f: (0, f)),
            pl.BlockSpec((1, tf), lambda i, t, f: (0, f)),
            pl.BlockSpec((bt, FFN_CONV - 1, tf), lambda i, t, f: (i, 0, f)),
            pl.BlockSpec(memory_space=pl.ANY),
            pl.BlockSpec((bt, 1, d), lambda i, t, f: (mrow + i, 0, 5)),
            pl.BlockSpec((1, 1, d), lambda i, t, f: (0, 0, 0)),
        ],
        out_specs=[
            pl.BlockSpec((bt, lt, d), lambda i, t, f: (i, t, 0)),
            pl.BlockSpec((bt, 1, FFN_CONV - 1, tf), lambda i, t, f: (i, t, 0, f)),
        ],
        out_shape=[
            jax.ShapeDtypeStruct((b, l, d), F32),
            jax.ShapeDtypeStruct((b, l // lt, FFN_CONV - 1, D_FF), F32),
        ],
        scratch_shapes=[
            pltpu.VMEM((bt, lt + FFN_PAD, tf), F32),
            pltpu.VMEM((bt, lt, d), F32),
            pltpu.SemaphoreType.DMA(()),
        ],
        compiler_params=_params("arbitrary", "arbitrary", "arbitrary",
                                vmem_limit_bytes=FFN_VMEM_LIMIT_BYTES),
        name="ffn",
    )(h2, h2, w_up16, w_up16, w_down16, conv_w, conv_b.reshape(1, D_FF), st0, x1, mod3,
      final_norm.reshape(1, 1, d))


def _trunk(x, mod3, mod_row0, states, weights, cfg):
    (w_in_t, w_small_t, norm1, gla_wg, gla_bg, gla_norm, gdn_conv_w, gate_vec, gdn_norm, w_out,
     norm2, w_up, w_down, ffn_conv_w, ffn_conv_b, final_norm) = weights
    s_gla, s_gdn, s_conv, s_ffn = states
    convert = w_up.dtype != BF16
    proj, small = _in_proj(x, mod3, mod_row0, norm1, w_in_t, w_small_t, cfg["in_bt"], cfg["in_lt"])
    o_a, n_gla, *gla_riders = _gla_mix(proj, small, s_gla, gla_wg, gla_bg, gla_norm, cfg["gla_bt"],
                                       cfg["gla_lt"], riders=[w_down] if convert else [])
    o_b, n_gdn, n_conv, *gdn_riders = _gdn_mix(proj, small, s_conv, s_gdn, gdn_conv_w, gate_vec, gdn_norm,
                                               cfg["gdn_bt"], cfg["gdn_lt"],
                                               riders=[w_up, w_out] if convert else [])
    w_out16, w_up16, w_down16 = (gdn_riders[1], gdn_riders[0], gla_riders[0]) if convert else (
        w_out, w_up, w_down)
    x1, h2 = _out_proj(x, o_a, o_b, mod3, mod_row0, w_out16, norm2, cfg["out_bt"], cfg["out_lt"])
    y, n_ffn = _ffn(h2, x1, s_ffn, mod3, mod_row0, w_up16, w_down16, ffn_conv_w, ffn_conv_b,
                    final_norm, cfg["ffn_bt"], cfg["ffn_lt"])
    return (y, n_gla[None], n_gdn[None], n_conv[None], n_ffn[:, -1][None]), (w_out16, w_up16, w_down16)


def kernel(x_prompt, x_sample, c_prompt, c_sample, state_gla, state_gdn, state_gdn_conv, state_ffn_conv, w_ada, b_ada, norm1, w_in, gla_wg, gla_bg, gla_norm, gdn_conv_w, gdn_a_log, gdn_dt_bias, gdn_norm, w_out, norm2, w_up, ffn_conv_w, ffn_conv_b, w_down, final_norm):
    bp = x_prompt.shape[0]
    bs = x_sample.shape[0]

    w_in_t = jnp.swapaxes(w_in[0], 0, 1)
    w_small_t = jnp.concatenate(
        [w_in_t[IN_OFF_LR:IN_OFF_GDN], w_in_t[IN_OFF_BETA:],
         jnp.zeros((SMALL_COLS - GLA_GATE_RANK - 2 * GDN_HEADS, D_MODEL), F32)], axis=0)
    gate_vec = jnp.zeros((2, SMALL_COLS), F32)
    gate_vec = gate_vec.at[0, ALPHA_OFF:ALPHA_OFF + GDN_HEADS].set(gdn_a_log[0])
    gate_vec = gate_vec.at[1, ALPHA_OFF:ALPHA_OFF + GDN_HEADS].set(gdn_dt_bias[0])

    c_all = jnp.concatenate([c_sample, c_prompt], axis=0)
    mod = _ada_mod(c_all, w_ada[0], b_ada[0])
    mod3 = mod.reshape(bs + bp, 1, N_MOD * D_MODEL)

    weights = (w_in_t, w_small_t, norm1[0], gla_wg[0], gla_bg[0], gla_norm[0], gdn_conv_w[0], gate_vec,
               gdn_norm[0], w_out[0], norm2[0], w_up[0], w_down[0], ffn_conv_w[0], ffn_conv_b[0],
               final_norm)

    fresh = (jnp.zeros((bp, GLA_HEADS, GLA_DK, GLA_DV), F32),
             jnp.zeros((bp, GDN_HEADS, GDN_DK, GDN_DV), F32),
             jnp.zeros((bp, GDN_CONV - 1, GDN_CONV_DIM), F32),
             jnp.zeros((bp, FFN_CONV - 1, D_FF), F32))
    cfg_p = dict(in_bt=1, in_lt=1024, gla_bt=1, gla_lt=512, gdn_bt=1, gdn_lt=256,
                 out_bt=1, out_lt=512, ffn_bt=1, ffn_lt=1024)
    (y_p, p_gla, p_gdn, p_conv, p_ffn), (w_out16, w_up16, w_down16) = _trunk(
        x_prompt, mod3, bs, fresh, weights, cfg_p)

    carried = (state_gla[0], state_gdn[0], state_gdn_conv[0], state_ffn_conv[0])
    ls = x_sample.shape[1]
    cfg_s = dict(in_bt=bs, in_lt=ls, gla_bt=4, gla_lt=ls, gdn_bt=4, gdn_lt=ls,
                 out_bt=bs // 2, out_lt=ls, ffn_bt=bs, ffn_lt=ls)
    weights_s = weights[:9] + (w_out16, weights[10], w_up16, w_down16) + weights[13:]
    (y_s, s_gla, s_gdn, s_conv, s_ffn), _ = _trunk(x_sample, mod3, 0, carried, weights_s, cfg_s)
    return (y_p, y_s, p_gla, p_gdn, p_conv, p_ffn, s_gla, s_gdn, s_conv, s_ffn)
```

```python
import functools

import jax
import jax.numpy as jnp
from jax import lax
from jax.experimental import pallas as pl
from jax.experimental.pallas import tpu as pltpu

F32 = jnp.float32
BF16 = jnp.bfloat16

D_MODEL = 2048
CHUNK = 64
GLA_HEADS = 4
GLA_DK = 128
GLA_DV = 256
GLA_WIDTH = GLA_HEADS * GLA_DV
GLA_GATE_RANK = 16
GLA_GATE_NORM = 16.0
GLA_MILD_LOG_DECAY = 60.0
GDN_HEADS = 8
GDN_DK = 128
GDN_DV = 128
GDN_WIDTH = GDN_HEADS * GDN_DV
GDN_CONV = 4
GDN_CONV_DIM = 3 * GDN_WIDTH
D_FF = 5632
FFN_CONV = 3
N_MOD = 6
EPS = 1e-6

GLA_COLS = 2 * GLA_HEADS * GLA_DK + 2 * GLA_WIDTH
GDN_COLS = 4 * GDN_WIDTH
MAIN_COLS = GLA_COLS + GDN_COLS
SMALL_COLS = 128
LR_OFF, BETA_OFF, ALPHA_OFF = 0, GLA_GATE_RANK, GLA_GATE_RANK + GDN_HEADS
IN_OFF_LR = GLA_COLS
IN_OFF_GDN = IN_OFF_LR + GLA_GATE_RANK
IN_OFF_BETA = IN_OFF_GDN + GDN_COLS

SUBLANES = 8
VMEM_LIMIT_BYTES = 56 * 1024 * 1024
BIG_VMEM_LIMIT_BYTES = 60 * 1024 * 1024

NT_DIMS = (((1,), (1,)), ((), ()))
TN_DIMS = (((0,), (0,)), ((), ()))


def _dot(a, b):
    return jnp.dot(a, b, preferred_element_type=F32)


def _dot_nt(a, b):
    return lax.dot_general(a, b, NT_DIMS, preferred_element_type=F32)


def _dot_tn(a, b):
    return lax.dot_general(a, b, TN_DIMS, preferred_element_type=F32)


def _dot_f32(a, b):
    return jnp.dot(a, b, preferred_element_type=F32, precision=lax.Precision.HIGHEST)


def _dot_many(a_list, b_list):
    return [_dot(a.astype(BF16), b.astype(BF16)) for a, b in zip(a_list, b_list)]


def _silu(x):
    return x * jax.nn.sigmoid(x)


def _softplus(x):
    return jnp.maximum(x, 0.0) + jnp.log1p(jnp.exp(-jnp.abs(x)))


def _log_sigmoid(x):
    return jnp.minimum(x, 0.0) - jnp.log1p(jnp.exp(-jnp.abs(x)))


def _params(*semantics, vmem_limit_bytes=VMEM_LIMIT_BYTES):
    return pltpu.CompilerParams(dimension_semantics=semantics, vmem_limit_bytes=vmem_limit_bytes)


def _rider_specs(rider, grid):
    steps = grid[0] * grid[1]
    rows, cols = rider.shape
    assert rows % steps == 0, (rows, steps)
    spec = pl.BlockSpec((rows // steps, cols), lambda i, t: (i * grid[1] + t, 0))
    return spec, spec, jax.ShapeDtypeStruct(rider.shape, BF16)


def _tri_masks():
    row = lax.broadcasted_iota(jnp.int32, (CHUNK, CHUNK), 0)
    col = lax.broadcasted_iota(jnp.int32, (CHUNK, CHUNK), 1)
    return row >= col, row > col


def _ada_kernel(c_ref, w_ref, b_ref, o_ref):
    s = _silu(c_ref[...]).astype(BF16)
    o_ref[...] = _dot(s, w_ref[...].astype(BF16)) + b_ref[...]


def _ada_mod(c_all, w_ada, b_ada):
    rows = c_all.shape[0]
    n = w_ada.shape[1]
    tn = 1024
    return pl.pallas_call(
        _ada_kernel,
        grid=(n // tn,),
        in_specs=[
            pl.BlockSpec((rows, D_MODEL), lambda j: (0, 0)),
            pl.BlockSpec((D_MODEL, tn), lambda j: (0, j)),
            pl.BlockSpec((1, tn), lambda j: (0, j)),
        ],
        out_specs=pl.BlockSpec((rows, tn), lambda j: (0, j)),
        out_shape=jax.ShapeDtypeStruct((rows, n), F32),
        compiler_params=_params("arbitrary"),
        name="ada_mod",
    )(c_all, w_ada, b_ada.reshape(1, n))


W_RING = 3


def _in_proj_kernel(x_ref, sc_ref, sh_ref, n1_ref, w_hbm, ws_ref, o_ref, os_ref, h_scr, w_ring, w_sems,
                    *, n_a):
    bt, lt, d = x_ref.shape
    tn = o_ref.shape[-1]
    n_t, n_j = pl.num_programs(1), pl.num_programs(2)
    j = pl.program_id(2)
    step = (pl.program_id(0) * n_t + pl.program_id(1)) * n_j + j
    n_steps = pl.num_programs(0) * n_t * n_j

    def tile_copy(s):
        jj = lax.rem(s, n_j)
        start = jnp.where(jj < n_a, jj * tn, IN_OFF_GDN + (jj - n_a) * tn)
        slot = lax.rem(s, W_RING)
        return pltpu.make_async_copy(
            w_hbm.at[pl.ds(pl.multiple_of(start, SUBLANES), tn), :], w_ring.at[slot], w_sems.at[slot])

    @pl.when(step == 0)
    def _():
        for s in range(W_RING - 1):
            tile_copy(s).start()

    @pl.when(step + (W_RING - 1) < n_steps)
    def _():
        tile_copy(step + (W_RING - 1)).start()

    @pl.when(j == 0)
    def _():
        x = x_ref[...]
        y = x * lax.rsqrt(jnp.mean(x * x, axis=-1, keepdims=True) + EPS) * n1_ref[...]
        h = y * (1.0 + sc_ref[...]) + sh_ref[...]
        hb = h.reshape(bt * lt, d).astype(BF16)
        h_scr[...] = hb
        os_ref[...] = _dot_nt(hb, ws_ref[...].astype(BF16)).reshape(os_ref.shape)

    tile_copy(step).wait()
    w = w_ring[lax.rem(step, W_RING)].astype(BF16)
    o_ref[...] = _dot_nt(h_scr[...], w).reshape(o_ref.shape)


def _in_proj(x, mod3, mod_row0, norm1, w_in_t, w_small_t, bt, lt):
    b, l, d = x.shape
    tn = 1024
    n_a = GLA_COLS // tn
    mrow = mod_row0 // bt
    grid = (b // bt, l // lt, MAIN_COLS // tn)
    assert grid[0] * grid[1] * grid[2] >= W_RING - 1
    return pl.pallas_call(
        functools.partial(_in_proj_kernel, n_a=n_a),
        grid=grid,
        in_specs=[
            pl.BlockSpec((bt, lt, d), lambda i, t, j: (i, t, 0)),
            pl.BlockSpec((bt, 1, d), lambda i, t, j: (mrow + i, 0, 1)),
            pl.BlockSpec((bt, 1, d), lambda i, t, j: (mrow + i, 0, 0)),
            pl.BlockSpec((1, 1, d), lambda i, t, j: (0, 0, 0)),
            pl.BlockSpec(memory_space=pl.ANY),
            pl.BlockSpec((SMALL_COLS, d), lambda i, t, j: (0, 0)),
        ],
        out_specs=[
            pl.BlockSpec((bt, lt, tn), lambda i, t, j: (i, t, j)),
            pl.BlockSpec((bt, lt, SMALL_COLS), lambda i, t, j: (i, t, 0)),
        ],
        out_shape=[
            jax.ShapeDtypeStruct((b, l, MAIN_COLS), F32),
            jax.ShapeDtypeStruct((b, l, SMALL_COLS), F32),
        ],
        scratch_shapes=[pltpu.VMEM((bt * lt, d), BF16),
                        pltpu.VMEM((W_RING, tn, d), F32),
                        pltpu.SemaphoreType.DMA((W_RING,))],
        compiler_params=_params("arbitrary", "arbitrary", "arbitrary",
                                vmem_limit_bytes=BIG_VMEM_LIMIT_BYTES),
        name="in_proj",
    )(x, mod3, mod3, norm1.reshape(1, 1, d), w_in_t, w_small_t)


def _gla_levels():
    row = lax.broadcasted_iota(jnp.int32, (CHUNK, CHUNK), 0)
    col = lax.broadcasted_iota(jnp.int32, (CHUNK, CHUNK), 1)
    ops, masks = [], []
    half = CHUNK // 2
    while half >= 1:
        shift = half.bit_length()
        parent_row = jnp.left_shift(jnp.right_shift(row, shift), shift)
        parent_col = jnp.left_shift(jnp.right_shift(col, shift), shift)
        m_row = parent_row + (half - 1)
        right_row = row > m_row
        in_span = (right_row & (col > m_row) & (col <= row)) | (
            jnp.logical_not(right_row) & (col > row) & (col <= m_row))
        ops.append(jnp.where(in_span, 1.0, 0.0))
        masks.append((parent_row == parent_col) & right_row & (col <= m_row))
        half //= 2
    return ops, masks, row == col


def _gla_kernel(q_ref, k_ref, v_ref, r_ref, sm_ref, wg_ref, bg_ref, gn_ref, s0_ref, *rest, n_riders):
    src_refs, (o_ref, s_ref), rest = rest[:n_riders], rest[n_riders:n_riders + 2], rest[n_riders + 2:]
    dst_refs, (st_scr, attn_scr) = rest[:n_riders], rest[n_riders:]
    for src_ref, dst_ref in zip(src_refs, dst_refs):
        dst_ref[...] = src_ref[...].astype(dst_ref.dtype)
    t = pl.program_id(1)
    bt, lt = q_ref.shape[0], q_ref.shape[1]
    n_chunks = lt // CHUNK
    causal, _ = _tri_masks()
    tri = causal.astype(F32)
    scale = GLA_DK ** -0.5

    @pl.when(t == 0)
    def _():
        for b in range(bt):
            for h in range(GLA_HEADS):
                st_scr[b, h] = s0_ref[b, h].T

    wg = wg_ref[...].astype(BF16)
    heads = range(GLA_HEADS)
    kcs = [slice(h * GLA_DK, (h + 1) * GLA_DK) for h in heads]
    vcs = [slice(h * GLA_DV, (h + 1) * GLA_DV) for h in heads]
    items = [(b, pl.ds(c * CHUNK, CHUNK)) for c in range(n_chunks) for b in range(bt)]

    def scaled_q(b, rows):
        return [q_ref[b, rows, kc] * scale for kc in kcs]

    log_a, big_g = [], []
    for b, rows in items:
        a_lr = sm_ref[b, rows, LR_OFF:LR_OFF + GLA_GATE_RANK].astype(BF16)
        la = _log_sigmoid(_dot(a_lr, wg) + bg_ref[...]) / GLA_GATE_NORM
        log_a.append(la)
        big_g.append(_dot_f32(tri, la))
    total = big_g[0][CHUNK - 1:CHUNK, :]
    for g in big_g[1:]:
        total = jnp.minimum(total, g[CHUNK - 1:CHUNK, :])
    mild = jnp.min(total) > -GLA_MILD_LOG_DECAY

    @pl.when(mild)
    def _():
        for n, (b, rows) in enumerate(items):
            g_mid = big_g[n][CHUNK // 2 - 1:CHUNK // 2, :]
            e_q = jnp.exp(big_g[n] - g_mid)
            e_k = jnp.exp(g_mid - big_g[n])
            q = scaled_q(b, rows)
            scores = [_dot_nt((qq * e_q[:, kc]).astype(BF16),
                              (k_ref[b, rows, kc] * e_k[:, kc]).astype(BF16))
                      for qq, kc in zip(q, kcs)]
            for h in heads:
                attn_scr[n, h] = jnp.where(causal, scores[h], 0.0)

    @pl.when(jnp.logical_not(mild))
    def _():
        level_ops, level_masks, diag = _gla_levels()
        level_op = jnp.concatenate(level_ops, axis=0)
        for n, (b, rows) in enumerate(items):
            sums = _dot_f32(level_op, log_a[n])
            q = scaled_q(b, rows)
            k = [k_ref[b, rows, kc] for kc in kcs]
            scores = [jnp.where(diag, _dot_nt(qq.astype(BF16), kk.astype(BF16)), 0.0)
                      for qq, kk in zip(q, k)]
            for i, mask in enumerate(level_masks):
                e = jnp.exp(sums[i * CHUNK:(i + 1) * CHUNK])
                part = [_dot_nt((qq * e[:, kc]).astype(BF16), (kk * e[:, kc]).astype(BF16))
                        for qq, kk, kc in zip(q, k, kcs)]
                scores = [jnp.where(mask, p, a) for p, a in zip(part, scores)]
            for h in heads:
                attn_scr[n, h] = scores[h]

    pending = []
    for n, (b, rows) in enumerate(items):
        g_last = big_g[n][CHUNK - 1:CHUNK, :]
        e_g = jnp.exp(big_g[n])
        e_kl = jnp.exp(g_last - big_g[n])
        e_l = jnp.exp(g_last)
        q = scaled_q(b, rows)
        v = [v_ref[b, rows, vc].astype(BF16) for vc in vcs]
        v_k = [_dot_tn(vv, (k_ref[b, rows, kc] * e_kl[:, kc]).astype(BF16)) for vv, kc in zip(v, kcs)]
        a_v = [_dot(attn_scr[n, h].astype(BF16), v[h]) for h in heads]
        q_g = [(qq * e_g[:, kc]).astype(BF16) for qq, kc in zip(q, kcs)]
        pending.append((q_g, a_v, v_k, [e_l[:, kc] for kc in kcs]))
    for c in range(n_chunks):
        group = range(c * bt, (c + 1) * bt)
        st = [[st_scr[items[n][0], h] for h in heads] for n in group]
        q_s = [[_dot_nt(qq, ss.astype(BF16)) for qq, ss in zip(pending[n][0], st_n)]
               for n, st_n in zip(group, st)]
        for n, st_n, q_s_n in zip(group, st, q_s):
            b, rows = items[n]
            _, a_v, v_k, e_l = pending[n]
            for h in heads:
                st_scr[b, h] = st_n[h] * e_l[h] + v_k[h]
                o = a_v[h] + q_s_n[h]
                o = o * lax.rsqrt(jnp.mean(o * o, axis=-1, keepdims=True) + EPS) * gn_ref[...]
                o = o * _silu(r_ref[b, rows, vcs[h]])
                o_ref[b, rows, vcs[h]] = o.astype(o_ref.dtype)

    @pl.when(t == pl.num_programs(1) - 1)
    def _():
        for b in range(bt):
            for h in range(GLA_HEADS):
                s_ref[b, h] = st_scr[b, h].T


def _gla_mix(proj, small, s0, gla_wg, gla_bg, gla_norm, bt, lt, riders=()):
    b, l, _ = proj.shape
    kw = GLA_HEADS * GLA_DK
    grid = (b // bt, l // lt)
    rider_args = list(riders)
    riders = [_rider_specs(r, grid) for r in rider_args]
    return pl.pallas_call(
        functools.partial(_gla_kernel, n_riders=len(riders)),
        grid=grid,
        in_specs=[
            pl.BlockSpec((bt, lt, kw), lambda i, t: (i, t, 0)),
            pl.BlockSpec((bt, lt, kw), lambda i, t: (i, t, 1)),
            pl.BlockSpec((bt, lt, GLA_WIDTH), lambda i, t: (i, t, 1)),
            pl.BlockSpec((bt, lt, GLA_WIDTH), lambda i, t: (i, t, 2)),
            pl.BlockSpec((bt, lt, SMALL_COLS), lambda i, t: (i, t, 0)),
            pl.BlockSpec((GLA_GATE_RANK, kw), lambda i, t: (0, 0)),
            pl.BlockSpec((1, kw), lambda i, t: (0, 0)),
            pl.BlockSpec((1, GLA_DV), lambda i, t: (0, 0)),
            pl.BlockSpec((bt, GLA_HEADS, GLA_DK, GLA_DV), lambda i, t: (i, 0, 0, 0)),
        ] + [r[0] for r in riders],
        out_specs=[
            pl.BlockSpec((bt, lt, GLA_WIDTH), lambda i, t: (i, t, 0)),
            pl.BlockSpec((bt, GLA_HEADS, GLA_DK, GLA_DV), lambda i, t: (i, 0, 0, 0)),
        ] + [r[1] for r in riders],
        out_shape=[
            jax.ShapeDtypeStruct((b, l, GLA_WIDTH), BF16),
            jax.ShapeDtypeStruct((b, GLA_HEADS, GLA_DK, GLA_DV), F32),
        ] + [r[2] for r in riders],
        scratch_shapes=[pltpu.VMEM((bt, GLA_HEADS, GLA_DV, GLA_DK), F32),
                        pltpu.VMEM((bt * (lt // CHUNK), GLA_HEADS, CHUNK, CHUNK), F32)],
        compiler_params=_params("arbitrary", "arbitrary"),
        name="gla_mix",
    )(proj, proj, proj, proj, small, gla_wg, gla_bg.reshape(1, kw), gla_norm.reshape(1, GLA_DV), s0,
      *rider_args)


CONV_PAD = SUBLANES
INV_BASE_LOG2 = 3
GDN_CHUNK_GROUP = 4


def _inverse_masks():
    row = lax.broadcasted_iota(jnp.int32, (CHUNK, CHUNK), 0)
    col = lax.broadcasted_iota(jnp.int32, (CHUNK, CHUNK), 1)

    def same_block(log2):
        return jnp.right_shift(row, log2) == jnp.right_shift(col, log2)

    base = same_block(INV_BASE_LOG2)
    merges = []
    log2 = INV_BASE_LOG2
    while (1 << log2) < CHUNK:
        merges.append(same_block(log2 + 1) & jnp.logical_not(same_block(log2)))
        log2 += 1
    return base, merges


def _unit_lower_inverse_minus_eye(lower, base, merges):
    neg = [jnp.where(base, -l, 0.0) for l in lower]
    x = neg
    p = neg
    for _ in range(INV_BASE_LOG2 - 1):
        p = _dot_many(p, p)
        xp = _dot_many(x, p)
        x = [a + b + c for a, b, c in zip(x, p, xp)]
    for m in merges:
        c = [jnp.where(m, l, 0.0) for l in lower]
        w = [a + b for a, b in zip(c, _dot_many(x, c))]
        wx = _dot_many(w, x)
        x = [a - (b + d) for a, b, d in zip(x, w, wx)]
    return x


def _gdn_kernel(x_ref, g_ref, sm_ref, cw_ref, ga_ref, gn_ref, c0_ref, s0_ref, *rest, n_riders):
    src_refs, (o_ref, s_ref, c_ref), rest = rest[:n_riders], rest[n_riders:n_riders + 3], rest[n_riders + 3:]
    dst_refs, (xbuf,) = rest[:n_riders], rest[n_riders:]
    for src_ref, dst_ref in zip(src_refs, dst_refs):
        dst_ref[...] = src_ref[...].astype(dst_ref.dtype)
    t = pl.program_id(1)
    bt, lt = x_ref.shape[0], x_ref.shape[1]
    causal, strict = _tri_masks()
    tri = causal.astype(F32)
    inv_base, inv_merges = _inverse_masks()
    scale = GDN_DK ** -0.5

    @pl.when(t == 0)
    def _():
        xbuf[:, 0:CONV_PAD, :] = jnp.zeros((bt, CONV_PAD, GDN_CONV_DIM), F32)
        xbuf[:, CONV_PAD - (GDN_CONV - 1):CONV_PAD, :] = c0_ref[...]
        s_ref[...] = s0_ref[...]

    @pl.when(t > 0)
    def _():
        xbuf[:, 0:CONV_PAD, :] = xbuf[:, lt:lt + CONV_PAD, :]

    xbuf[:, CONV_PAD:CONV_PAD + lt, :] = x_ref[...]
    c_ref[...] = xbuf[:, lt + CONV_PAD - (GDN_CONV - 1):lt + CONV_PAD, :]

    def conv_silu(b, r0, cols):
        acc = cw_ref[GDN_CONV - 1:GDN_CONV, cols] * xbuf[b, r0 + CONV_PAD:r0 + CONV_PAD + CHUNK, cols]
        for tap in range(GDN_CONV - 1):
            start = r0 + CONV_PAD - (GDN_CONV - 1 - tap)
            acc = acc + cw_ref[tap:tap + 1, cols] * xbuf[b, start:start + CHUNK, cols]
        return _silu(acc)

    def l2norm(x):
        return x * lax.rsqrt(jnp.sum(x * x, axis=-1, keepdims=True) + EPS)

    heads = range(GDN_HEADS)
    def prep_gates(b, c):
        sm = sm_ref[b, pl.ds(c * CHUNK, CHUNK), :]
        gate = -jnp.exp(ga_ref[0:1, :]) * _softplus(sm + ga_ref[1:2, :])
        big_g = _dot_f32(tri, gate)
        return jax.nn.sigmoid(sm), big_g, big_g.T

    def prep_head(b, c, gates, h):
        beta, big_g, big_gt = gates
        r0 = c * CHUNK
        g_col = big_g[:, ALPHA_OFF + h:ALPHA_OFF + h + 1]
        g_row = big_gt[ALPHA_OFF + h:ALPHA_OFF + h + 1, :]
        b_col = beta[:, BETA_OFF + h:BETA_OFF + h + 1]
        g_last = g_col[CHUNK - 1:CHUNK, :]
        e_g = jnp.exp(g_col)
        q = l2norm(conv_silu(b, r0, slice(h * GDN_DK, (h + 1) * GDN_DK))) * scale
        k = l2norm(conv_silu(b, r0, slice(GDN_WIDTH + h * GDN_DK, GDN_WIDTH + (h + 1) * GDN_DK)))
        v = conv_silu(b, r0, slice(2 * GDN_WIDTH + h * GDN_DV, 2 * GDN_WIDTH + (h + 1) * GDN_DV))
        kb = k * b_col
        return dict(
            decay=jnp.where(causal, jnp.exp(g_col - g_row), 0.0),
            rhs=jnp.concatenate([v * b_col, kb * e_g], axis=-1),
            q16=q.astype(BF16), k16=k.astype(BF16), kb16=kb.astype(BF16),
            q_g=(q * e_g).astype(BF16),
            k_dec=(k * jnp.exp(g_last - g_col)).astype(BF16),
            e_last=jnp.exp(g_last))

    def key_products(p):
        kk_t = [_dot_nt(ph["kb16"], ph["k16"]) for ph in p]
        qk_t = [_dot_nt(ph["q16"], ph["k16"]) for ph in p]
        lower = [jnp.where(strict, m * ph["decay"], 0.0) for m, ph in zip(kk_t, p)]
        attn = [jnp.where(causal, m * ph["decay"], 0.0).astype(BF16) for m, ph in zip(qk_t, p)]
        return lower, attn

    n_chunks = lt // CHUNK
    items = [(b, c) for c in range(n_chunks) for b in range(bt)]
    pending = []
    for i0 in range(0, len(items), GDN_CHUNK_GROUP):
        group = items[i0:i0 + GDN_CHUNK_GROUP]
        gates = [prep_gates(b, c) for b, c in group]
        p = [prep_head(b, c, g, h) for (b, c), g in zip(group, gates) for h in heads]
        lower, attn = key_products(p)
        xinv = _unit_lower_inverse_minus_eye(lower, inv_base, inv_merges)
        rhs = [ph["rhs"] for ph in p]
        sol = [r + xr for r, xr in zip(rhs, _dot_many(xinv, rhs))]
        for n in range(len(group)):
            sl = slice(n * GDN_HEADS, (n + 1) * GDN_HEADS)
            pending.append((attn[sl], [so[:, :GDN_DV] for so in sol[sl]],
                            [so[:, GDN_DV:].astype(BF16) for so in sol[sl]],
                            [ph["q_g"] for ph in p[sl]], [ph["k_dec"] for ph in p[sl]],
                            [ph["e_last"] for ph in p[sl]]))
    for c in range(n_chunks):
        group = range(c * bt, (c + 1) * bt)
        attn, sol_v, sol_k, q_g, k_dec, e_last = (
            [x for n in group for x in pending[n][field]] for field in range(6))
        where = [(items[n][0], h) for n in group for h in heads]
        s = [s_ref[b, h] for b, h in where]
        s16 = [ss.astype(BF16) for ss in s]
        k_s = [_dot(a, ss) for a, ss in zip(sol_k, s16)]
        q_s = [_dot(a, ss) for a, ss in zip(q_g, s16)]
        u16 = [(sv - ks).astype(BF16) for sv, ks in zip(sol_v, k_s)]
        a_u = [_dot(a, uu) for a, uu in zip(attn, u16)]
        k_u = [_dot_tn(kd, uu) for kd, uu in zip(k_dec, u16)]
        rows = pl.ds(c * CHUNK, CHUNK)
        for i, (b, h) in enumerate(where):
            hc = slice(h * GDN_DV, (h + 1) * GDN_DV)
            s_ref[b, h] = e_last[i] * s[i] + k_u[i]
            o = q_s[i] + a_u[i]
            o = o * lax.rsqrt(jnp.mean(o * o, axis=-1, keepdims=True) + EPS) * gn_ref[...]
            o = o * _silu(g_ref[b, rows, hc])
            o_ref[b, rows, hc] = o.astype(o_ref.dtype)


def _gdn_mix(proj, small, c0, s0, conv_w, gate_vec, gdn_norm, bt, lt, riders=()):
    b, l, _ = proj.shape
    grid = (b // bt, l // lt)
    rider_args = list(riders)
    riders = [_rider_specs(r, grid) for r in rider_args]
    return pl.pallas_call(
        functools.partial(_gdn_kernel, n_riders=len(riders)),
        grid=grid,
        in_specs=[
            pl.BlockSpec((bt, lt, GDN_CONV_DIM), lambda i, t: (i, t, 1)),
            pl.BlockSpec((bt, lt, GDN_WIDTH), lambda i, t: (i, t, 6)),
            pl.BlockSpec((bt, lt, SMALL_COLS), lambda i, t: (i, t, 0)),
            pl.BlockSpec((GDN_CONV, GDN_CONV_DIM), lambda i, t: (0, 0)),
            pl.BlockSpec((2, SMALL_COLS), lambda i, t: (0, 0)),
            pl.BlockSpec((1, GDN_DV), lambda i, t: (0, 0)),
            pl.BlockSpec((bt, GDN_CONV - 1, GDN_CONV_DIM), lambda i, t: (i, 0, 0)),
            pl.BlockSpec((bt, GDN_HEADS, GDN_DK, GDN_DV), lambda i, t: (i, 0, 0, 0)),
        ] + [r[0] for r in riders],
        out_specs=[
            pl.BlockSpec((bt, lt, GDN_WIDTH), lambda i, t: (i, t, 0)),
            pl.BlockSpec((bt, GDN_HEADS, GDN_DK, GDN_DV), lambda i, t: (i, 0, 0, 0)),
            pl.BlockSpec((bt, GDN_CONV - 1, GDN_CONV_DIM), lambda i, t: (i, 0, 0)),
        ] + [r[1] for r in riders],
        out_shape=[
            jax.ShapeDtypeStruct((b, l, GDN_WIDTH), BF16),
            jax.ShapeDtypeStruct((b, GDN_HEADS, GDN_DK, GDN_DV), F32),
            jax.ShapeDtypeStruct((b, GDN_CONV - 1, GDN_CONV_DIM), F32),
        ] + [r[2] for r in riders],
        scratch_shapes=[pltpu.VMEM((bt, lt + CONV_PAD, GDN_CONV_DIM), F32)],
        compiler_params=_params("arbitrary", "arbitrary"),
        name="gdn_mix",
    )(proj, proj, small, conv_w, gate_vec, gdn_norm.reshape(1, GDN_DV), c0, s0, *rider_args)


def _out_proj_kernel(x_ref, oa_ref, ob_ref, w_ref, g1_ref, sc_ref, sh_ref, n2_ref, x1_ref, h2_ref,
                     *, n_sub):
    bt, lt, d = x_ref.shape
    if bt == 1:
        ls = lt // n_sub
        subs = [(slice(0, 1), slice(i * ls, (i + 1) * ls)) for i in range(n_sub)]
    else:
        bs = bt // n_sub
        subs = [(slice(i * bs, (i + 1) * bs), slice(0, lt)) for i in range(n_sub)]
    mixes = []
    for bsl, lsl in subs:
        oa = oa_ref[bsl, lsl, :]
        nb, nl, _ = oa.shape
        oa = oa.reshape(nb * nl, GLA_WIDTH)
        ob = ob_ref[bsl, lsl, :].reshape(nb * nl, GDN_WIDTH)
        mix = _dot(oa, w_ref[0:GLA_WIDTH, :]) + _dot(ob, w_ref[GLA_WIDTH:GLA_WIDTH + GDN_WIDTH, :])
        mixes.append(mix.reshape(nb, nl, d))
    for (bsl, lsl), mix in zip(subs, mixes):
        x1 = x_ref[bsl, lsl, :] + g1_ref[bsl] * mix
        x1_ref[bsl, lsl, :] = x1
        y = x1 * lax.rsqrt(jnp.mean(x1 * x1, axis=-1, keepdims=True) + EPS) * n2_ref[...]
        h2_ref[bsl, lsl, :] = (y * (1.0 + sc_ref[bsl]) + sh_ref[bsl]).astype(h2_ref.dtype)


def _out_proj(x, o_a, o_b, mod3, mod_row0, w_out16, norm2, bt, lt):
    b, l, d = x.shape
    mrow = mod_row0 // bt
    return pl.pallas_call(
        functools.partial(_out_proj_kernel, n_sub=2),
        grid=(b // bt, l // lt),
        in_specs=[
            pl.BlockSpec((bt, lt, d), lambda i, t: (i, t, 0)),
            pl.BlockSpec((bt, lt, GLA_WIDTH), lambda i, t: (i, t, 0)),
            pl.BlockSpec((bt, lt, GDN_WIDTH), lambda i, t: (i, t, 0)),
            pl.BlockSpec((GLA_WIDTH + GDN_WIDTH, d), lambda i, t: (0, 0)),
            pl.BlockSpec((bt, 1, d), lambda i, t: (mrow + i, 0, 2)),
            pl.BlockSpec((bt, 1, d), lambda i, t: (mrow + i, 0, 4)),
            pl.BlockSpec((bt, 1, d), lambda i, t: (mrow + i, 0, 3)),
            pl.BlockSpec((1, 1, d), lambda i, t: (0, 0, 0)),
        ],
        out_specs=[
            pl.BlockSpec((bt, lt, d), lambda i, t: (i, t, 0)),
            pl.BlockSpec((bt, lt, d), lambda i, t: (i, t, 0)),
        ],
        out_shape=[
            jax.ShapeDtypeStruct((b, l, d), F32),
            jax.ShapeDtypeStruct((b, l, d), BF16),
        ],
        compiler_params=_params("arbitrary", "arbitrary"),
        name="out_proj",
    )(x, o_a, o_b, w_out16, mod3, mod3, mod3, norm2.reshape(1, 1, d))


FFN_PAD = SUBLANES


def _ffn_kernel(h_ref, halo_ref, wg_ref, wv_ref, wd_ref, cw_ref, cb_ref, st_ref, x1_hbm, g2_ref,
                fn_ref, y_ref, so_ref, gbuf, x1_buf, x1_sem, *, has_halo, n_sub):
    ib = pl.program_id(0)
    t = pl.program_id(1)
    f = pl.program_id(2)
    bt, lt, d = h_ref.shape
    tf = wg_ref.shape[1]
    x1_copy = pltpu.make_async_copy(
        x1_hbm.at[pl.ds(ib * bt, bt), pl.ds(t * lt, lt), :], x1_buf, x1_sem)
    if bt == 1:
        ls = lt // n_sub
        subs = [(slice(0, 1), slice(i * ls, (i + 1) * ls)) for i in range(n_sub)]
    else:
        bs = bt // n_sub
        subs = [(slice(i * bs, (i + 1) * bs), slice(0, lt)) for i in range(n_sub)]

    @pl.when(f == 0)
    def _():
        x1_copy.start()
        y_ref[...] = jnp.zeros(y_ref.shape, F32)

    prev = st_ref[...]
    if has_halo:
        halo_gate = _dot(halo_ref[0], wg_ref[...])
        prev = jnp.where(t == 0, prev, halo_gate[SUBLANES - (FFN_CONV - 1):SUBLANES, :][None])
    gbuf[:, FFN_PAD - (FFN_CONV - 1):FFN_PAD, :] = prev
    gates, vals = [], []
    for bsl, lsl in subs:
        hb = h_ref[bsl, lsl, :]
        nb, nl, _ = hb.shape
        hb = hb.reshape(nb * nl, d)
        gate = _dot(hb, wg_ref[...]).reshape(nb, nl, tf)
        gbuf[bsl, FFN_PAD + lsl.start:FFN_PAD + lsl.stop, :] = gate
        gates.append(gate)
        vals.append(_dot(hb, wv_ref[...]))
    so_ref[...] = gbuf[:, FFN_PAD + lt - (FFN_CONV - 1):FFN_PAD + lt, :].reshape(so_ref.shape)
    acts = []
    for (bsl, lsl), gate, val in zip(subs, gates, vals):
        nb, nl, _ = gate.shape
        conv = cw_ref[FFN_CONV - 1:FFN_CONV, :] * gate
        for tap in range(FFN_CONV - 1):
            back = FFN_CONV - 1 - tap
            conv = conv + cw_ref[tap:tap + 1, :] * gbuf[bsl, FFN_PAD + lsl.start - back:FFN_PAD + lsl.stop - back, :]
        acts.append((_silu(conv + cb_ref[...]).reshape(nb * nl, tf) * val).astype(BF16))
    parts = [_dot(act, wd_ref[...]) for act in acts]
    for (bsl, lsl), gate, part in zip(subs, gates, parts):
        nb, nl, _ = gate.shape
        y_ref[bsl, lsl, :] += part.reshape(nb, nl, d)

    @pl.when(f == pl.num_programs(2) - 1)
    def _():
        x1_copy.wait()
        x2 = x1_buf[...] + g2_ref[...] * y_ref[...]
        y = x2 * lax.rsqrt(jnp.mean(x2 * x2, axis=-1, keepdims=True) + EPS) * fn_ref[...]
        y_ref[...] = y


def _ffn(h2, x1, st0, mod3, mod_row0, w_up16, w_down16, conv_w, conv_b, final_norm, bt, lt):
    b, l, d = h2.shape
    tf = 512
    nf = D_FF // tf
    mrow = mod_row0 // bt
    has_halo = lt < l
    halo_blocks = lt // SUBLANES
    kern = functools.partial(_ffn_kernel, has_halo=has_halo, n_sub=4)
    return pl.pallas_call(
        kern,
        grid=(b // bt, l // lt, nf),
        in_specs=[
            pl.BlockSpec((bt, lt, d), lambda i, t, f: (i, t, 0)),
            pl.BlockSpec((1, SUBLANES, d), lambda i, t, f: (i, jnp.maximum(t * halo_blocks - 1, 0), 0)),
            pl.BlockSpec((d, tf), lambda i, t, f: (0, f)),
            pl.BlockSpec((d, tf), lambda i, t, f: (0, nf + f)),
            pl.BlockSpec((tf, d), lambda i, t, f: (f, 0)),
            pl.BlockSpec((FFN_CONV, tf), lambda i, t, f: (0, f)),
            pl.BlockSpec((1, tf), lambda i, t, f: (0, f)),
            pl.BlockSpec((bt, FFN_CONV - 1, tf), lambda i, t, f: (i, 0, f)),
            pl.BlockSpec(memory_space=pl.ANY),
            pl.BlockSpec((bt, 1, d), lambda i, t, f: (mrow + i, 0, 5)),
            pl.BlockSpec((1, 1, d), lambda i, t, f: (0, 0, 0)),
        ],
        out_specs=[
            pl.BlockSpec((bt, lt, d), lambda i, t, f: (i, t, 0)),
            pl.BlockSpec((bt, 1, FFN_CONV - 1, tf), lambda i, t, f: (i, t, 0, f)),
        ],
        out_shape=[
            jax.ShapeDtypeStruct((b, l, d), F32),
            jax.ShapeDtypeStruct((b, l // lt, FFN_CONV - 1, D_FF), F32),
        ],
        scratch_shapes=[
            pltpu.VMEM((bt, lt + FFN_PAD, tf), F32),
            pltpu.VMEM((bt, lt, d), F32),
            pltpu.SemaphoreType.DMA(()),
        ],
        compiler_params=_params("arbitrary", "arbitrary", "arbitrary",
                                vmem_limit_bytes=BIG_VMEM_LIMIT_BYTES),
        name="ffn",
    )(h2, h2, w_up16, w_up16, w_down16, conv_w, conv_b.reshape(1, D_FF), st0, x1, mod3,
      final_norm.reshape(1, 1, d))


def _trunk(x, mod3, mod_row0, states, weights, cfg):
    (w_in_t, w_small_t, norm1, gla_wg, gla_bg, gla_norm, gdn_conv_w, gate_vec, gdn_norm, w_out,
     norm2, w_up, w_down, ffn_conv_w, ffn_conv_b, final_norm) = weights
    s_gla, s_gdn, s_conv, s_ffn = states
    convert = w_up.dtype != BF16
    proj, small = _in_proj(x, mod3, mod_row0, norm1, w_in_t, w_small_t, cfg["in_bt"], cfg["in_lt"])
    o_a, n_gla, *gla_riders = _gla_mix(proj, small, s_gla, gla_wg, gla_bg, gla_norm, cfg["gla_bt"],
                                       cfg["gla_lt"], riders=[w_down] if convert else [])
    o_b, n_gdn, n_conv, *gdn_riders = _gdn_mix(proj, small, s_conv, s_gdn, gdn_conv_w, gate_vec, gdn_norm,
                                               cfg["gdn_bt"], cfg["gdn_lt"],
                                               riders=[w_up, w_out] if convert else [])
    w_out16, w_up16, w_down16 = (gdn_riders[1], gdn_riders[0], gla_riders[0]) if convert else (
        w_out, w_up, w_down)
    x1, h2 = _out_proj(x, o_a, o_b, mod3, mod_row0, w_out16, norm2, cfg["out_bt"], cfg["out_lt"])
    y, n_ffn = _ffn(h2, x1, s_ffn, mod3, mod_row0, w_up16, w_down16, ffn_conv_w, ffn_conv_b,
                    final_norm, cfg["ffn_bt"], cfg["ffn_lt"])
    return (y, n_gla[None], n_gdn[None], n_conv[None], n_ffn[:, -1][None]), (w_out16, w_up16, w_down16)


def kernel(x_prompt, x_sample, c_prompt, c_sample, state_gla, state_gdn, state_gdn_conv, state_ffn_conv, w_ada, b_ada, norm1, w_in, gla_wg, gla_bg, gla_norm, gdn_conv_w, gdn_a_log, gdn_dt_bias, gdn_norm, w_out, norm2, w_up, ffn_conv_w, ffn_conv_b, w_down, final_norm):
    bp = x_prompt.shape[0]
    bs = x_sample.shape[0]

    w_in_t = jnp.swapaxes(w_in[0], 0, 1)
    w_small_t = jnp.concatenate(
        [w_in_t[IN_OFF_LR:IN_OFF_GDN], w_in_t[IN_OFF_BETA:],
         jnp.zeros((SMALL_COLS - GLA_GATE_RANK - 2 * GDN_HEADS, D_MODEL), F32)], axis=0)
    gate_vec = jnp.zeros((2, SMALL_COLS), F32)
    gate_vec = gate_vec.at[0, ALPHA_OFF:ALPHA_OFF + GDN_HEADS].set(gdn_a_log[0])
    gate_vec = gate_vec.at[1, ALPHA_OFF:ALPHA_OFF + GDN_HEADS].set(gdn_dt_bias[0])

    c_all = jnp.concatenate([c_sample, c_prompt], axis=0)
    mod = _ada_mod(c_all, w_ada[0], b_ada[0])
    mod3 = mod.reshape(bs + bp, 1, N_MOD * D_MODEL)

    weights = (w_in_t, w_small_t, norm1[0], gla_wg[0], gla_bg[0], gla_norm[0], gdn_conv_w[0], gate_vec,
               gdn_norm[0], w_out[0], norm2[0], w_up[0], w_down[0], ffn_conv_w[0], ffn_conv_b[0],
               final_norm)

    fresh = (jnp.zeros((bp, GLA_HEADS, GLA_DK, GLA_DV), F32),
             jnp.zeros((bp, GDN_HEADS, GDN_DK, GDN_DV), F32),
             jnp.zeros((bp, GDN_CONV - 1, GDN_CONV_DIM), F32),
             jnp.zeros((bp, FFN_CONV - 1, D_FF), F32))
    cfg_p = dict(in_bt=1, in_lt=1024, gla_bt=1, gla_lt=512, gdn_bt=1, gdn_lt=256,
                 out_bt=1, out_lt=512, ffn_bt=1, ffn_lt=1024)
    (y_p, p_gla, p_gdn, p_conv, p_ffn), (w_out16, w_up16, w_down16) = _trunk(
        x_prompt, mod3, bs, fresh, weights, cfg_p)

    carried = (state_gla[0], state_gdn[0], state_gdn_conv[0], state_ffn_conv[0])
    ls = x_sample.shape[1]
    cfg_s = dict(in_bt=bs, in_lt=ls, gla_bt=4, gla_lt=ls, gdn_bt=4, gdn_lt=ls,
                 out_bt=bs // 2, out_lt=ls, ffn_bt=bs, ffn_lt=ls)
    weights_s = weights[:9] + (w_out16, weights[10], w_up16, w_down16) + weights[13:]
    (y_s, s_gla, s_gdn, s_conv, s_ffn), _ = _trunk(x_sample, mod3, 0, carried, weights_s, cfg_s)
    return (y_p, y_s, p_gla, p_gdn, p_conv, p_ffn, s_gla, s_gdn, s_conv, s_ffn)
```

```python
import functools

import jax
import jax.numpy as jnp
from jax import lax
from jax.experimental import pallas as pl
from jax.experimental.pallas import tpu as pltpu

F32 = jnp.float32
BF16 = jnp.bfloat16

D_MODEL = 2048
CHUNK = 64
GLA_HEADS = 4
GLA_DK = 128
GLA_DV = 256
GLA_WIDTH = GLA_HEADS * GLA_DV
GLA_GATE_RANK = 16
GLA_GATE_NORM = 16.0
GLA_MILD_LOG_DECAY = 60.0
GDN_HEADS = 8
GDN_DK = 128
GDN_DV = 128
GDN_WIDTH = GDN_HEADS * GDN_DV
GDN_CONV = 4
GDN_CONV_DIM = 3 * GDN_WIDTH
D_FF = 5632
FFN_CONV = 3
N_MOD = 6
EPS = 1e-6

GLA_COLS = 2 * GLA_HEADS * GLA_DK + 2 * GLA_WIDTH
GDN_COLS = 4 * GDN_WIDTH
MAIN_COLS = GLA_COLS + GDN_COLS
SMALL_COLS = 128
LR_OFF, BETA_OFF, ALPHA_OFF = 0, GLA_GATE_RANK, GLA_GATE_RANK + GDN_HEADS
IN_OFF_LR = GLA_COLS
IN_OFF_GDN = IN_OFF_LR + GLA_GATE_RANK
IN_OFF_BETA = IN_OFF_GDN + GDN_COLS

SUBLANES = 8
VMEM_LIMIT_BYTES = 56 * 1024 * 1024
BIG_VMEM_LIMIT_BYTES = 60 * 1024 * 1024

NT_DIMS = (((1,), (1,)), ((), ()))
TN_DIMS = (((0,), (0,)), ((), ()))


def _dot(a, b):
    return jnp.dot(a, b, preferred_element_type=F32)


def _dot_nt(a, b):
    return lax.dot_general(a, b, NT_DIMS, preferred_element_type=F32)


def _dot_tn(a, b):
    return lax.dot_general(a, b, TN_DIMS, preferred_element_type=F32)


def _dot_f32(a, b):
    return jnp.dot(a, b, preferred_element_type=F32, precision=lax.Precision.HIGHEST)


def _dot_many(a_list, b_list):
    return [_dot(a.astype(BF16), b.astype(BF16)) for a, b in zip(a_list, b_list)]


def _silu(x):
    return x * jax.nn.sigmoid(x)


def _softplus(x):
    return jnp.maximum(x, 0.0) + jnp.log1p(jnp.exp(-jnp.abs(x)))


def _log_sigmoid(x):
    return jnp.minimum(x, 0.0) - jnp.log1p(jnp.exp(-jnp.abs(x)))


def _params(*semantics, vmem_limit_bytes=VMEM_LIMIT_BYTES):
    return pltpu.CompilerParams(dimension_semantics=semantics, vmem_limit_bytes=vmem_limit_bytes)


def _rider_specs(rider, grid):
    steps = grid[0] * grid[1]
    rows, cols = rider.shape
    assert rows % steps == 0, (rows, steps)
    spec = pl.BlockSpec((rows // steps, cols), lambda i, t: (i * grid[1] + t, 0))
    return spec, spec, jax.ShapeDtypeStruct(rider.shape, BF16)


def _tri_masks():
    row = lax.broadcasted_iota(jnp.int32, (CHUNK, CHUNK), 0)
    col = lax.broadcasted_iota(jnp.int32, (CHUNK, CHUNK), 1)
    return row >= col, row > col


def _ada_kernel(c_ref, w_ref, b_ref, o_ref):
    s = _silu(c_ref[...]).astype(BF16)
    o_ref[...] = _dot(s, w_ref[...].astype(BF16)) + b_ref[...]


def _ada_mod(c_all, w_ada, b_ada):
    rows = c_all.shape[0]
    n = w_ada.shape[1]
    tn = 1024
    return pl.pallas_call(
        _ada_kernel,
        grid=(n // tn,),
        in_specs=[
            pl.BlockSpec((rows, D_MODEL), lambda j: (0, 0)),
            pl.BlockSpec((D_MODEL, tn), lambda j: (0, j)),
            pl.BlockSpec((1, tn), lambda j: (0, j)),
        ],
        out_specs=pl.BlockSpec((rows, tn), lambda j: (0, j)),
        out_shape=jax.ShapeDtypeStruct((rows, n), F32),
        compiler_params=_params("arbitrary"),
        name="ada_mod",
    )(c_all, w_ada, b_ada.reshape(1, n))


W_RING = 3


def _in_proj_kernel(x_ref, sc_ref, sh_ref, n1_ref, w_hbm, ws_ref, o_ref, os_ref, h_scr, w_ring, w_sems,
                    *, n_a):
    bt, lt, d = x_ref.shape
    tn = o_ref.shape[-1]
    n_t, n_j = pl.num_programs(1), pl.num_programs(2)
    j = pl.program_id(2)
    step = (pl.program_id(0) * n_t + pl.program_id(1)) * n_j + j
    n_steps = pl.num_programs(0) * n_t * n_j

    def tile_copy(s):
        jj = lax.rem(s, n_j)
        start = jnp.where(jj < n_a, jj * tn, IN_OFF_GDN + (jj - n_a) * tn)
        slot = lax.rem(s, W_RING)
        return pltpu.make_async_copy(
            w_hbm.at[pl.ds(pl.multiple_of(start, SUBLANES), tn), :], w_ring.at[slot], w_sems.at[slot])

    @pl.when(step == 0)
    def _():
        for s in range(W_RING - 1):
            tile_copy(s).start()

    @pl.when(step + (W_RING - 1) < n_steps)
    def _():
        tile_copy(step + (W_RING - 1)).start()

    @pl.when(j == 0)
    def _():
        x = x_ref[...]
        y = x * lax.rsqrt(jnp.mean(x * x, axis=-1, keepdims=True) + EPS) * n1_ref[...]
        h = y * (1.0 + sc_ref[...]) + sh_ref[...]
        hb = h.reshape(bt * lt, d).astype(BF16)
        h_scr[...] = hb
        os_ref[...] = _dot_nt(hb, ws_ref[...].astype(BF16)).reshape(os_ref.shape)

    tile_copy(step).wait()
    w = w_ring[lax.rem(step, W_RING)].astype(BF16)
    o_ref[...] = _dot_nt(h_scr[...], w).reshape(o_ref.shape)


def _in_proj(x, mod3, mod_row0, norm1, w_in_t, w_small_t, bt, lt):
    b, l, d = x.shape
    tn = 1024
    n_a = GLA_COLS // tn
    mrow = mod_row0 // bt
    grid = (b // bt, l // lt, MAIN_COLS // tn)
    assert grid[0] * grid[1] * grid[2] >= W_RING - 1
    return pl.pallas_call(
        functools.partial(_in_proj_kernel, n_a=n_a),
        grid=grid,
        in_specs=[
            pl.BlockSpec((bt, lt, d), lambda i, t, j: (i, t, 0)),
            pl.BlockSpec((bt, 1, d), lambda i, t, j: (mrow + i, 0, 1)),
            pl.BlockSpec((bt, 1, d), lambda i, t, j: (mrow + i, 0, 0)),
            pl.BlockSpec((1, 1, d), lambda i, t, j: (0, 0, 0)),
            pl.BlockSpec(memory_space=pl.ANY),
            pl.BlockSpec((SMALL_COLS, d), lambda i, t, j: (0, 0)),
        ],
        out_specs=[
            pl.BlockSpec((bt, lt, tn), lambda i, t, j: (i, t, j)),
            pl.BlockSpec((bt, lt, SMALL_COLS), lambda i, t, j: (i, t, 0)),
        ],
        out_shape=[
            jax.ShapeDtypeStruct((b, l, MAIN_COLS), F32),
            jax.ShapeDtypeStruct((b, l, SMALL_COLS), F32),
        ],
        scratch_shapes=[pltpu.VMEM((bt * lt, d), BF16),
                        pltpu.VMEM((W_RING, tn, d), F32),
                        pltpu.SemaphoreType.DMA((W_RING,))],
        compiler_params=_params("arbitrary", "arbitrary", "arbitrary",
                                vmem_limit_bytes=BIG_VMEM_LIMIT_BYTES),
        name="in_proj",
    )(x, mod3, mod3, norm1.reshape(1, 1, d), w_in_t, w_small_t)


def _gla_levels():
    row = lax.broadcasted_iota(jnp.int32, (CHUNK, CHUNK), 0)
    col = lax.broadcasted_iota(jnp.int32, (CHUNK, CHUNK), 1)
    ops, masks = [], []
    half = CHUNK // 2
    while half >= 1:
        shift = half.bit_length()
        parent_row = jnp.left_shift(jnp.right_shift(row, shift), shift)
        parent_col = jnp.left_shift(jnp.right_shift(col, shift), shift)
        m_row = parent_row + (half - 1)
        right_row = row > m_row
        in_span = (right_row & (col > m_row) & (col <= row)) | (
            jnp.logical_not(right_row) & (col > row) & (col <= m_row))
        ops.append(jnp.where(in_span, 1.0, 0.0))
        masks.append((parent_row == parent_col) & right_row & (col <= m_row))
        half //= 2
    return ops, masks, row == col


def _gla_kernel(q_ref, k_ref, v_ref, r_ref, sm_ref, wg_ref, bg_ref, gn_ref, s0_ref, *rest, n_riders):
    src_refs, (o_ref, s_ref), rest = rest[:n_riders], rest[n_riders:n_riders + 2], rest[n_riders + 2:]
    dst_refs, (st_scr, attn_scr) = rest[:n_riders], rest[n_riders:]
    for src_ref, dst_ref in zip(src_refs, dst_refs):
        dst_ref[...] = src_ref[...].astype(dst_ref.dtype)
    t = pl.program_id(1)
    bt, lt = q_ref.shape[0], q_ref.shape[1]
    n_chunks = lt // CHUNK
    causal, _ = _tri_masks()
    tri = causal.astype(F32)
    scale = GLA_DK ** -0.5

    @pl.when(t == 0)
    def _():
        for b in range(bt):
            for h in range(GLA_HEADS):
                st_scr[b, h] = s0_ref[b, h].T

    wg = wg_ref[...].astype(BF16)
    heads = range(GLA_HEADS)
    kcs = [slice(h * GLA_DK, (h + 1) * GLA_DK) for h in heads]
    vcs = [slice(h * GLA_DV, (h + 1) * GLA_DV) for h in heads]
    items = [(b, pl.ds(c * CHUNK, CHUNK)) for c in range(n_chunks) for b in range(bt)]

    def scaled_q(b, rows):
        return [q_ref[b, rows, kc] * scale for kc in kcs]

    log_a, big_g = [], []
    for b, rows in items:
        a_lr = sm_ref[b, rows, LR_OFF:LR_OFF + GLA_GATE_RANK].astype(BF16)
        la = _log_sigmoid(_dot(a_lr, wg) + bg_ref[...]) / GLA_GATE_NORM
        log_a.append(la)
        big_g.append(_dot_f32(tri, la))
    total = big_g[0][CHUNK - 1:CHUNK, :]
    for g in big_g[1:]:
        total = jnp.minimum(total, g[CHUNK - 1:CHUNK, :])
    mild = jnp.min(total) > -GLA_MILD_LOG_DECAY

    @pl.when(mild)
    def _():
        for n, (b, rows) in enumerate(items):
            g_mid = big_g[n][CHUNK // 2 - 1:CHUNK // 2, :]
            e_q = jnp.exp(big_g[n] - g_mid)
            e_k = jnp.exp(g_mid - big_g[n])
            q = scaled_q(b, rows)
            scores = [_dot_nt((qq * e_q[:, kc]).astype(BF16),
                              (k_ref[b, rows, kc] * e_k[:, kc]).astype(BF16))
                      for qq, kc in zip(q, kcs)]
            for h in heads:
                attn_scr[n, h] = jnp.where(causal, scores[h], 0.0)

    @pl.when(jnp.logical_not(mild))
    def _():
        level_ops, level_masks, diag = _gla_levels()
        level_op = jnp.concatenate(level_ops, axis=0)
        for n, (b, rows) in enumerate(items):
            sums = _dot_f32(level_op, log_a[n])
            q = scaled_q(b, rows)
            k = [k_ref[b, rows, kc] for kc in kcs]
            scores = [jnp.where(diag, _dot_nt(qq.astype(BF16), kk.astype(BF16)), 0.0)
                      for qq, kk in zip(q, k)]
            for i, mask in enumerate(level_masks):
                e = jnp.exp(sums[i * CHUNK:(i + 1) * CHUNK])
                part = [_dot_nt((qq * e[:, kc]).astype(BF16), (kk * e[:, kc]).astype(BF16))
                        for qq, kk, kc in zip(q, k, kcs)]
                scores = [jnp.where(mask, p, a) for p, a in zip(part, scores)]
            for h in heads:
                attn_scr[n, h] = scores[h]

    pending = []
    for n, (b, rows) in enumerate(items):
        g_last = big_g[n][CHUNK - 1:CHUNK, :]
        e_g = jnp.exp(big_g[n])
        e_kl = jnp.exp(g_last - big_g[n])
        e_l = jnp.exp(g_last)
        q = scaled_q(b, rows)
        v = [v_ref[b, rows, vc].astype(BF16) for vc in vcs]
        v_k = [_dot_tn(vv, (k_ref[b, rows, kc] * e_kl[:, kc]).astype(BF16)) for vv, kc in zip(v, kcs)]
        a_v = [_dot(attn_scr[n, h].astype(BF16), v[h]) for h in heads]
        q_g = [(qq * e_g[:, kc]).astype(BF16) for qq, kc in zip(q, kcs)]
        pending.append((q_g, a_v, v_k, [e_l[:, kc] for kc in kcs]))
    for c in range(n_chunks):
        group = range(c * bt, (c + 1) * bt)
        st = [[st_scr[items[n][0], h] for h in heads] for n in group]
        q_s = [[_dot_nt(qq, ss.astype(BF16)) for qq, ss in zip(pending[n][0], st_n)]
               for n, st_n in zip(group, st)]
        for n, st_n, q_s_n in zip(group, st, q_s):
            b, rows = items[n]
            _, a_v, v_k, e_l = pending[n]
            for h in heads:
                st_scr[b, h] = st_n[h] * e_l[h] + v_k[h]
                o = a_v[h] + q_s_n[h]
                o = o * lax.rsqrt(jnp.mean(o * o, axis=-1, keepdims=True) + EPS) * gn_ref[...]
                o = o * _silu(r_ref[b, rows, vcs[h]])
                o_ref[b, rows, vcs[h]] = o.astype(o_ref.dtype)

    @pl.when(t == pl.num_programs(1) - 1)
    def _():
        for b in range(bt):
            for h in range(GLA_HEADS):
                s_ref[b, h] = st_scr[b, h].T


def _gla_mix(proj, small, s0, gla_wg, gla_bg, gla_norm, bt, lt, riders=()):
    b, l, _ = proj.shape
    kw = GLA_HEADS * GLA_DK
    grid = (b // bt, l // lt)
    rider_args = list(riders)
    riders = [_rider_specs(r, grid) for r in rider_args]
    return pl.pallas_call(
        functools.partial(_gla_kernel, n_riders=len(riders)),
        grid=grid,
        in_specs=[
            pl.BlockSpec((bt, lt, kw), lambda i, t: (i, t, 0)),
            pl.BlockSpec((bt, lt, kw), lambda i, t: (i, t, 1)),
            pl.BlockSpec((bt, lt, GLA_WIDTH), lambda i, t: (i, t, 1)),
            pl.BlockSpec((bt, lt, GLA_WIDTH), lambda i, t: (i, t, 2)),
            pl.BlockSpec((bt, lt, SMALL_COLS), lambda i, t: (i, t, 0)),
            pl.BlockSpec((GLA_GATE_RANK, kw), lambda i, t: (0, 0)),
            pl.BlockSpec((1, kw), lambda i, t: (0, 0)),
            pl.BlockSpec((1, GLA_DV), lambda i, t: (0, 0)),
            pl.BlockSpec((bt, GLA_HEADS, GLA_DK, GLA_DV), lambda i, t: (i, 0, 0, 0)),
        ] + [r[0] for r in riders],
        out_specs=[
            pl.BlockSpec((bt, lt, GLA_WIDTH), lambda i, t: (i, t, 0)),
            pl.BlockSpec((bt, GLA_HEADS, GLA_DK, GLA_DV), lambda i, t: (i, 0, 0, 0)),
        ] + [r[1] for r in riders],
        out_shape=[
            jax.ShapeDtypeStruct((b, l, GLA_WIDTH), BF16),
            jax.ShapeDtypeStruct((b, GLA_HEADS, GLA_DK, GLA_DV), F32),
        ] + [r[2] for r in riders],
        scratch_shapes=[pltpu.VMEM((bt, GLA_HEADS, GLA_DV, GLA_DK), F32),
                        pltpu.VMEM((bt * (lt // CHUNK), GLA_HEADS, CHUNK, CHUNK), F32)],
        compiler_params=_params("arbitrary", "arbitrary"),
        name="gla_mix",
    )(proj, proj, proj, proj, small, gla_wg, gla_bg.reshape(1, kw), gla_norm.reshape(1, GLA_DV), s0,
      *rider_args)


CONV_PAD = SUBLANES
INV_BASE_LOG2 = 3
GDN_CHUNK_GROUP = 4


def _inverse_masks():
    row = lax.broadcasted_iota(jnp.int32, (CHUNK, CHUNK), 0)
    col = lax.broadcasted_iota(jnp.int32, (CHUNK, CHUNK), 1)

    def same_block(log2):
        return jnp.right_shift(row, log2) == jnp.right_shift(col, log2)

    base = same_block(INV_BASE_LOG2)
    merges = []
    log2 = INV_BASE_LOG2
    while (1 << log2) < CHUNK:
        merges.append(same_block(log2 + 1) & jnp.logical_not(same_block(log2)))
        log2 += 1
    return base, merges


def _unit_lower_inverse_minus_eye(lower, base, merges):
    neg = [jnp.where(base, -l, 0.0) for l in lower]
    x = neg
    p = neg
    for _ in range(INV_BASE_LOG2 - 1):
        p = _dot_many(p, p)
        xp = _dot_many(x, p)
        x = [a + b + c for a, b, c in zip(x, p, xp)]
    for m in merges:
        c = [jnp.where(m, l, 0.0) for l in lower]
        w = [a + b for a, b in zip(c, _dot_many(x, c))]
        wx = _dot_many(w, x)
        x = [a - (b + d) for a, b, d in zip(x, w, wx)]
    return x


def _gdn_kernel(x_ref, g_ref, sm_ref, cw_ref, ga_ref, gn_ref, c0_ref, s0_ref, *rest, n_riders):
    src_refs, (o_ref, s_ref, c_ref), rest = rest[:n_riders], rest[n_riders:n_riders + 3], rest[n_riders + 3:]
    dst_refs, (xbuf,) = rest[:n_riders], rest[n_riders:]
    for src_ref, dst_ref in zip(src_refs, dst_refs):
        dst_ref[...] = src_ref[...].astype(dst_ref.dtype)
    t = pl.program_id(1)
    bt, lt = x_ref.shape[0], x_ref.shape[1]
    causal, strict = _tri_masks()
    tri = causal.astype(F32)
    inv_base, inv_merges = _inverse_masks()
    scale = GDN_DK ** -0.5

    @pl.when(t == 0)
    def _():
        xbuf[:, 0:CONV_PAD, :] = jnp.zeros((bt, CONV_PAD, GDN_CONV_DIM), F32)
        xbuf[:, CONV_PAD - (GDN_CONV - 1):CONV_PAD, :] = c0_ref[...]
        s_ref[...] = s0_ref[...]

    @pl.when(t > 0)
    def _():
        xbuf[:, 0:CONV_PAD, :] = xbuf[:, lt:lt + CONV_PAD, :]

    xbuf[:, CONV_PAD:CONV_PAD + lt, :] = x_ref[...]
    c_ref[...] = xbuf[:, lt + CONV_PAD - (GDN_CONV - 1):lt + CONV_PAD, :]

    def conv_silu(b, r0, cols):
        acc = cw_ref[GDN_CONV - 1:GDN_CONV, cols] * xbuf[b, r0 + CONV_PAD:r0 + CONV_PAD + CHUNK, cols]
        for tap in range(GDN_CONV - 1):
            start = r0 + CONV_PAD - (GDN_CONV - 1 - tap)
            acc = acc + cw_ref[tap:tap + 1, cols] * xbuf[b, start:start + CHUNK, cols]
        return _silu(acc)

    def l2norm(x):
        return x * lax.rsqrt(jnp.sum(x * x, axis=-1, keepdims=True) + EPS)

    heads = range(GDN_HEADS)
    def prep_gates(b, c):
        sm = sm_ref[b, pl.ds(c * CHUNK, CHUNK), :]
        gate = -jnp.exp(ga_ref[0:1, :]) * _softplus(sm + ga_ref[1:2, :])
        big_g = _dot_f32(tri, gate)
        return jax.nn.sigmoid(sm), big_g, big_g.T

    def prep_head(b, c, gates, h):
        beta, big_g, big_gt = gates
        r0 = c * CHUNK
        g_col = big_g[:, ALPHA_OFF + h:ALPHA_OFF + h + 1]
        g_row = big_gt[ALPHA_OFF + h:ALPHA_OFF + h + 1, :]
        b_col = beta[:, BETA_OFF + h:BETA_OFF + h + 1]
        g_last = g_col[CHUNK - 1:CHUNK, :]
        e_g = jnp.exp(g_col)
        q = l2norm(conv_silu(b, r0, slice(h * GDN_DK, (h + 1) * GDN_DK))) * scale
        k = l2norm(conv_silu(b, r0, slice(GDN_WIDTH + h * GDN_DK, GDN_WIDTH + (h + 1) * GDN_DK)))
        v = conv_silu(b, r0, slice(2 * GDN_WIDTH + h * GDN_DV, 2 * GDN_WIDTH + (h + 1) * GDN_DV))
        kb = k * b_col
        return dict(
            decay=jnp.where(causal, jnp.exp(g_col - g_row), 0.0),
            rhs=jnp.concatenate([v * b_col, kb * e_g], axis=-1),
            q16=q.astype(BF16), k16=k.astype(BF16), kb16=kb.astype(BF16),
            q_g=(q * e_g).astype(BF16),
            k_dec=(k * jnp.exp(g_last - g_col)).astype(BF16),
            e_last=jnp.exp(g_last))

    def key_products(p):
        kk_t = [_dot_nt(ph["kb16"], ph["k16"]) for ph in p]
        qk_t = [_dot_nt(ph["q16"], ph["k16"]) for ph in p]
        lower = [jnp.where(strict, m * ph["decay"], 0.0) for m, ph in zip(kk_t, p)]
        attn = [jnp.where(causal, m * ph["decay"], 0.0).astype(BF16) for m, ph in zip(qk_t, p)]
        return lower, attn

    n_chunks = lt // CHUNK
    items = [(b, c) for c in range(n_chunks) for b in range(bt)]
    pending = []
    for i0 in range(0, len(items), GDN_CHUNK_GROUP):
        group = items[i0:i0 + GDN_CHUNK_GROUP]
        gates = [prep_gates(b, c) for b, c in group]
        p = [prep_head(b, c, g, h) for (b, c), g in zip(group, gates) for h in heads]
        lower, attn = key_products(p)
        xinv = _unit_lower_inverse_minus_eye(lower, inv_base, inv_merges)
        rhs = [ph["rhs"] for ph in p]
        sol = [r + xr for r, xr in zip(rhs, _dot_many(xinv, rhs))]
        for n in range(len(group)):
            sl = slice(n * GDN_HEADS, (n + 1) * GDN_HEADS)
            pending.append((attn[sl], [so[:, :GDN_DV] for so in sol[sl]],
                            [so[:, GDN_DV:].astype(BF16) for so in sol[sl]],
                            [ph["q_g"] for ph in p[sl]], [ph["k_dec"] for ph in p[sl]],
                            [ph["e_last"] for ph in p[sl]]))
    for c in range(n_chunks):
        group = range(c * bt, (c + 1) * bt)
        attn, sol_v, sol_k, q_g, k_dec, e_last = (
            [x for n in group for x in pending[n][field]] for field in range(6))
        where = [(items[n][0], h) for n in group for h in heads]
        s = [s_ref[b, h] for b, h in where]
        s16 = [ss.astype(BF16) for ss in s]
        k_s = [_dot(a, ss) for a, ss in zip(sol_k, s16)]
        q_s = [_dot(a, ss) for a, ss in zip(q_g, s16)]
        u16 = [(sv - ks).astype(BF16) for sv, ks in zip(sol_v, k_s)]
        a_u = [_dot(a, uu) for a, uu in zip(attn, u16)]
        k_u = [_dot_tn(kd, uu) for kd, uu in zip(k_dec, u16)]
        rows = pl.ds(c * CHUNK, CHUNK)
        for i, (b, h) in enumerate(where):
            hc = slice(h * GDN_DV, (h + 1) * GDN_DV)
            s_ref[b, h] = e_last[i] * s[i] + k_u[i]
            o = q_s[i] + a_u[i]
            o = o * lax.rsqrt(jnp.mean(o * o, axis=-1, keepdims=True) + EPS) * gn_ref[...]
            o = o * _silu(g_ref[b, rows, hc])
            o_ref[b, rows, hc] = o.astype(o_ref.dtype)


def _gdn_mix(proj, small, c0, s0, conv_w, gate_vec, gdn_norm, bt, lt, riders=()):
    b, l, _ = proj.shape
    grid = (b // bt, l // lt)
    rider_args = list(riders)
    riders = [_rider_specs(r, grid) for r in rider_args]
    return pl.pallas_call(
        functools.partial(_gdn_kernel, n_riders=len(riders)),
        grid=grid,
        in_specs=[
            pl.BlockSpec((bt, lt, GDN_CONV_DIM), lambda i, t: (i, t, 1)),
            pl.BlockSpec((bt, lt, GDN_WIDTH), lambda i, t: (i, t, 6)),
            pl.BlockSpec((bt, lt, SMALL_COLS), lambda i, t: (i, t, 0)),
            pl.BlockSpec((GDN_CONV, GDN_CONV_DIM), lambda i, t: (0, 0)),
            pl.BlockSpec((2, SMALL_COLS), lambda i, t: (0, 0)),
            pl.BlockSpec((1, GDN_DV), lambda i, t: (0, 0)),
            pl.BlockSpec((bt, GDN_CONV - 1, GDN_CONV_DIM), lambda i, t: (i, 0, 0)),
            pl.BlockSpec((bt, GDN_HEADS, GDN_DK, GDN_DV), lambda i, t: (i, 0, 0, 0)),
        ] + [r[0] for r in riders],
        out_specs=[
            pl.BlockSpec((bt, lt, GDN_WIDTH), lambda i, t: (i, t, 0)),
            pl.BlockSpec((bt, GDN_HEADS, GDN_DK, GDN_DV), lambda i, t: (i, 0, 0, 0)),
            pl.BlockSpec((bt, GDN_CONV - 1, GDN_CONV_DIM), lambda i, t: (i, 0, 0)),
        ] + [r[1] for r in riders],
        out_shape=[
            jax.ShapeDtypeStruct((b, l, GDN_WIDTH), BF16),
            jax.ShapeDtypeStruct((b, GDN_HEADS, GDN_DK, GDN_DV), F32),
            jax.ShapeDtypeStruct((b, GDN_CONV - 1, GDN_CONV_DIM), F32),
        ] + [r[2] for r in riders],
        scratch_shapes=[pltpu.VMEM((bt, lt + CONV_PAD, GDN_CONV_DIM), F32)],
        compiler_params=_params("arbitrary", "arbitrary"),
        name="gdn_mix",
    )(proj, proj, small, conv_w, gate_vec, gdn_norm.reshape(1, GDN_DV), c0, s0, *rider_args)


def _out_proj_kernel(x_ref, oa_ref, ob_ref, w_ref, g1_ref, sc_ref, sh_ref, n2_ref, x1_ref, h2_ref,
                     *, n_sub):
    bt, lt, d = x_ref.shape
    if bt == 1:
        ls = lt // n_sub
        subs = [(slice(0, 1), slice(i * ls, (i + 1) * ls)) for i in range(n_sub)]
    else:
        bs = bt // n_sub
        subs = [(slice(i * bs, (i + 1) * bs), slice(0, lt)) for i in range(n_sub)]
    mixes = []
    for bsl, lsl in subs:
        oa = oa_ref[bsl, lsl, :]
        nb, nl, _ = oa.shape
        oa = oa.reshape(nb * nl, GLA_WIDTH)
        ob = ob_ref[bsl, lsl, :].reshape(nb * nl, GDN_WIDTH)
        mix = _dot(oa, w_ref[0:GLA_WIDTH, :]) + _dot(ob, w_ref[GLA_WIDTH:GLA_WIDTH + GDN_WIDTH, :])
        mixes.append(mix.reshape(nb, nl, d))
    for (bsl, lsl), mix in zip(subs, mixes):
        x1 = x_ref[bsl, lsl, :] + g1_ref[bsl] * mix
        x1_ref[bsl, lsl, :] = x1
        y = x1 * lax.rsqrt(jnp.mean(x1 * x1, axis=-1, keepdims=True) + EPS) * n2_ref[...]
        h2_ref[bsl, lsl, :] = (y * (1.0 + sc_ref[bsl]) + sh_ref[bsl]).astype(h2_ref.dtype)


def _out_proj(x, o_a, o_b, mod3, mod_row0, w_out16, norm2, bt, lt):
    b, l, d = x.shape
    mrow = mod_row0 // bt
    return pl.pallas_call(
        functools.partial(_out_proj_kernel, n_sub=2),
        grid=(b // bt, l // lt),
        in_specs=[
            pl.BlockSpec((bt, lt, d), lambda i, t: (i, t, 0)),
            pl.BlockSpec((bt, lt, GLA_WIDTH), lambda i, t: (i, t, 0)),
            pl.BlockSpec((bt, lt, GDN_WIDTH), lambda i, t: (i, t, 0)),
            pl.BlockSpec((GLA_WIDTH + GDN_WIDTH, d), lambda i, t: (0, 0)),
            pl.BlockSpec((bt, 1, d), lambda i, t: (mrow + i, 0, 2)),
            pl.BlockSpec((bt, 1, d), lambda i, t: (mrow + i, 0, 4)),
            pl.BlockSpec((bt, 1, d), lambda i, t: (mrow + i, 0, 3)),
            pl.BlockSpec((1, 1, d), lambda i, t: (0, 0, 0)),
        ],
        out_specs=[
            pl.BlockSpec((bt, lt, d), lambda i, t: (i, t, 0)),
            pl.BlockSpec((bt, lt, d), lambda i, t: (i, t, 0)),
        ],
        out_shape=[
            jax.ShapeDtypeStruct((b, l, d), F32),
            jax.ShapeDtypeStruct((b, l, d), BF16),
        ],
        compiler_params=_params("arbitrary", "arbitrary"),
        name="out_proj",
    )(x, o_a, o_b, w_out16, mod3, mod3, mod3, norm2.reshape(1, 1, d))


FFN_PAD = SUBLANES
HALO_ROWS = 2 * SUBLANES


def _ffn_kernel(h_ref, halo_ref, wg_ref, wv_ref, wd_ref, cw_ref, cb_ref, st_ref, x1_hbm, g2_ref,
                fn_ref, y_ref, so_ref, gbuf, x1_buf, x1_sem, *, has_halo, n_sub):
    ib = pl.program_id(0)
    t = pl.program_id(1)
    f = pl.program_id(2)
    bt, lt, d = h_ref.shape
    tf = wg_ref.shape[1]
    x1_copy = pltpu.make_async_copy(
        x1_hbm.at[pl.ds(ib * bt, bt), pl.ds(t * lt, lt), :], x1_buf, x1_sem)
    if bt == 1:
        ls = lt // n_sub
        subs = [(slice(0, 1), slice(i * ls, (i + 1) * ls)) for i in range(n_sub)]
    else:
        bs = bt // n_sub
        subs = [(slice(i * bs, (i + 1) * bs), slice(0, lt)) for i in range(n_sub)]

    @pl.when(f == 0)
    def _():
        x1_copy.start()
        y_ref[...] = jnp.zeros(y_ref.shape, F32)

    prev = st_ref[...]
    gates, vals = [], []
    for n, (bsl, lsl) in enumerate(subs):
        hb = h_ref[bsl, lsl, :]
        nb, nl, _ = hb.shape
        hb = hb.reshape(nb * nl, d)
        if has_halo and n == 0:
            ext = _dot(jnp.concatenate([halo_ref[0], hb], axis=0), wg_ref[...])
            halo_gate = ext[HALO_ROWS - (FFN_CONV - 1):HALO_ROWS, :]
            prev = jnp.where(t == 0, prev, halo_gate[None])
            gate = ext[HALO_ROWS:, :].reshape(nb, nl, tf)
        else:
            gate = _dot(hb, wg_ref[...]).reshape(nb, nl, tf)
        gbuf[bsl, FFN_PAD + lsl.start:FFN_PAD + lsl.stop, :] = gate
        gates.append(gate)
        vals.append(_dot(hb, wv_ref[...]))
    gbuf[:, FFN_PAD - (FFN_CONV - 1):FFN_PAD, :] = prev
    so_ref[...] = gbuf[:, FFN_PAD + lt - (FFN_CONV - 1):FFN_PAD + lt, :].reshape(so_ref.shape)
    acts = []
    for (bsl, lsl), gate, val in zip(subs, gates, vals):
        nb, nl, _ = gate.shape
        conv = cw_ref[FFN_CONV - 1:FFN_CONV, :] * gate
        for tap in range(FFN_CONV - 1):
            back = FFN_CONV - 1 - tap
            conv = conv + cw_ref[tap:tap + 1, :] * gbuf[bsl, FFN_PAD + lsl.start - back:FFN_PAD + lsl.stop - back, :]
        acts.append((_silu(conv + cb_ref[...]).reshape(nb * nl, tf) * val).astype(BF16))
    parts = [_dot(act, wd_ref[...]) for act in acts]
    for (bsl, lsl), gate, part in zip(subs, gates, parts):
        nb, nl, _ = gate.shape
        y_ref[bsl, lsl, :] += part.reshape(nb, nl, d)

    @pl.when(f == pl.num_programs(2) - 1)
    def _():
        x1_copy.wait()
        x2 = x1_buf[...] + g2_ref[...] * y_ref[...]
        y = x2 * lax.rsqrt(jnp.mean(x2 * x2, axis=-1, keepdims=True) + EPS) * fn_ref[...]
        y_ref[...] = y


def _ffn(h2, x1, st0, mod3, mod_row0, w_up16, w_down16, conv_w, conv_b, final_norm, bt, lt):
    b, l, d = h2.shape
    tf = 512
    nf = D_FF // tf
    mrow = mod_row0 // bt
    has_halo = lt < l
    halo_blocks = lt // HALO_ROWS
    kern = functools.partial(_ffn_kernel, has_halo=has_halo, n_sub=4)
    return pl.pallas_call(
        kern,
        grid=(b // bt, l // lt, nf),
        in_specs=[
            pl.BlockSpec((bt, lt, d), lambda i, t, f: (i, t, 0)),
            pl.BlockSpec((1, HALO_ROWS, d), lambda i, t, f: (i, jnp.maximum(t * halo_blocks - 1, 0), 0)),
            pl.BlockSpec((d, tf), lambda i, t, f: (0, f)),
            pl.BlockSpec((d, tf), lambda i, t, f: (0, nf + f)),
            pl.BlockSpec((tf, d), lambda i, t, f: (f, 0)),
            pl.BlockSpec((FFN_CONV, tf), lambda i, t, f: (0, f)),
            pl.BlockSpec((1, tf), lambda i, t, f: (0, f)),
            pl.BlockSpec((bt, FFN_CONV - 1, tf), lambda i, t, f: (i, 0, f)),
            pl.BlockSpec(memory_space=pl.ANY),
            pl.BlockSpec((bt, 1, d), lambda i, t, f: (mrow + i, 0, 5)),
            pl.BlockSpec((1, 1, d), lambda i, t, f: (0, 0, 0)),
        ],
        out_specs=[
            pl.BlockSpec((bt, lt, d), lambda i, t, f: (i, t, 0)),
            pl.BlockSpec((bt, 1, FFN_CONV - 1, tf), lambda i, t, f: (i, t, 0, f)),
        ],
        out_shape=[
            jax.ShapeDtypeStruct((b, l, d), F32),
            jax.ShapeDtypeStruct((b, l // lt, FFN_CONV - 1, D_FF), F32),
        ],
        scratch_shapes=[
            pltpu.VMEM((bt, lt + FFN_PAD, tf), F32),
            pltpu.VMEM((bt, lt, d), F32),
            pltpu.SemaphoreType.DMA(()),
        ],
        compiler_params=_params("arbitrary", "arbitrary", "arbitrary",
                                vmem_limit_bytes=BIG_VMEM_LIMIT_BYTES),
        name="ffn",
    )(h2, h2, w_up16, w_up16, w_down16, conv_w, conv_b.reshape(1, D_FF), st0, x1, mod3,
      final_norm.reshape(1, 1, d))


def _trunk(x, mod3, mod_row0, states, weights, cfg):
    (w_in_t, w_small_t, norm1, gla_wg, gla_bg, gla_norm, gdn_conv_w, gate_vec, gdn_norm, w_out,
     norm2, w_up, w_down, ffn_conv_w, ffn_conv_b, final_norm) = weights
    s_gla, s_gdn, s_conv, s_ffn = states
    convert = w_up.dtype != BF16
    proj, small = _in_proj(x, mod3, mod_row0, norm1, w_in_t, w_small_t, cfg["in_bt"], cfg["in_lt"])
    o_a, n_gla, *gla_riders = _gla_mix(proj, small, s_gla, gla_wg, gla_bg, gla_norm, cfg["gla_bt"],
                                       cfg["gla_lt"], riders=[w_down] if convert else [])
    o_b, n_gdn, n_conv, *gdn_riders = _gdn_mix(proj, small, s_conv, s_gdn, gdn_conv_w, gate_vec, gdn_norm,
                                               cfg["gdn_bt"], cfg["gdn_lt"],
                                               riders=[w_up, w_out] if convert else [])
    w_out16, w_up16, w_down16 = (gdn_riders[1], gdn_riders[0], gla_riders[0]) if convert else (
        w_out, w_up, w_down)
    x1, h2 = _out_proj(x, o_a, o_b, mod3, mod_row0, w_out16, norm2, cfg["out_bt"], cfg["out_lt"])
    y, n_ffn = _ffn(h2, x1, s_ffn, mod3, mod_row0, w_up16, w_down16, ffn_conv_w, ffn_conv_b,
                    final_norm, cfg["ffn_bt"], cfg["ffn_lt"])
    return (y, n_gla[None], n_gdn[None], n_conv[None], n_ffn[:, -1][None]), (w_out16, w_up16, w_down16)


def kernel(x_prompt, x_sample, c_prompt, c_sample, state_gla, state_gdn, state_gdn_conv, state_ffn_conv, w_ada, b_ada, norm1, w_in, gla_wg, gla_bg, gla_norm, gdn_conv_w, gdn_a_log, gdn_dt_bias, gdn_norm, w_out, norm2, w_up, ffn_conv_w, ffn_conv_b, w_down, final_norm):
    bp = x_prompt.shape[0]
    bs = x_sample.shape[0]

    w_in_t = jnp.swapaxes(w_in[0], 0, 1)
    w_small_t = jnp.concatenate(
        [w_in_t[IN_OFF_LR:IN_OFF_GDN], w_in_t[IN_OFF_BETA:],
         jnp.zeros((SMALL_COLS - GLA_GATE_RANK - 2 * GDN_HEADS, D_MODEL), F32)], axis=0)
    gate_vec = jnp.zeros((2, SMALL_COLS), F32)
    gate_vec = gate_vec.at[0, ALPHA_OFF:ALPHA_OFF + GDN_HEADS].set(gdn_a_log[0])
    gate_vec = gate_vec.at[1, ALPHA_OFF:ALPHA_OFF + GDN_HEADS].set(gdn_dt_bias[0])

    c_all = jnp.concatenate([c_sample, c_prompt], axis=0)
    mod = _ada_mod(c_all, w_ada[0], b_ada[0])
    mod3 = mod.reshape(bs + bp, 1, N_MOD * D_MODEL)

    weights = (w_in_t, w_small_t, norm1[0], gla_wg[0], gla_bg[0], gla_norm[0], gdn_conv_w[0], gate_vec,
               gdn_norm[0], w_out[0], norm2[0], w_up[0], w_down[0], ffn_conv_w[0], ffn_conv_b[0],
               final_norm)

    fresh = (jnp.zeros((bp, GLA_HEADS, GLA_DK, GLA_DV), F32),
             jnp.zeros((bp, GDN_HEADS, GDN_DK, GDN_DV), F32),
             jnp.zeros((bp, GDN_CONV - 1, GDN_CONV_DIM), F32),
             jnp.zeros((bp, FFN_CONV - 1, D_FF), F32))
    cfg_p = dict(in_bt=1, in_lt=1024, gla_bt=1, gla_lt=512, gdn_bt=1, gdn_lt=256,
                 out_bt=1, out_lt=512, ffn_bt=1, ffn_lt=1024)
    (y_p, p_gla, p_gdn, p_conv, p_ffn), (w_out16, w_up16, w_down16) = _trunk(
        x_prompt, mod3, bs, fresh, weights, cfg_p)

    carried = (state_gla[0], state_gdn[0], state_gdn_conv[0], state_ffn_conv[0])
    ls = x_sample.shape[1]
    cfg_s = dict(in_bt=bs, in_lt=ls, gla_bt=4, gla_lt=ls, gdn_bt=4, gdn_lt=ls,
                 out_bt=bs // 2, out_lt=ls, ffn_bt=bs, ffn_lt=ls)
    weights_s = weights[:9] + (w_out16, weights[10], w_up16, w_down16) + weights[13:]
    (y_s, s_gla, s_gdn, s_conv, s_ffn), _ = _trunk(x_sample, mod3, 0, carried, weights_s, cfg_s)
    return (y_p, y_s, p_gla, p_gdn, p_conv, p_ffn, s_gla, s_gdn, s_conv, s_ffn)
```

```python
import functools

import jax
import jax.numpy as jnp
from jax import lax
from jax.experimental import pallas as pl
from jax.experimental.pallas import tpu as pltpu

F32 = jnp.float32
BF16 = jnp.bfloat16

D_MODEL = 2048
CHUNK = 64
GLA_HEADS = 4
GLA_DK = 128
GLA_DV = 256
GLA_WIDTH = GLA_HEADS * GLA_DV
GLA_GATE_RANK = 16
GLA_GATE_NORM = 16.0
GLA_MILD_LOG_DECAY = 60.0
GDN_HEADS = 8
GDN_DK = 128
GDN_DV = 128
GDN_WIDTH = GDN_HEADS * GDN_DV
GDN_CONV = 4
GDN_CONV_DIM = 3 * GDN_WIDTH
D_FF = 5632
FFN_CONV = 3
N_MOD = 6
EPS = 1e-6

GLA_COLS = 2 * GLA_HEADS * GLA_DK + 2 * GLA_WIDTH
GDN_COLS = 4 * GDN_WIDTH
MAIN_COLS = GLA_COLS + GDN_COLS
SMALL_COLS = 128
LR_OFF, BETA_OFF, ALPHA_OFF = 0, GLA_GATE_RANK, GLA_GATE_RANK + GDN_HEADS
IN_OFF_LR = GLA_COLS
IN_OFF_GDN = IN_OFF_LR + GLA_GATE_RANK
IN_OFF_BETA = IN_OFF_GDN + GDN_COLS

SUBLANES = 8
HAND_DMA_PRIORITY = 1
VMEM_LIMIT_BYTES = 56 * 1024 * 1024
BIG_VMEM_LIMIT_BYTES = 60 * 1024 * 1024

NT_DIMS = (((1,), (1,)), ((), ()))
TN_DIMS = (((0,), (0,)), ((), ()))


def _dot(a, b):
    return jnp.dot(a, b, preferred_element_type=F32)


def _dot_nt(a, b):
    return lax.dot_general(a, b, NT_DIMS, preferred_element_type=F32)


def _dot_tn(a, b):
    return lax.dot_general(a, b, TN_DIMS, preferred_element_type=F32)


def _dot_f32(a, b):
    return jnp.dot(a, b, preferred_element_type=F32, precision=lax.Precision.HIGHEST)


def _dot_many(a_list, b_list):
    return [_dot(a.astype(BF16), b.astype(BF16)) for a, b in zip(a_list, b_list)]


def _silu(x):
    return x * jax.nn.sigmoid(x)


def _softplus(x):
    return jnp.maximum(x, 0.0) + jnp.log1p(jnp.exp(-jnp.abs(x)))


def _log_sigmoid(x):
    return jnp.minimum(x, 0.0) - jnp.log1p(jnp.exp(-jnp.abs(x)))


def _params(*semantics, vmem_limit_bytes=VMEM_LIMIT_BYTES):
    return pltpu.CompilerParams(dimension_semantics=semantics, vmem_limit_bytes=vmem_limit_bytes)


def _rider_specs(rider, grid):
    steps = grid[0] * grid[1]
    rows, cols = rider.shape
    assert rows % steps == 0, (rows, steps)
    spec = pl.BlockSpec((rows // steps, cols), lambda i, t: (i * grid[1] + t, 0))
    return spec, spec, jax.ShapeDtypeStruct(rider.shape, BF16)


def _tri_masks():
    row = lax.broadcasted_iota(jnp.int32, (CHUNK, CHUNK), 0)
    col = lax.broadcasted_iota(jnp.int32, (CHUNK, CHUNK), 1)
    return row >= col, row > col


def _ada_kernel(c_ref, w_ref, b_ref, o_ref):
    s = _silu(c_ref[...]).astype(BF16)
    o_ref[...] = _dot(s, w_ref[...].astype(BF16)) + b_ref[...]


def _ada_mod(c_all, w_ada, b_ada):
    rows = c_all.shape[0]
    n = w_ada.shape[1]
    tn = 1024
    return pl.pallas_call(
        _ada_kernel,
        grid=(n // tn,),
        in_specs=[
            pl.BlockSpec((rows, D_MODEL), lambda j: (0, 0)),
            pl.BlockSpec((D_MODEL, tn), lambda j: (0, j)),
            pl.BlockSpec((1, tn), lambda j: (0, j)),
        ],
        out_specs=pl.BlockSpec((rows, tn), lambda j: (0, j)),
        out_shape=jax.ShapeDtypeStruct((rows, n), F32),
        compiler_params=_params("arbitrary"),
        name="ada_mod",
    )(c_all, w_ada, b_ada.reshape(1, n))


W_RING = 3


def _in_proj_kernel(x_ref, sc_ref, sh_ref, n1_ref, w_hbm, ws_ref, o_ref, os_ref, h_scr, w_ring, w_sems,
                    *, n_a):
    bt, lt, d = x_ref.shape
    tn = o_ref.shape[-1]
    n_t, n_j = pl.num_programs(1), pl.num_programs(2)
    j = pl.program_id(2)
    step = (pl.program_id(0) * n_t + pl.program_id(1)) * n_j + j
    n_steps = pl.num_programs(0) * n_t * n_j

    def tile_copy(s):
        jj = lax.rem(s, n_j)
        start = jnp.where(jj < n_a, jj * tn, IN_OFF_GDN + (jj - n_a) * tn)
        slot = lax.rem(s, W_RING)
        return pltpu.make_async_copy(
            w_hbm.at[pl.ds(pl.multiple_of(start, SUBLANES), tn), :], w_ring.at[slot], w_sems.at[slot])

    @pl.when(step == 0)
    def _():
        for s in range(W_RING - 1):
            tile_copy(s).start(priority=HAND_DMA_PRIORITY)

    @pl.when(step + (W_RING - 1) < n_steps)
    def _():
        tile_copy(step + (W_RING - 1)).start(priority=HAND_DMA_PRIORITY)

    @pl.when(j == 0)
    def _():
        x = x_ref[...]
        y = x * lax.rsqrt(jnp.mean(x * x, axis=-1, keepdims=True) + EPS) * n1_ref[...]
        h = y * (1.0 + sc_ref[...]) + sh_ref[...]
        hb = h.reshape(bt * lt, d).astype(BF16)
        h_scr[...] = hb
        os_ref[...] = _dot_nt(hb, ws_ref[...].astype(BF16)).reshape(os_ref.shape)

    tile_copy(step).wait()
    w = w_ring[lax.rem(step, W_RING)].astype(BF16)
    o_ref[...] = _dot_nt(h_scr[...], w).reshape(o_ref.shape)


def _in_proj(x, mod3, mod_row0, norm1, w_in_t, w_small_t, bt, lt):
    b, l, d = x.shape
    tn = 1024
    n_a = GLA_COLS // tn
    mrow = mod_row0 // bt
    grid = (b // bt, l // lt, MAIN_COLS // tn)
    assert grid[0] * grid[1] * grid[2] >= W_RING - 1
    return pl.pallas_call(
        functools.partial(_in_proj_kernel, n_a=n_a),
        grid=grid,
        in_specs=[
            pl.BlockSpec((bt, lt, d), lambda i, t, j: (i, t, 0)),
            pl.BlockSpec((bt, 1, d), lambda i, t, j: (mrow + i, 0, 1)),
            pl.BlockSpec((bt, 1, d), lambda i, t, j: (mrow + i, 0, 0)),
            pl.BlockSpec((1, 1, d), lambda i, t, j: (0, 0, 0)),
            pl.BlockSpec(memory_space=pl.ANY),
            pl.BlockSpec((SMALL_COLS, d), lambda i, t, j: (0, 0)),
        ],
        out_specs=[
            pl.BlockSpec((bt, lt, tn), lambda i, t, j: (i, t, j)),
            pl.BlockSpec((bt, lt, SMALL_COLS), lambda i, t, j: (i, t, 0)),
        ],
        out_shape=[
            jax.ShapeDtypeStruct((b, l, MAIN_COLS), F32),
            jax.ShapeDtypeStruct((b, l, SMALL_COLS), F32),
        ],
        scratch_shapes=[pltpu.VMEM((bt * lt, d), BF16),
                        pltpu.VMEM((W_RING, tn, d), F32),
                        pltpu.SemaphoreType.DMA((W_RING,))],
        compiler_params=_params("arbitrary", "arbitrary", "arbitrary",
                                vmem_limit_bytes=BIG_VMEM_LIMIT_BYTES),
        name="in_proj",
    )(x, mod3, mod3, norm1.reshape(1, 1, d), w_in_t, w_small_t)


def _gla_levels():
    row = lax.broadcasted_iota(jnp.int32, (CHUNK, CHUNK), 0)
    col = lax.broadcasted_iota(jnp.int32, (CHUNK, CHUNK), 1)
    ops, masks = [], []
    half = CHUNK // 2
    while half >= 1:
        shift = half.bit_length()
        parent_row = jnp.left_shift(jnp.right_shift(row, shift), shift)
        parent_col = jnp.left_shift(jnp.right_shift(col, shift), shift)
        m_row = parent_row + (half - 1)
        right_row = row > m_row
        in_span = (right_row & (col > m_row) & (col <= row)) | (
            jnp.logical_not(right_row) & (col > row) & (col <= m_row))
        ops.append(jnp.where(in_span, 1.0, 0.0))
        masks.append((parent_row == parent_col) & right_row & (col <= m_row))
        half //= 2
    return ops, masks, row == col


def _gla_kernel(q_ref, k_ref, v_ref, r_ref, sm_ref, wg_ref, bg_ref, gn_ref, s0_ref, *rest, n_riders):
    src_refs, (o_ref, s_ref), rest = rest[:n_riders], rest[n_riders:n_riders + 2], rest[n_riders + 2:]
    dst_refs, (st_scr, attn_scr) = rest[:n_riders], rest[n_riders:]
    for src_ref, dst_ref in zip(src_refs, dst_refs):
        dst_ref[...] = src_ref[...].astype(dst_ref.dtype)
    t = pl.program_id(1)
    bt, lt = q_ref.shape[0], q_ref.shape[1]
    n_chunks = lt // CHUNK
    causal, _ = _tri_masks()
    tri = causal.astype(F32)
    scale = GLA_DK ** -0.5

    @pl.when(t == 0)
    def _():
        for b in range(bt):
            for h in range(GLA_HEADS):
                st_scr[b, h] = s0_ref[b, h].T

    wg = wg_ref[...].astype(BF16)
    heads = range(GLA_HEADS)
    kcs = [slice(h * GLA_DK, (h + 1) * GLA_DK) for h in heads]
    vcs = [slice(h * GLA_DV, (h + 1) * GLA_DV) for h in heads]
    items = [(b, pl.ds(c * CHUNK, CHUNK)) for c in range(n_chunks) for b in range(bt)]

    def scaled_q(b, rows):
        return [q_ref[b, rows, kc] * scale for kc in kcs]

    log_a, big_g = [], []
    for b, rows in items:
        a_lr = sm_ref[b, rows, LR_OFF:LR_OFF + GLA_GATE_RANK].astype(BF16)
        la = _log_sigmoid(_dot(a_lr, wg) + bg_ref[...]) / GLA_GATE_NORM
        log_a.append(la)
        big_g.append(_dot_f32(tri, la))
    total = big_g[0][CHUNK - 1:CHUNK, :]
    for g in big_g[1:]:
        total = jnp.minimum(total, g[CHUNK - 1:CHUNK, :])
    mild = jnp.min(total) > -GLA_MILD_LOG_DECAY

    @pl.when(mild)
    def _():
        for n, (b, rows) in enumerate(items):
            g_mid = big_g[n][CHUNK // 2 - 1:CHUNK // 2, :]
            e_q = jnp.exp(big_g[n] - g_mid)
            e_k = jnp.exp(g_mid - big_g[n])
            q = scaled_q(b, rows)
            scores = [_dot_nt((qq * e_q[:, kc]).astype(BF16),
                              (k_ref[b, rows, kc] * e_k[:, kc]).astype(BF16))
                      for qq, kc in zip(q, kcs)]
            for h in heads:
                attn_scr[n, h] = jnp.where(causal, scores[h], 0.0)

    @pl.when(jnp.logical_not(mild))
    def _():
        level_ops, level_masks, diag = _gla_levels()
        level_op = jnp.concatenate(level_ops, axis=0)
        for n, (b, rows) in enumerate(items):
            sums = _dot_f32(level_op, log_a[n])
            q = scaled_q(b, rows)
            k = [k_ref[b, rows, kc] for kc in kcs]
            scores = [jnp.where(diag, _dot_nt(qq.astype(BF16), kk.astype(BF16)), 0.0)
                      for qq, kk in zip(q, k)]
            for i, mask in enumerate(level_masks):
                e = jnp.exp(sums[i * CHUNK:(i + 1) * CHUNK])
                part = [_dot_nt((qq * e[:, kc]).astype(BF16), (kk * e[:, kc]).astype(BF16))
                        for qq, kk, kc in zip(q, k, kcs)]
                scores = [jnp.where(mask, p, a) for p, a in zip(part, scores)]
            for h in heads:
                attn_scr[n, h] = scores[h]

    pending = []
    for n, (b, rows) in enumerate(items):
        g_last = big_g[n][CHUNK - 1:CHUNK, :]
        e_g = jnp.exp(big_g[n])
        e_kl = jnp.exp(g_last - big_g[n])
        e_l = jnp.exp(g_last)
        q = scaled_q(b, rows)
        v = [v_ref[b, rows, vc].astype(BF16) for vc in vcs]
        v_k = [_dot_tn(vv, (k_ref[b, rows, kc] * e_kl[:, kc]).astype(BF16)) for vv, kc in zip(v, kcs)]
        a_v = [_dot(attn_scr[n, h].astype(BF16), v[h]) for h in heads]
        q_g = [(qq * e_g[:, kc]).astype(BF16) for qq, kc in zip(q, kcs)]
        pending.append((q_g, a_v, v_k, [e_l[:, kc] for kc in kcs]))
    for c in range(n_chunks):
        group = range(c * bt, (c + 1) * bt)
        st = [[st_scr[items[n][0], h] for h in heads] for n in group]
        q_s = [[_dot_nt(qq, ss.astype(BF16)) for qq, ss in zip(pending[n][0], st_n)]
               for n, st_n in zip(group, st)]
        for n, st_n, q_s_n in zip(group, st, q_s):
            b, rows = items[n]
            _, a_v, v_k, e_l = pending[n]
            for h in heads:
                st_scr[b, h] = st_n[h] * e_l[h] + v_k[h]
                o = a_v[h] + q_s_n[h]
                o = o * lax.rsqrt(jnp.mean(o * o, axis=-1, keepdims=True) + EPS) * gn_ref[...]
                o = o * _silu(r_ref[b, rows, vcs[h]])
                o_ref[b, rows, vcs[h]] = o.astype(o_ref.dtype)

    @pl.when(t == pl.num_programs(1) - 1)
    def _():
        for b in range(bt):
            for h in range(GLA_HEADS):
                s_ref[b, h] = st_scr[b, h].T


def _gla_mix(proj, small, s0, gla_wg, gla_bg, gla_norm, bt, lt, riders=()):
    b, l, _ = proj.shape
    kw = GLA_HEADS * GLA_DK
    grid = (b // bt, l // lt)
    rider_args = list(riders)
    riders = [_rider_specs(r, grid) for r in rider_args]
    return pl.pallas_call(
        functools.partial(_gla_kernel, n_riders=len(riders)),
        grid=grid,
        in_specs=[
            pl.BlockSpec((bt, lt, kw), lambda i, t: (i, t, 0)),
            pl.BlockSpec((bt, lt, kw), lambda i, t: (i, t, 1)),
            pl.BlockSpec((bt, lt, GLA_WIDTH), lambda i, t: (i, t, 1)),
            pl.BlockSpec((bt, lt, GLA_WIDTH), lambda i, t: (i, t, 2)),
            pl.BlockSpec((bt, lt, SMALL_COLS), lambda i, t: (i, t, 0)),
            pl.BlockSpec((GLA_GATE_RANK, kw), lambda i, t: (0, 0)),
            pl.BlockSpec((1, kw), lambda i, t: (0, 0)),
            pl.BlockSpec((1, GLA_DV), lambda i, t: (0, 0)),
            pl.BlockSpec((bt, GLA_HEADS, GLA_DK, GLA_DV), lambda i, t: (i, 0, 0, 0)),
        ] + [r[0] for r in riders],
        out_specs=[
            pl.BlockSpec((bt, lt, GLA_WIDTH), lambda i, t: (i, t, 0)),
            pl.BlockSpec((bt, GLA_HEADS, GLA_DK, GLA_DV), lambda i, t: (i, 0, 0, 0)),
        ] + [r[1] for r in riders],
        out_shape=[
            jax.ShapeDtypeStruct((b, l, GLA_WIDTH), BF16),
            jax.ShapeDtypeStruct((b, GLA_HEADS, GLA_DK, GLA_DV), F32),
        ] + [r[2] for r in riders],
        scratch_shapes=[pltpu.VMEM((bt, GLA_HEADS, GLA_DV, GLA_DK), F32),
                        pltpu.VMEM((bt * (lt // CHUNK), GLA_HEADS, CHUNK, CHUNK), F32)],
        compiler_params=_params("arbitrary", "arbitrary"),
        name="gla_mix",
    )(proj, proj, proj, proj, small, gla_wg, gla_bg.reshape(1, kw), gla_norm.reshape(1, GLA_DV), s0,
      *rider_args)


CONV_PAD = SUBLANES
INV_BASE_LOG2 = 3
GDN_CHUNK_GROUP = 4


def _inverse_masks():
    row = lax.broadcasted_iota(jnp.int32, (CHUNK, CHUNK), 0)
    col = lax.broadcasted_iota(jnp.int32, (CHUNK, CHUNK), 1)

    def same_block(log2):
        return jnp.right_shift(row, log2) == jnp.right_shift(col, log2)

    base = same_block(INV_BASE_LOG2)
    merges = []
    log2 = INV_BASE_LOG2
    while (1 << log2) < CHUNK:
        merges.append(same_block(log2 + 1) & jnp.logical_not(same_block(log2)))
        log2 += 1
    return base, merges


def _unit_lower_inverse_minus_eye(lower, base, merges):
    neg = [jnp.where(base, -l, 0.0) for l in lower]
    x = neg
    p = neg
    for _ in range(INV_BASE_LOG2 - 1):
        p = _dot_many(p, p)
        xp = _dot_many(x, p)
        x = [a + b + c for a, b, c in zip(x, p, xp)]
    for m in merges:
        c = [jnp.where(m, l, 0.0) for l in lower]
        w = [a + b for a, b in zip(c, _dot_many(x, c))]
        wx = _dot_many(w, x)
        x = [a - (b + d) for a, b, d in zip(x, w, wx)]
    return x


def _gdn_kernel(x_ref, g_ref, sm_ref, cw_ref, ga_ref, gn_ref, c0_ref, s0_ref, *rest, n_riders):
    src_refs, (o_ref, s_ref, c_ref), rest = rest[:n_riders], rest[n_riders:n_riders + 3], rest[n_riders + 3:]
    dst_refs, (xbuf,) = rest[:n_riders], rest[n_riders:]
    for src_ref, dst_ref in zip(src_refs, dst_refs):
        dst_ref[...] = src_ref[...].astype(dst_ref.dtype)
    t = pl.program_id(1)
    bt, lt = x_ref.shape[0], x_ref.shape[1]
    causal, strict = _tri_masks()
    tri = causal.astype(F32)
    inv_base, inv_merges = _inverse_masks()
    scale = GDN_DK ** -0.5

    @pl.when(t == 0)
    def _():
        xbuf[:, 0:CONV_PAD, :] = jnp.zeros((bt, CONV_PAD, GDN_CONV_DIM), F32)
        xbuf[:, CONV_PAD - (GDN_CONV - 1):CONV_PAD, :] = c0_ref[...]
        s_ref[...] = s0_ref[...]

    @pl.when(t > 0)
    def _():
        xbuf[:, 0:CONV_PAD, :] = xbuf[:, lt:lt + CONV_PAD, :]

    xbuf[:, CONV_PAD:CONV_PAD + lt, :] = x_ref[...]
    c_ref[...] = xbuf[:, lt + CONV_PAD - (GDN_CONV - 1):lt + CONV_PAD, :]

    def conv_silu(b, r0, cols):
        acc = cw_ref[GDN_CONV - 1:GDN_CONV, cols] * xbuf[b, r0 + CONV_PAD:r0 + CONV_PAD + CHUNK, cols]
        for tap in range(GDN_CONV - 1):
            start = r0 + CONV_PAD - (GDN_CONV - 1 - tap)
            acc = acc + cw_ref[tap:tap + 1, cols] * xbuf[b, start:start + CHUNK, cols]
        return _silu(acc)

    def l2norm(x):
        return x * lax.rsqrt(jnp.sum(x * x, axis=-1, keepdims=True) + EPS)

    heads = range(GDN_HEADS)
    def prep_gates(b, c):
        sm = sm_ref[b, pl.ds(c * CHUNK, CHUNK), :]
        gate = -jnp.exp(ga_ref[0:1, :]) * _softplus(sm + ga_ref[1:2, :])
        big_g = _dot_f32(tri, gate)
        return jax.nn.sigmoid(sm), big_g, big_g.T

    def prep_head(b, c, gates, h):
        beta, big_g, big_gt = gates
        r0 = c * CHUNK
        g_col = big_g[:, ALPHA_OFF + h:ALPHA_OFF + h + 1]
        g_row = big_gt[ALPHA_OFF + h:ALPHA_OFF + h + 1, :]
        b_col = beta[:, BETA_OFF + h:BETA_OFF + h + 1]
        g_last = g_col[CHUNK - 1:CHUNK, :]
        e_g = jnp.exp(g_col)
        q = l2norm(conv_silu(b, r0, slice(h * GDN_DK, (h + 1) * GDN_DK))) * scale
        k = l2norm(conv_silu(b, r0, slice(GDN_WIDTH + h * GDN_DK, GDN_WIDTH + (h + 1) * GDN_DK)))
        v = conv_silu(b, r0, slice(2 * GDN_WIDTH + h * GDN_DV, 2 * GDN_WIDTH + (h + 1) * GDN_DV))
        kb = k * b_col
        return dict(
            decay=jnp.where(causal, jnp.exp(g_col - g_row), 0.0),
            rhs=jnp.concatenate([v * b_col, kb * e_g], axis=-1),
            q16=q.astype(BF16), k16=k.astype(BF16), kb16=kb.astype(BF16),
            q_g=(q * e_g).astype(BF16),
            k_dec=(k * jnp.exp(g_last - g_col)).astype(BF16),
            e_last=jnp.exp(g_last))

    def key_products(p):
        kk_t = [_dot_nt(ph["kb16"], ph["k16"]) for ph in p]
        qk_t = [_dot_nt(ph["q16"], ph["k16"]) for ph in p]
        lower = [jnp.where(strict, m * ph["decay"], 0.0) for m, ph in zip(kk_t, p)]
        attn = [jnp.where(causal, m * ph["decay"], 0.0).astype(BF16) for m, ph in zip(qk_t, p)]
        return lower, attn

    n_chunks = lt // CHUNK
    items = [(b, c) for c in range(n_chunks) for b in range(bt)]
    pending = []
    for i0 in range(0, len(items), GDN_CHUNK_GROUP):
        group = items[i0:i0 + GDN_CHUNK_GROUP]
        gates = [prep_gates(b, c) for b, c in group]
        p = [prep_head(b, c, g, h) for (b, c), g in zip(group, gates) for h in heads]
        lower, attn = key_products(p)
        xinv = _unit_lower_inverse_minus_eye(lower, inv_base, inv_merges)
        rhs = [ph["rhs"] for ph in p]
        sol = [r + xr for r, xr in zip(rhs, _dot_many(xinv, rhs))]
        for n in range(len(group)):
            sl = slice(n * GDN_HEADS, (n + 1) * GDN_HEADS)
            pending.append((attn[sl], [so[:, :GDN_DV] for so in sol[sl]],
                            [so[:, GDN_DV:].astype(BF16) for so in sol[sl]],
                            [ph["q_g"] for ph in p[sl]], [ph["k_dec"] for ph in p[sl]],
                            [ph["e_last"] for ph in p[sl]]))
    for c in range(n_chunks):
        group = range(c * bt, (c + 1) * bt)
        attn, sol_v, sol_k, q_g, k_dec, e_last = (
            [x for n in group for x in pending[n][field]] for field in range(6))
        where = [(items[n][0], h) for n in group for h in heads]
        s = [s_ref[b, h] for b, h in where]
        s16 = [ss.astype(BF16) for ss in s]
        k_s = [_dot(a, ss) for a, ss in zip(sol_k, s16)]
        q_s = [_dot(a, ss) for a, ss in zip(q_g, s16)]
        u16 = [(sv - ks).astype(BF16) for sv, ks in zip(sol_v, k_s)]
        a_u = [_dot(a, uu) for a, uu in zip(attn, u16)]
        k_u = [_dot_tn(kd, uu) for kd, uu in zip(k_dec, u16)]
        rows = pl.ds(c * CHUNK, CHUNK)
        for i, (b, h) in enumerate(where):
            hc = slice(h * GDN_DV, (h + 1) * GDN_DV)
            s_ref[b, h] = e_last[i] * s[i] + k_u[i]
            o = q_s[i] + a_u[i]
            o = o * lax.rsqrt(jnp.mean(o * o, axis=-1, keepdims=True) + EPS) * gn_ref[...]
            o = o * _silu(g_ref[b, rows, hc])
            o_ref[b, rows, hc] = o.astype(o_ref.dtype)


def _gdn_mix(proj, small, c0, s0, conv_w, gate_vec, gdn_norm, bt, lt, riders=()):
    b, l, _ = proj.shape
    grid = (b // bt, l // lt)
    rider_args = list(riders)
    riders = [_rider_specs(r, grid) for r in rider_args]
    return pl.pallas_call(
        functools.partial(_gdn_kernel, n_riders=len(riders)),
        grid=grid,
        in_specs=[
            pl.BlockSpec((bt, lt, GDN_CONV_DIM), lambda i, t: (i, t, 1)),
            pl.BlockSpec((bt, lt, GDN_WIDTH), lambda i, t: (i, t, 6)),
            pl.BlockSpec((bt, lt, SMALL_COLS), lambda i, t: (i, t, 0)),
            pl.BlockSpec((GDN_CONV, GDN_CONV_DIM), lambda i, t: (0, 0)),
            pl.BlockSpec((2, SMALL_COLS), lambda i, t: (0, 0)),
            pl.BlockSpec((1, GDN_DV), lambda i, t: (0, 0)),
            pl.BlockSpec((bt, GDN_CONV - 1, GDN_CONV_DIM), lambda i, t: (i, 0, 0)),
            pl.BlockSpec((bt, GDN_HEADS, GDN_DK, GDN_DV), lambda i, t: (i, 0, 0, 0)),
        ] + [r[0] for r in riders],
        out_specs=[
            pl.BlockSpec((bt, lt, GDN_WIDTH), lambda i, t: (i, t, 0)),
            pl.BlockSpec((bt, GDN_HEADS, GDN_DK, GDN_DV), lambda i, t: (i, 0, 0, 0)),
            pl.BlockSpec((bt, GDN_CONV - 1, GDN_CONV_DIM), lambda i, t: (i, 0, 0)),
        ] + [r[1] for r in riders],
        out_shape=[
            jax.ShapeDtypeStruct((b, l, GDN_WIDTH), BF16),
            jax.ShapeDtypeStruct((b, GDN_HEADS, GDN_DK, GDN_DV), F32),
            jax.ShapeDtypeStruct((b, GDN_CONV - 1, GDN_CONV_DIM), F32),
        ] + [r[2] for r in riders],
        scratch_shapes=[pltpu.VMEM((bt, lt + CONV_PAD, GDN_CONV_DIM), F32)],
        compiler_params=_params("arbitrary", "arbitrary"),
        name="gdn_mix",
    )(proj, proj, small, conv_w, gate_vec, gdn_norm.reshape(1, GDN_DV), c0, s0, *rider_args)


def _out_proj_kernel(x_ref, oa_ref, ob_ref, w_ref, g1_ref, sc_ref, sh_ref, n2_ref, x1_ref, h2_ref,
                     *, n_sub):
    bt, lt, d = x_ref.shape
    if bt == 1:
        ls = lt // n_sub
        subs = [(slice(0, 1), slice(i * ls, (i + 1) * ls)) for i in range(n_sub)]
    else:
        bs = bt // n_sub
        subs = [(slice(i * bs, (i + 1) * bs), slice(0, lt)) for i in range(n_sub)]
    mixes = []
    for bsl, lsl in subs:
        oa = oa_ref[bsl, lsl, :]
        nb, nl, _ = oa.shape
        oa = oa.reshape(nb * nl, GLA_WIDTH)
        ob = ob_ref[bsl, lsl, :].reshape(nb * nl, GDN_WIDTH)
        mix = _dot(oa, w_ref[0:GLA_WIDTH, :]) + _dot(ob, w_ref[GLA_WIDTH:GLA_WIDTH + GDN_WIDTH, :])
        mixes.append(mix.reshape(nb, nl, d))
    for (bsl, lsl), mix in zip(subs, mixes):
        x1 = x_ref[bsl, lsl, :] + g1_ref[bsl] * mix
        x1_ref[bsl, lsl, :] = x1
        y = x1 * lax.rsqrt(jnp.mean(x1 * x1, axis=-1, keepdims=True) + EPS) * n2_ref[...]
        h2_ref[bsl, lsl, :] = (y * (1.0 + sc_ref[bsl]) + sh_ref[bsl]).astype(h2_ref.dtype)


def _out_proj(x, o_a, o_b, mod3, mod_row0, w_out16, norm2, bt, lt):
    b, l, d = x.shape
    mrow = mod_row0 // bt
    return pl.pallas_call(
        functools.partial(_out_proj_kernel, n_sub=2),
        grid=(b // bt, l // lt),
        in_specs=[
            pl.BlockSpec((bt, lt, d), lambda i, t: (i, t, 0)),
            pl.BlockSpec((bt, lt, GLA_WIDTH), lambda i, t: (i, t, 0)),
            pl.BlockSpec((bt, lt, GDN_WIDTH), lambda i, t: (i, t, 0)),
            pl.BlockSpec((GLA_WIDTH + GDN_WIDTH, d), lambda i, t: (0, 0)),
            pl.BlockSpec((bt, 1, d), lambda i, t: (mrow + i, 0, 2)),
            pl.BlockSpec((bt, 1, d), lambda i, t: (mrow + i, 0, 4)),
            pl.BlockSpec((bt, 1, d), lambda i, t: (mrow + i, 0, 3)),
            pl.BlockSpec((1, 1, d), lambda i, t: (0, 0, 0)),
        ],
        out_specs=[
            pl.BlockSpec((bt, lt, d), lambda i, t: (i, t, 0)),
            pl.BlockSpec((bt, lt, d), lambda i, t: (i, t, 0)),
        ],
        out_shape=[
            jax.ShapeDtypeStruct((b, l, d), F32),
            jax.ShapeDtypeStruct((b, l, d), BF16),
        ],
        compiler_params=_params("arbitrary", "arbitrary"),
        name="out_proj",
    )(x, o_a, o_b, w_out16, mod3, mod3, mod3, norm2.reshape(1, 1, d))


FFN_PAD = SUBLANES
HALO_ROWS = 2 * SUBLANES


def _ffn_kernel(h_ref, halo_ref, wg_ref, wv_ref, wd_ref, cw_ref, cb_ref, st_ref, x1_hbm, g2_ref,
                fn_ref, y_ref, so_ref, gbuf, x1_buf, x1_sem, *, has_halo, n_sub):
    ib = pl.program_id(0)
    t = pl.program_id(1)
    f = pl.program_id(2)
    bt, lt, d = h_ref.shape
    tf = wg_ref.shape[1]
    x1_copy = pltpu.make_async_copy(
        x1_hbm.at[pl.ds(ib * bt, bt), pl.ds(t * lt, lt), :], x1_buf, x1_sem)
    if bt == 1:
        ls = lt // n_sub
        subs = [(slice(0, 1), slice(i * ls, (i + 1) * ls)) for i in range(n_sub)]
    else:
        bs = bt // n_sub
        subs = [(slice(i * bs, (i + 1) * bs), slice(0, lt)) for i in range(n_sub)]

    @pl.when(f == 0)
    def _():
        x1_copy.start(priority=HAND_DMA_PRIORITY)
        y_ref[...] = jnp.zeros(y_ref.shape, F32)

    prev = st_ref[...]
    gates, vals = [], []
    for n, (bsl, lsl) in enumerate(subs):
        hb = h_ref[bsl, lsl, :]
        nb, nl, _ = hb.shape
        hb = hb.reshape(nb * nl, d)
        if has_halo and n == 0:
            ext = _dot(jnp.concatenate([halo_ref[0], hb], axis=0), wg_ref[...])
            halo_gate = ext[HALO_ROWS - (FFN_CONV - 1):HALO_ROWS, :]
            prev = jnp.where(t == 0, prev, halo_gate[None])
            gate = ext[HALO_ROWS:, :].reshape(nb, nl, tf)
        else:
            gate = _dot(hb, wg_ref[...]).reshape(nb, nl, tf)
        gbuf[bsl, FFN_PAD + lsl.start:FFN_PAD + lsl.stop, :] = gate
        gates.append(gate)
        vals.append(_dot(hb, wv_ref[...]))
    gbuf[:, FFN_PAD - (FFN_CONV - 1):FFN_PAD, :] = prev
    so_ref[...] = gbuf[:, FFN_PAD + lt - (FFN_CONV - 1):FFN_PAD + lt, :].reshape(so_ref.shape)
    acts = []
    for (bsl, lsl), gate, val in zip(subs, gates, vals):
        nb, nl, _ = gate.shape
        conv = cw_ref[FFN_CONV - 1:FFN_CONV, :] * gate
        for tap in range(FFN_CONV - 1):
            back = FFN_CONV - 1 - tap
            conv = conv + cw_ref[tap:tap + 1, :] * gbuf[bsl, FFN_PAD + lsl.start - back:FFN_PAD + lsl.stop - back, :]
        acts.append((_silu(conv + cb_ref[...]).reshape(nb * nl, tf) * val).astype(BF16))
    parts = [_dot(act, wd_ref[...]) for act in acts]
    for (bsl, lsl), gate, part in zip(subs, gates, parts):
        nb, nl, _ = gate.shape
        y_ref[bsl, lsl, :] += part.reshape(nb, nl, d)

    @pl.when(f == pl.num_programs(2) - 1)
    def _():
        x1_copy.wait()
        x2 = x1_buf[...] + g2_ref[...] * y_ref[...]
        y = x2 * lax.rsqrt(jnp.mean(x2 * x2, axis=-1, keepdims=True) + EPS) * fn_ref[...]
        y_ref[...] = y


def _ffn(h2, x1, st0, mod3, mod_row0, w_up16, w_down16, conv_w, conv_b, final_norm, bt, lt):
    b, l, d = h2.shape
    tf = 512
    nf = D_FF // tf
    mrow = mod_row0 // bt
    has_halo = lt < l
    halo_blocks = lt // HALO_ROWS
    kern = functools.partial(_ffn_kernel, has_halo=has_halo, n_sub=4)
    return pl.pallas_call(
        kern,
        grid=(b // bt, l // lt, nf),
        in_specs=[
            pl.BlockSpec((bt, lt, d), lambda i, t, f: (i, t, 0)),
            pl.BlockSpec((1, HALO_ROWS, d), lambda i, t, f: (i, jnp.maximum(t * halo_blocks - 1, 0), 0)),
            pl.BlockSpec((d, tf), lambda i, t, f: (0, f)),
            pl.BlockSpec((d, tf), lambda i, t, f: (0, nf + f)),
            pl.BlockSpec((tf, d), lambda i, t, f: (f, 0)),
            pl.BlockSpec((FFN_CONV, tf), lambda i, t, f: (0, f)),
            pl.BlockSpec((1, tf), lambda i, t, f: (0, f)),
            pl.BlockSpec((bt, FFN_CONV - 1, tf), lambda i, t, f: (i, 0, f)),
            pl.BlockSpec(memory_space=pl.ANY),
            pl.BlockSpec((bt, 1, d), lambda i, t, f: (mrow + i, 0, 5)),
            pl.BlockSpec((1, 1, d), lambda i, t, f: (0, 0, 0)),
        ],
        out_specs=[
            pl.BlockSpec((bt, lt, d), lambda i, t, f: (i, t, 0)),
            pl.BlockSpec((bt, 1, FFN_CONV - 1, tf), lambda i, t, f: (i, t, 0, f)),
        ],
        out_shape=[
            jax.ShapeDtypeStruct((b, l, d), F32),
            jax.ShapeDtypeStruct((b, l // lt, FFN_CONV - 1, D_FF), F32),
        ],
        scratch_shapes=[
            pltpu.VMEM((bt, lt + FFN_PAD, tf), F32),
            pltpu.VMEM((bt, lt, d), F32),
            pltpu.SemaphoreType.DMA(()),
        ],
        compiler_params=_params("arbitrary", "arbitrary", "arbitrary",
                                vmem_limit_bytes=BIG_VMEM_LIMIT_BYTES),
        name="ffn",
    )(h2, h2, w_up16, w_up16, w_down16, conv_w, conv_b.reshape(1, D_FF), st0, x1, mod3,
      final_norm.reshape(1, 1, d))


def _trunk(x, mod3, mod_row0, states, weights, cfg):
    (w_in_t, w_small_t, norm1, gla_wg, gla_bg, gla_norm, gdn_conv_w, gate_vec, gdn_norm, w_out,
     norm2, w_up, w_down, ffn_conv_w, ffn_conv_b, final_norm) = weights
    s_gla, s_gdn, s_conv, s_ffn = states
    convert = w_up.dtype != BF16
    proj, small = _in_proj(x, mod3, mod_row0, norm1, w_in_t, w_small_t, cfg["in_bt"], cfg["in_lt"])
    o_a, n_gla, *gla_riders = _gla_mix(proj, small, s_gla, gla_wg, gla_bg, gla_norm, cfg["gla_bt"],
                                       cfg["gla_lt"], riders=[w_down] if convert else [])
    o_b, n_gdn, n_conv, *gdn_riders = _gdn_mix(proj, small, s_conv, s_gdn, gdn_conv_w, gate_vec, gdn_norm,
                                               cfg["gdn_bt"], cfg["gdn_lt"],
                                               riders=[w_up, w_out] if convert else [])
    w_out16, w_up16, w_down16 = (gdn_riders[1], gdn_riders[0], gla_riders[0]) if convert else (
        w_out, w_up, w_down)
    x1, h2 = _out_proj(x, o_a, o_b, mod3, mod_row0, w_out16, norm2, cfg["out_bt"], cfg["out_lt"])
    y, n_ffn = _ffn(h2, x1, s_ffn, mod3, mod_row0, w_up16, w_down16, ffn_conv_w, ffn_conv_b,
                    final_norm, cfg["ffn_bt"], cfg["ffn_lt"])
    return (y, n_gla[None], n_gdn[None], n_conv[None], n_ffn[:, -1][None]), (w_out16, w_up16, w_down16)


def kernel(x_prompt, x_sample, c_prompt, c_sample, state_gla, state_gdn, state_gdn_conv, state_ffn_conv, w_ada, b_ada, norm1, w_in, gla_wg, gla_bg, gla_norm, gdn_conv_w, gdn_a_log, gdn_dt_bias, gdn_norm, w_out, norm2, w_up, ffn_conv_w, ffn_conv_b, w_down, final_norm):
    bp = x_prompt.shape[0]
    bs = x_sample.shape[0]

    w_in_t = jnp.swapaxes(w_in[0], 0, 1)
    w_small_t = jnp.concatenate(
        [w_in_t[IN_OFF_LR:IN_OFF_GDN], w_in_t[IN_OFF_BETA:],
         jnp.zeros((SMALL_COLS - GLA_GATE_RANK - 2 * GDN_HEADS, D_MODEL), F32)], axis=0)
    gate_vec = jnp.zeros((2, SMALL_COLS), F32)
    gate_vec = gate_vec.at[0, ALPHA_OFF:ALPHA_OFF + GDN_HEADS].set(gdn_a_log[0])
    gate_vec = gate_vec.at[1, ALPHA_OFF:ALPHA_OFF + GDN_HEADS].set(gdn_dt_bias[0])

    c_all = jnp.concatenate([c_sample, c_prompt], axis=0)
    mod = _ada_mod(c_all, w_ada[0], b_ada[0])
    mod3 = mod.reshape(bs + bp, 1, N_MOD * D_MODEL)

    weights = (w_in_t, w_small_t, norm1[0], gla_wg[0], gla_bg[0], gla_norm[0], gdn_conv_w[0], gate_vec,
               gdn_norm[0], w_out[0], norm2[0], w_up[0], w_down[0], ffn_conv_w[0], ffn_conv_b[0],
               final_norm)

    fresh = (jnp.zeros((bp, GLA_HEADS, GLA_DK, GLA_DV), F32),
             jnp.zeros((bp, GDN_HEADS, GDN_DK, GDN_DV), F32),
             jnp.zeros((bp, GDN_CONV - 1, GDN_CONV_DIM), F32),
             jnp.zeros((bp, FFN_CONV - 1, D_FF), F32))
    cfg_p = dict(in_bt=1, in_lt=1024, gla_bt=1, gla_lt=512, gdn_bt=1, gdn_lt=256,
                 out_bt=1, out_lt=512, ffn_bt=1, ffn_lt=1024)
    (y_p, p_gla, p_gdn, p_conv, p_ffn), (w_out16, w_up16, w_down16) = _trunk(
        x_prompt, mod3, bs, fresh, weights, cfg_p)

    carried = (state_gla[0], state_gdn[0], state_gdn_conv[0], state_ffn_conv[0])
    ls = x_sample.shape[1]
    cfg_s = dict(in_bt=bs, in_lt=ls, gla_bt=4, gla_lt=ls, gdn_bt=4, gdn_lt=ls,
                 out_bt=bs // 2, out_lt=ls, ffn_bt=bs, ffn_lt=ls)
    weights_s = weights[:9] + (w_out16, weights[10], w_up16, w_down16) + weights[13:]
    (y_s, s_gla, s_gdn, s_conv, s_ffn), _ = _trunk(x_sample, mod3, 0, carried, weights_s, cfg_s)
    return (y_p, y_s, p_gla, p_gdn, p_conv, p_ffn, s_gla, s_gdn, s_conv, s_ffn)
```
